```python
import math
import jax, jax.numpy as jnp
from jax import lax
import numpy as np

D_MODEL = 1024
BATCH = 8
SEQ = 2048
DEPTH = 2
DEC_BATCH = 128
DEC_SEQ = 1
PAST_LEN = 16384
PAGE_SIZE = 128

DN_ALPHA = (2 * DEPTH) ** 0.25
DN_BETA = (8 * DEPTH) ** -0.25
LN_EPS = 1e-5
GDN_HEADS = 4
GDN_DK = 128
GDN_DV = 128
GDN_WIDTH = GDN_HEADS * GDN_DK
SCONV_W = 4
GDN_CHUNK = 64
CC_CH = D_MODEL // 2
CC_W = 31
GM_WIDTH = D_MODEL
GM_GROUPS = 4
GM_CHUNK = 128
N_MEM = 256
XA_HEADS = 4
XA_HEAD_DIM = D_MODEL // XA_HEADS
D_FF = 2816
AB_IN = 4 * GDN_WIDTH + 2 * GDN_HEADS + 2 * CC_CH

kernel_name = "hybrid_gdn_conformer_gmlp_decoder_step"


def layer_norm(x, g, b):
    xf = x.astype(jnp.float32)
    mu = jnp.mean(xf, -1, keepdims=True)
    var = jnp.mean(jnp.square(xf - mu), -1, keepdims=True)
    return ((xf - mu) * lax.rsqrt(var + LN_EPS) * g.astype(jnp.float32) + b.astype(jnp.float32)).astype(x.dtype)


def rms_norm(x, g):
    xf = x.astype(jnp.float32)
    return xf * lax.rsqrt(jnp.mean(xf * xf, -1, keepdims=True) + 1e-6) * g.astype(jnp.float32)


def l2_normalize(x):
    xf = x.astype(jnp.float32)
    return xf * lax.rsqrt(jnp.sum(xf * xf, -1, keepdims=True) + 1e-6)


def swiglu(x, wg, wu, wd):
    return (jax.nn.silu(x @ wg) * (x @ wu)) @ wd


def causal_depthwise_conv(x, buf, w):
    xx = jnp.concatenate([buf.astype(x.dtype), x], axis=1)
    y = lax.conv_general_dilated(xx, w[:, None, :].astype(x.dtype), (1,), 'VALID',
                                 dimension_numbers=('NWC', 'WIO', 'NWC'),
                                 feature_group_count=x.shape[-1])
    return y, xx[:, xx.shape[1] - (w.shape[0] - 1):]


def gated_delta_rule(q, k, v, g, beta, S0):
    N, L, H, dk = k.shape
    dv = v.shape[-1]
    c = min(GDN_CHUNK, L)
    pad = (-L) % c
    f32 = jnp.float32
    q, k, v, g, beta = (t.astype(f32) for t in (q, k, v, g, beta))
    if pad:
        q, k, v = (jnp.pad(t, ((0, 0), (0, pad), (0, 0), (0, 0))) for t in (q, k, v))
        g, beta = (jnp.pad(t, ((0, 0), (0, pad), (0, 0))) for t in (g, beta))
    n = (L + pad) // c
    chunks = lambda t: t.reshape(N, n, c, H, -1).transpose(0, 3, 1, 2, 4)
    q, k, v = chunks(q), chunks(k), chunks(v)
    g = jnp.cumsum(g.reshape(N, n, c, H).transpose(0, 3, 1, 2), axis=-1)
    beta = beta.reshape(N, n, c, H).transpose(0, 3, 1, 2)
    causal = jnp.tril(jnp.ones((c, c), bool))
    strict = jnp.tril(jnp.ones((c, c), bool), -1)
    diff = g[..., :, None] - g[..., None, :]
    decay = jnp.where(causal, jnp.exp(jnp.where(causal, diff, 0.0)), 0.0)
    kb = k * beta[..., None]
    M = jnp.where(strict, jnp.einsum('bhnid,bhnjd->bhnij', kb, k) * decay, 0.0)
    A = M + jnp.eye(c, dtype=f32)
    rhs = jnp.concatenate([v * beta[..., None], kb * jnp.exp(g)[..., None]], axis=-1)
    sol = lax.linalg.triangular_solve(A, rhs, left_side=True, lower=True, unit_diagonal=True)
    u, w = sol[..., :dv], sol[..., dv:]
    qk = jnp.einsum('bhnid,bhnjd->bhnij', q, k) * decay
    q_g = q * jnp.exp(g)[..., None]
    k_tail = k * jnp.exp(g[..., -1:] - g)[..., None]
    g_last = jnp.exp(g[..., -1])

    def step(S, inp):
        qg_i, kt_i, u_i, w_i, qk_i, gl_i = inp
        v_new = u_i - jnp.einsum('bhcd,bhde->bhce', w_i, S)
        o = jnp.einsum('bhcd,bhde->bhce', qg_i, S) + jnp.einsum('bhij,bhje->bhie', qk_i, v_new)
        S = S * gl_i[..., None, None] + jnp.einsum('bhcd,bhce->bhde', kt_i, v_new)
        return S, o

    xs = tuple(jnp.moveaxis(t, 2, 0) for t in (q_g, k_tail, u, w, qk, g_last))
    S, o = lax.scan(step, S0.astype(f32), xs)
    o = jnp.moveaxis(o, 0, 2).transpose(0, 2, 3, 1, 4).reshape(N, n * c, H, dv)[:, :L]
    return o, S


def gdn_conformer_mixer(h, p, dn_S, dn_conv, cc_conv):
    N, L, _ = h.shape
    proj = h @ p['ab_w_in']
    qkv_raw, z, b_raw, a_raw, glu_in = jnp.split(
        proj, [3 * GDN_WIDTH, 4 * GDN_WIDTH, 4 * GDN_WIDTH + GDN_HEADS, 4 * GDN_WIDTH + 2 * GDN_HEADS], axis=-1)
    qkv, new_dn_conv = causal_depthwise_conv(qkv_raw, dn_conv, p['dn_conv_w'])
    qkv = jax.nn.silu(qkv)
    q, k, v = jnp.split(qkv, 3, axis=-1)
    q = l2_normalize(q.reshape(N, L, GDN_HEADS, GDN_DK)) * (GDN_DK ** -0.5)
    k = l2_normalize(k.reshape(N, L, GDN_HEADS, GDN_DK))
    v = v.reshape(N, L, GDN_HEADS, GDN_DV)
    beta = jax.nn.sigmoid(b_raw.astype(jnp.float32))
    g = -jnp.exp(p['dn_A_log'].astype(jnp.float32)) * jax.nn.softplus(
        a_raw.astype(jnp.float32) + p['dn_dt_bias'].astype(jnp.float32))
    o, new_S = gated_delta_rule(q, k, v, g, beta, dn_S)
    o = rms_norm(o, p['dn_norm_g']) * jax.nn.silu(z.reshape(N, L, GDN_HEADS, GDN_DV).astype(jnp.float32))
    o = o.reshape(N, L, GDN_WIDTH).astype(h.dtype)
    ga, gb = jnp.split(glu_in, 2, axis=-1)
    glu = ga * jax.nn.sigmoid(gb)
    c, new_cc_conv = causal_depthwise_conv(glu, cc_conv, p['cc_conv_w'])
    c = jax.nn.silu(layer_norm(c + p['cc_conv_b'], p['cc_ln_g'], p['cc_ln_b']))
    y = jnp.concatenate([o, c], axis=-1) @ p['ab_w_out']
    return y, new_S.astype(dn_S.dtype), new_dn_conv, new_cc_conv


def chunk_spatial_mix(v, w_s, b_s):
    N, L, C = v.shape
    pad = (-L) % GM_CHUNK
    vp = jnp.pad(v, ((0, 0), (0, pad), (0, 0)))
    n = (L + pad) // GM_CHUNK
    vg = vp.reshape(N, n, GM_CHUNK, GM_GROUPS, C // GM_GROUPS)
    w = jnp.where(jnp.tril(jnp.ones((GM_CHUNK, GM_CHUNK), bool)), w_s, 0.0).astype(v.dtype)
    f = jnp.einsum('gij,bnjgc->bnigc', w, vg) + jnp.transpose(b_s)[:, :, None]
    return f.reshape(N, n * GM_CHUNK, C)[:, :L]


def chunk_mlp_mixer(h, p):
    pr = jax.nn.gelu(h @ p['gm_w_in'])
    u, v = jnp.split(pr, 2, axis=-1)
    v = layer_norm(v, p['gm_ln_g'], p['gm_ln_b'])
    f = chunk_spatial_mix(v, p['gm_w_s'], p['gm_b_s'])
    return (u * f) @ p['gm_w_out'], v


def cross_attend(h, mk, mv, wq, wo):
    N, L, _ = h.shape
    q = (h @ wq).reshape(N, L, XA_HEADS, XA_HEAD_DIM)
    s = jnp.einsum('blhd,bmhd->bhlm', q, mk.astype(h.dtype)).astype(jnp.float32) * (XA_HEAD_DIM ** -0.5)
    pr = jax.nn.softmax(s, axis=-1).astype(h.dtype)
    o = jnp.einsum('bhlm,bmhd->blhd', pr, mv.astype(h.dtype)).reshape(N, L, D_MODEL)
    return o @ wo


def trunk(x, mem_k, mem_v, dn_S, dn_conv, cc_conv, p):
    gm_v = None
    for l in range(DEPTH):
        x = layer_norm(DN_ALPHA * x + 0.5 * swiglu(x, p['ffn_w_gate'][l, 0], p['ffn_w_up'][l, 0], p['ffn_w_down'][l, 0]),
                       p['ln_g'][l, 0], p['ln_b'][l, 0])
        if l % 2 == 0:
            mix, dn_S, dn_conv, cc_conv = gdn_conformer_mixer(x, p, dn_S, dn_conv, cc_conv)
        else:
            mix, gm_v = chunk_mlp_mixer(x, p)
        x = layer_norm(DN_ALPHA * x + mix, p['ln_g'][l, 1], p['ln_b'][l, 1])
        x = layer_norm(DN_ALPHA * x + cross_attend(x, mem_k[l], mem_v[l], p['xa_wq'][l], p['xa_wo'][l]),
                       p['ln_g'][l, 2], p['ln_b'][l, 2])
        x = layer_norm(DN_ALPHA * x + 0.5 * swiglu(x, p['ffn_w_gate'][l, 1], p['ffn_w_up'][l, 1], p['ffn_w_down'][l, 1]),
                       p['ln_g'][l, 3], p['ln_b'][l, 3])
    return x, dn_S, dn_conv, cc_conv, gm_v


def setup_inputs(seed: int = 0) -> dict:
    key = jax.random.key(seed)
    ks = iter(jax.random.split(key, 48))
    nrm = lambda shape, scale: scale * jax.random.normal(next(ks), shape, jnp.float32)
    dense = lambda shape, s=1.0: nrm(shape, s * shape[-2] ** -0.5)
    dt = jnp.exp(jax.random.uniform(next(ks), (GDN_HEADS,), jnp.float32, math.log(1e-3), math.log(1e-1)))
    inp = {
        'x_prompt': nrm((BATCH, SEQ, D_MODEL), 1.0),
        'x_sample': nrm((DEC_BATCH, DEC_SEQ, D_MODEL), 1.0),
        'mem_prompt': nrm((BATCH, N_MEM, D_MODEL), 1.0),
        'cache_mem_k': nrm((DEPTH, DEC_BATCH, N_MEM, XA_HEADS, XA_HEAD_DIM), 1.0),
        'cache_mem_v': nrm((DEPTH, DEC_BATCH, N_MEM, XA_HEADS, XA_HEAD_DIM), 1.0),
        'state_dn_S': nrm((DEC_BATCH, GDN_HEADS, GDN_DK, GDN_DV), 0.1),
        'state_dn_conv': nrm((DEC_BATCH, SCONV_W - 1, 3 * GDN_WIDTH), 1.0),
        'state_cc_conv': nrm((DEC_BATCH, CC_W - 1, CC_CH), 0.5),
        'ln_g': 1.0 + nrm((DEPTH, 4, D_MODEL), 0.02),
        'ln_b': nrm((DEPTH, 4, D_MODEL), 0.02),
        'ffn_w_gate': dense((DEPTH, 2, D_MODEL, D_FF)),
        'ffn_w_up': dense((DEPTH, 2, D_MODEL, D_FF)),
        'ffn_w_down': dense((DEPTH, 2, D_FF, D_MODEL), DN_BETA),
        'xa_wq': dense((DEPTH, D_MODEL, D_MODEL)),
        'xa_wk': dense((DEPTH, D_MODEL, D_MODEL)),
        'xa_wv': dense((DEPTH, D_MODEL, D_MODEL)),
        'xa_wo': dense((DEPTH, D_MODEL, D_MODEL), DN_BETA),
        'ab_w_in': dense((D_MODEL, AB_IN)),
        'dn_conv_w': nrm((SCONV_W, 3 * GDN_WIDTH), SCONV_W ** -0.5),
        'dn_A_log': jnp.log(jax.random.uniform(next(ks), (GDN_HEADS,), jnp.float32, 1.0, 16.0)),
        'dn_dt_bias': dt + jnp.log(-jnp.expm1(-dt)),
        'dn_norm_g': 1.0 + nrm((GDN_DV,), 0.02),
        'cc_conv_w': nrm((CC_W, CC_CH), CC_W ** -0.5),
        'cc_conv_b': nrm((CC_CH,), 0.02),
        'cc_ln_g': 1.0 + nrm((CC_CH,), 0.02),
        'cc_ln_b': nrm((CC_CH,), 0.02),
        'ab_w_out': dense((GDN_WIDTH + CC_CH, D_MODEL), DN_BETA),
        'gm_w_in': dense((D_MODEL, 2 * GM_WIDTH)),
        'gm_ln_g': 1.0 + nrm((GM_WIDTH,), 0.02),
        'gm_ln_b': nrm((GM_WIDTH,), 0.02),
        'gm_w_s': nrm((GM_GROUPS, GM_CHUNK, GM_CHUNK), GM_CHUNK ** -0.5),
        'gm_b_s': 1.0 + nrm((GM_GROUPS, GM_CHUNK), 0.02),
        'gm_w_out': dense((GM_WIDTH, D_MODEL), DN_BETA),
    }
    return inp


def reference(x_prompt, x_sample, mem_prompt, cache_mem_k, cache_mem_v, state_dn_S, state_dn_conv, state_cc_conv,
              ln_g, ln_b, ffn_w_gate, ffn_w_up, ffn_w_down, xa_wq, xa_wk, xa_wv, xa_wo,
              ab_w_in, dn_conv_w, dn_A_log, dn_dt_bias, dn_norm_g, cc_conv_w, cc_conv_b, cc_ln_g, cc_ln_b, ab_w_out,
              gm_w_in, gm_ln_g, gm_ln_b, gm_w_s, gm_b_s, gm_w_out):
    p = dict(ln_g=ln_g, ln_b=ln_b, ffn_w_gate=ffn_w_gate, ffn_w_up=ffn_w_up, ffn_w_down=ffn_w_down,
             xa_wq=xa_wq, xa_wo=xa_wo, ab_w_in=ab_w_in, dn_conv_w=dn_conv_w, dn_A_log=dn_A_log,
             dn_dt_bias=dn_dt_bias, dn_norm_g=dn_norm_g, cc_conv_w=cc_conv_w, cc_conv_b=cc_conv_b,
             cc_ln_g=cc_ln_g, cc_ln_b=cc_ln_b, ab_w_out=ab_w_out, gm_w_in=gm_w_in, gm_ln_g=gm_ln_g,
             gm_ln_b=gm_ln_b, gm_w_s=gm_w_s, gm_b_s=gm_b_s, gm_w_out=gm_w_out)
    B = x_prompt.shape[0]
    dt = x_prompt.dtype
    mem_k_prompt = jnp.einsum('bmd,lde->lbme', mem_prompt, xa_wk).reshape(DEPTH, B, N_MEM, XA_HEADS, XA_HEAD_DIM)
    mem_v_prompt = jnp.einsum('bmd,lde->lbme', mem_prompt, xa_wv).reshape(DEPTH, B, N_MEM, XA_HEADS, XA_HEAD_DIM)
    y_prompt, dn_S_prompt, dn_conv_prompt, cc_conv_prompt, _ = trunk(
        x_prompt, mem_k_prompt, mem_v_prompt,
        jnp.zeros((B, GDN_HEADS, GDN_DK, GDN_DV), dt),
        jnp.zeros((B, SCONV_W - 1, 3 * GDN_WIDTH), dt),
        jnp.zeros((B, CC_W - 1, CC_CH), dt), p)
    y_sample, dn_S_sample, dn_conv_sample, cc_conv_sample, gm_v_sample = trunk(
        x_sample, cache_mem_k, cache_mem_v, state_dn_S, state_dn_conv, state_cc_conv, p)
    return (y_prompt, y_sample, mem_k_prompt, mem_v_prompt, dn_S_prompt, dn_conv_prompt, cc_conv_prompt,
            dn_S_sample, dn_conv_sample, cc_conv_sample, gm_v_sample)
```

```python
import functools

import jax
import jax.numpy as jnp
from jax import lax
from jax.experimental import pallas as pl
from jax.experimental.pallas import tpu as pltpu

D_MODEL = 1024
DEPTH = 2
DN_ALPHA = (2 * DEPTH) ** 0.25
LN_EPS = 1e-5
GDN_HEADS = 4
GDN_DK = 128
GDN_DV = 128
GDN_WIDTH = GDN_HEADS * GDN_DK
SCONV_W = 4
CC_CH = D_MODEL // 2
CC_W = 31
GM_WIDTH = D_MODEL
GM_GROUPS = 4
GM_CHUNK = 128
N_MEM = 256
XA_HEADS = 4
XA_HEAD_DIM = D_MODEL // XA_HEADS
D_FF = 2816

LANES = 128
SUBLANES = 8
GDN_BLOCK = 128
DN_TAIL = SUBLANES
CC_TAIL = 32
VMEM_LIMIT = 56 * 1024 * 1024

bf16 = jnp.bfloat16
f32 = jnp.float32


def _cparams(*sem):
    return pltpu.CompilerParams(dimension_semantics=sem, vmem_limit_bytes=VMEM_LIMIT)


def _full(shape):
    n = len(shape)
    return pl.BlockSpec(shape, lambda *_: (0,) * n)


def _dot(a, b):
    return jnp.dot(a.astype(bf16), b.astype(bf16), preferred_element_type=f32)


def _dot_nt(a, b):
    return lax.dot_general(a.astype(bf16), b.astype(bf16), (((1,), (1,)), ((), ())), preferred_element_type=f32)


def _split3(a):
    a1 = a.astype(bf16)
    r = a - a1.astype(f32)
    a2 = r.astype(bf16)
    a3 = (r - a2.astype(f32)).astype(bf16)
    return a1, a2, a3


def _dot_f32(a, b):
    a1, a2, a3 = _split3(a)
    b1, b2, b3 = _split3(b)
    d = lambda x, y: jnp.dot(x, y, preferred_element_type=f32)
    small = d(a1, b3) + d(a3, b1) + d(a2, b2)
    mid = d(a1, b2) + d(a2, b1)
    return d(a1, b1) + (mid + small)


def _dot_exact_lhs(a_exact_bf16, b):
    b1, b2, b3 = _split3(b)
    d = lambda y: jnp.dot(a_exact_bf16, y, preferred_element_type=f32)
    return d(b1) + (d(b2) + d(b3))


def _ln(y, g, b):
    mu = jnp.mean(y, -1, keepdims=True)
    d = y - mu
    var = jnp.mean(d * d, -1, keepdims=True)
    return d * lax.rsqrt(var + LN_EPS) * g + b


def _silu(x):
    return x * jax.nn.sigmoid(x)


def _softplus(x):
    return jnp.maximum(x, 0.0) + jnp.log(1.0 + jnp.exp(-jnp.abs(x)))


def _ffn_kernel(x_ref, wg_ref, wu_ref, wd_ref, g_ref, b_ref, o_ref, xb_ref, acc_ref):
    j = pl.program_id(1)

    @pl.when(j == 0)
    def _():
        xb_ref[...] = x_ref[...].astype(bf16)
        acc_ref[...] = jnp.zeros_like(acc_ref)

    xb = xb_ref[...]
    hg = jnp.dot(xb, wg_ref[...], preferred_element_type=f32)
    hu = jnp.dot(xb, wu_ref[...], preferred_element_type=f32)
    h = (_silu(hg) * hu).astype(bf16)
    acc_ref[...] += jnp.dot(h, wd_ref[...], preferred_element_type=f32)

    @pl.when(j == pl.num_programs(1) - 1)
    def _():
        o_ref[...] = _ln(DN_ALPHA * x_ref[...] + 0.5 * acc_ref[...], g_ref[...], b_ref[...])


def _ffn_ln(x, wg, wu, wd, g, b, *, tm, tf=256):
    T, D = x.shape
    tm = min(tm, T)
    F = wg.shape[1]
    return pl.pallas_call(
        _ffn_kernel,
        grid=(T // tm, F // tf),
        in_specs=[
            pl.BlockSpec((tm, D), lambda i, j: (i, 0)),
            pl.BlockSpec((D, tf), lambda i, j: (0, j)),
            pl.BlockSpec((D, tf), lambda i, j: (0, j)),
            pl.BlockSpec((tf, D), lambda i, j: (j, 0)),
            pl.BlockSpec((1, D), lambda i, j: (0, 0)),
            pl.BlockSpec((1, D), lambda i, j: (0, 0)),
        ],
        out_specs=pl.BlockSpec((tm, D), lambda i, j: (i, 0)),
        out_shape=jax.ShapeDtypeStruct((T, D), f32),
        scratch_shapes=[pltpu.VMEM((tm, D), bf16), pltpu.VMEM((tm, D), f32)],
        compiler_params=_cparams("parallel", "arbitrary"),
        name="ffn_ln",
    )(x, wg, wu, wd, g, b)


def _memkv_kernel(m_ref, wk_ref, wv_ref, k_ref, v_ref):
    mb = m_ref[...].astype(bf16)
    k_ref[0] = jnp.dot(mb, wk_ref[0], preferred_element_type=f32)
    v_ref[0] = jnp.dot(mb, wv_ref[0], preferred_element_type=f32)


def _mem_kv(mem, wk, wv, *, tm=512):
    R, D = mem.shape
    tm = min(tm, R)
    L = wk.shape[0]
    out = jax.ShapeDtypeStruct((L, R, D), f32)
    return pl.pallas_call(
        _memkv_kernel,
        grid=(L, R // tm),
        in_specs=[
            pl.BlockSpec((tm, D), lambda l, i: (i, 0)),
            pl.BlockSpec((1, D, D), lambda l, i: (l, 0, 0)),
            pl.BlockSpec((1, D, D), lambda l, i: (l, 0, 0)),
        ],
        out_specs=[pl.BlockSpec((1, tm, D), lambda l, i: (l, i, 0)),
                   pl.BlockSpec((1, tm, D), lambda l, i: (l, i, 0))],
        out_shape=[out, out],
        compiler_params=_cparams("parallel", "parallel"),
        name="mem_kv",
    )(mem, wk, wv)


def _xattn_kernel(x_ref, k_ref, v_ref, wq_ref, wo_ref, g_ref, b_ref, o_ref, oh_ref):
    x = x_ref[...]
    q = jnp.dot(x.astype(bf16), wq_ref[...], preferred_element_type=f32).astype(bf16)
    kb = k_ref[...].astype(bf16)
    vb = v_ref[...].astype(bf16)
    for h in range(XA_HEADS):
        sl = slice(h * XA_HEAD_DIM, (h + 1) * XA_HEAD_DIM)
        s = _dot_nt(q[:, sl], kb[:, sl]) * (XA_HEAD_DIM ** -0.5)
        s = s - jnp.max(s, -1, keepdims=True)
        e = jnp.exp(s)
        p = e / jnp.sum(e, -1, keepdims=True)
        oh_ref[:, sl] = jnp.dot(p.astype(bf16), vb[:, sl], preferred_element_type=f32).astype(bf16)
    att = jnp.dot(oh_ref[...], wo_ref[...], preferred_element_type=f32)
    o_ref[...] = _ln(DN_ALPHA * x + att, g_ref[...], b_ref[...])


def _xattn_ln(x, mk, mv, wq, wo, g, b, *, seq, tq=512):
    T, D = x.shape
    tq = min(tq, seq)
    nq = seq // tq
    return pl.pallas_call(
        _xattn_kernel,
        grid=(T // tq,),
        in_specs=[
            pl.BlockSpec((tq, D), lambda i: (i, 0)),
            pl.BlockSpec((N_MEM, D), lambda i: (i // nq, 0)),
            pl.BlockSpec((N_MEM, D), lambda i: (i // nq, 0)),
            _full((D, D)), _full((D, D)), _full((1, D)), _full((1, D)),
        ],
        out_specs=pl.BlockSpec((tq, D), lambda i: (i, 0)),
        out_shape=jax.ShapeDtypeStruct((T, D), f32),
        scratch_shapes=[pltpu.VMEM((tq, D), bf16)],
        compiler_params=_cparams("parallel"),
        name="xattn_ln",
    )(x, mk, mv, wq, wo, g, b)


def _xattn_dec_kernel(x_ref, k_ref, v_ref, wq_ref, wo_ref, g_ref, b_ref, o_ref, q_scr, a_scr, *, bs):
    i = pl.program_id(0)

    @pl.when(i == 0)
    def _():
        q_scr[...] = jnp.dot(x_ref[...].astype(bf16), wq_ref[...], preferred_element_type=f32)

    for s in range(bs):
        n = i * bs + s
        q = q_scr[pl.ds(n, 1), :]
        prod = k_ref[s] * q
        v = v_ref[s]
        outs = []
        for h in range(XA_HEADS):
            sl = slice(h * XA_HEAD_DIM, (h + 1) * XA_HEAD_DIM)
            sc = jnp.sum(prod[:, sl], -1, keepdims=True) * (XA_HEAD_DIM ** -0.5)
            e = jnp.exp(sc - jnp.max(sc, 0, keepdims=True))
            p = e / jnp.sum(e, 0, keepdims=True)
            outs.append(jnp.sum(p * v[:, sl], 0, keepdims=True))
        a_scr[pl.ds(n, 1), :] = jnp.concatenate(outs, axis=-1)

    @pl.when(i == pl.num_programs(0) - 1)
    def _():
        att = jnp.dot(a_scr[...].astype(bf16), wo_ref[...], preferred_element_type=f32)
        o_ref[...] = _ln(DN_ALPHA * x_ref[...] + att, g_ref[...], b_ref[...])


def _xattn_dec_ln(x, ck, cv, wq, wo, g, b, *, bs=8):
    N, D = x.shape
    return pl.pallas_call(
        functools.partial(_xattn_dec_kernel, bs=bs),
        grid=(N // bs,),
        in_specs=[
            _full((N, D)),
            pl.BlockSpec((bs, N_MEM, D), lambda i: (i, 0, 0)),
            pl.BlockSpec((bs, N_MEM, D), lambda i: (i, 0, 0)),
            _full((D, D)), _full((D, D)), _full((1, D)), _full((1, D)),
        ],
        out_specs=_full((N, D)),
        out_shape=jax.ShapeDtypeStruct((N, D), f32),
        scratch_shapes=[pltpu.VMEM((N, D), f32), pltpu.VMEM((N, D), f32)],
        compiler_params=_cparams("arbitrary"),
        name="xattn_dec_ln",
    )(x, ck, cv, wq, wo, g, b)


def _gmlp_kernel(x_ref, win_ref, ws_ref, bs_ref, vg_ref, vb_ref, wout_ref, g_ref, b_ref, o_ref, v_ref, uf_ref, *, single):
    x = x_ref[...]
    tm = x.shape[0]
    pr = jax.nn.gelu(jnp.dot(x.astype(bf16), win_ref[...], preferred_element_type=f32))
    u = pr[:, :GM_WIDTH]
    v = _ln(pr[:, GM_WIDTH:], vg_ref[...], vb_ref[...])
    v_ref[...] = v
    gw = GM_WIDTH // GM_GROUPS
    if single:
        for g in range(GM_GROUPS):
            sl = slice(g * gw, (g + 1) * gw)
            f = ws_ref[g][0:1, 0:1] * v[:, sl] + bs_ref[g][0:1, 0:1]
            uf_ref[:, sl] = (u[:, sl] * f).astype(bf16)
    else:
        row = lax.broadcasted_iota(jnp.int32, (GM_CHUNK, GM_CHUNK), 0)
        col = lax.broadcasted_iota(jnp.int32, (GM_CHUNK, GM_CHUNK), 1)
        vb16 = v.astype(bf16)
        for g in range(GM_GROUPS):
            sl = slice(g * gw, (g + 1) * gw)
            wmask = jnp.where(col <= row, ws_ref[g], 0.0).astype(bf16)
            bias = bs_ref[g]
            for c in range(tm // GM_CHUNK):
                rs = slice(c * GM_CHUNK, (c + 1) * GM_CHUNK)
                f = jnp.dot(wmask, vb16[rs, sl], preferred_element_type=f32) + bias
                uf_ref[rs, sl] = (u[rs, sl] * f).astype(bf16)
    y = jnp.dot(uf_ref[...], wout_ref[...], preferred_element_type=f32)
    o_ref[...] = _ln(DN_ALPHA * x + y, g_ref[...], b_ref[...])


def _gmlp_ln(x, w_in, w_s, b_s, vg, vb, w_out, g, b, *, tm, single):
    T, D = x.shape
    tm = min(tm, T)
    return pl.pallas_call(
        functools.partial(_gmlp_kernel, single=single),
        grid=(T // tm,),
        in_specs=[
            pl.BlockSpec((tm, D), lambda i: (i, 0)),
            _full((D, 2 * GM_WIDTH)), _full(w_s.shape), _full(b_s.shape),
            _full((1, GM_WIDTH)), _full((1, GM_WIDTH)), _full((GM_WIDTH, D)), _full((1, D)), _full((1, D)),
        ],
        out_specs=[pl.BlockSpec((tm, D), lambda i: (i, 0)), pl.BlockSpec((tm, GM_WIDTH), lambda i: (i, 0))],
        out_shape=[jax.ShapeDtypeStruct((T, D), f32), jax.ShapeDtypeStruct((T, GM_WIDTH), f32)],
        scratch_shapes=[pltpu.VMEM((tm, GM_WIDTH), bf16)],
        compiler_params=_cparams("parallel"),
        name="gmlp_ln",
    )(x, w_in, w_s, b_s, vg, vb, w_out, g, b)


def _gdn_gates(ba, alog, dtb):
    beta = jax.nn.sigmoid(ba[:, :LANES])
    g = -jnp.exp(alog) * _softplus(ba[:, LANES:] + dtb)
    return beta, g


def _l2n(x):
    return x * lax.rsqrt(jnp.sum(x * x, -1, keepdims=True) + 1e-6)


def _unit_lower_inverse(n_mat):
    c = n_mat.shape[0]
    row = lax.broadcasted_iota(jnp.int32, (c, c), 0)
    col = lax.broadcasted_iota(jnp.int32, (c, c), 1)
    p = jnp.where(row == col, 1.0, 0.0) + n_mat
    nk = n_mat
    k = 1
    while 2 * k < c:
        nk = _dot_f32(nk, nk)
        p = p + _dot_f32(p, nk)
        k *= 2
    return p


def _mixer0_kernel(x_ref, wmain_ref, wba_ref, dnw_ref, alog_ref, dtb_ref, ng_ref, ccw_ref, ccb_ref, cclg_ref, cclb_ref,
                   wout_ref, g_ref, b_ref,
                   o_ref, s_out_ref, dnc_out_ref, ccc_out_ref,
                   qkv_ext, glu_ext, s_scr, oc_scr):
    blk = pl.program_id(1)
    tb = x_ref.shape[0]
    C = GDN_BLOCK

    @pl.when(blk == 0)
    def _():
        qkv_ext[0:DN_TAIL, :] = jnp.zeros((DN_TAIL, 3 * GDN_WIDTH), f32)
        glu_ext[0:CC_TAIL, :] = jnp.zeros((CC_TAIL, CC_CH), f32)
        s_scr[...] = jnp.zeros_like(s_scr)

    x = x_ref[...]
    xb = x.astype(bf16)
    proj = jnp.dot(xb, wmain_ref[...], preferred_element_type=f32)
    ba = jnp.dot(xb, wba_ref[...], preferred_element_type=f32)
    nq = 3 * GDN_WIDTH

    qkv_ext[DN_TAIL:DN_TAIL + tb, :] = proj[:, :nq]
    acc = dnw_ref[SCONV_W - 1:SCONV_W, :] * proj[:, :nq]
    for s in range(1, SCONV_W):
        acc = acc + dnw_ref[SCONV_W - 1 - s:SCONV_W - s, :] * qkv_ext[pl.ds(DN_TAIL - s, tb), :]
    tail = qkv_ext[tb:tb + DN_TAIL, :]
    qkv_ext[0:DN_TAIL, :] = tail
    dnc_out_ref[0] = tail
    qkv = _silu(acc)

    beta, g = _gdn_gates(ba, alog_ref[...], dtb_ref[...])

    row = lax.broadcasted_iota(jnp.int32, (C, C), 0)
    col = lax.broadcasted_iota(jnp.int32, (C, C), 1)
    causal = col <= row
    strict = col < row
    ltri = jnp.where(causal, 1.0, 0.0).astype(bf16)

    for c in range(tb // C):
        rs = slice(c * C, (c + 1) * C)
        gc = _dot_exact_lhs(ltri, g[rs])
        gct = gc.T
        eg = jnp.exp(gc)
        g_last = gc[C - 1:C, :]
        ekt = jnp.exp(g_last - gc)
        egl = jnp.exp(g_last)
        for h in range(GDN_HEADS):
            hs = slice(h * GDN_DK, (h + 1) * GDN_DK)
            q_h = _l2n(qkv[rs, h * GDN_DK:(h + 1) * GDN_DK]) * (GDN_DK ** -0.5)
            k_h = _l2n(qkv[rs, GDN_WIDTH + h * GDN_DK:GDN_WIDTH + (h + 1) * GDN_DK])
            v_h = qkv[rs, 2 * GDN_WIDTH + h * GDN_DV:2 * GDN_WIDTH + (h + 1) * GDN_DV]
            b_col = beta[rs, h:h + 1]
            diff = gc[:, h:h + 1] - gct[h:h + 1, :]
            decay = jnp.where(causal, jnp.exp(jnp.where(causal, diff, 0.0)), 0.0)
            kb = k_h * b_col
            n_mat = -jnp.where(strict, _dot_nt(kb, k_h) * decay, 0.0)
            t_inv = _unit_lower_inverse(n_mat)
            rhs = jnp.concatenate([v_h * b_col, kb * eg[:, h:h + 1]], axis=-1)
            sol = _dot_f32(t_inv, rhs)
            u, w = sol[:, :GDN_DV], sol[:, GDN_DV:]
            qk = _dot_nt(q_h, k_h) * decay
            s_h = s_scr[h]
            v_new = u - _dot(w, s_h)
            o = _dot(q_h * eg[:, h:h + 1], s_h) + _dot(qk, v_new)
            kt = k_h * ekt[:, h:h + 1]
            s_scr[h] = s_h * egl[:, h:h + 1] + _dot(kt.T, v_new)
            o = o * lax.rsqrt(jnp.mean(o * o, -1, keepdims=True) + 1e-6) * ng_ref[...]
            z_h = proj[rs, nq + h * GDN_DV:nq + (h + 1) * GDN_DV]
            oc_scr[rs, hs] = (o * _silu(z_h)).astype(bf16)

    s_out_ref[0] = s_scr[...]

    ga = proj[:, nq + GDN_WIDTH:nq + GDN_WIDTH + CC_CH]
    gb = proj[:, nq + GDN_WIDTH + CC_CH:]
    glu = ga * jax.nn.sigmoid(gb)
    glu_ext[CC_TAIL:CC_TAIL + tb, :] = glu
    acc = ccw_ref[CC_W - 1:CC_W, :] * glu
    for s in range(1, CC_W):
        acc = acc + ccw_ref[CC_W - 1 - s:CC_W - s, :] * glu_ext[pl.ds(CC_TAIL - s, tb), :]
    tail = glu_ext[tb:tb + CC_TAIL, :]
    glu_ext[0:CC_TAIL, :] = tail
    ccc_out_ref[0] = tail
    cc = _silu(_ln(acc + ccb_ref[...], cclg_ref[...], cclb_ref[...]))
    oc_scr[:, GDN_WIDTH:] = cc.astype(bf16)

    y = jnp.dot(oc_scr[...], wout_ref[...], preferred_element_type=f32)
    o_ref[...] = _ln(DN_ALPHA * x + y, g_ref[...], b_ref[...])


def _mixer0_ln(x, wmain, wba, dnw, alog, dtb, ng, ccw, ccb, cclg, cclb, wout, g, b, *, batch, seq, tb=512):
    T, D = x.shape
    tb = min(tb, seq)
    nb = seq // tb
    nmain = wmain.shape[1]
    consts = [wmain, wba, dnw, alog, dtb, ng, ccw, ccb, cclg, cclb, wout, g, b]
    return pl.pallas_call(
        _mixer0_kernel,
        grid=(batch, nb),
        in_specs=[pl.BlockSpec((tb, D), lambda i, j: (i * nb + j, 0))] + [_full(c.shape) for c in consts],
        out_specs=[
            pl.BlockSpec((tb, D), lambda i, j: (i * nb + j, 0)),
            pl.BlockSpec((1, GDN_HEADS, GDN_DK, GDN_DV), lambda i, j: (i, 0, 0, 0)),
            pl.BlockSpec((1, DN_TAIL, 3 * GDN_WIDTH), lambda i, j: (i, 0, 0)),
            pl.BlockSpec((1, CC_TAIL, CC_CH), lambda i, j: (i, 0, 0)),
        ],
        out_shape=[
            jax.ShapeDtypeStruct((T, D), f32),
            jax.ShapeDtypeStruct((batch, GDN_HEADS, GDN_DK, GDN_DV), f32),
            jax.ShapeDtypeStruct((batch, DN_TAIL, 3 * GDN_WIDTH), f32),
            jax.ShapeDtypeStruct((batch, CC_TAIL, CC_CH), f32),
        ],
        scratch_shapes=[
            pltpu.VMEM((tb + DN_TAIL, 3 * GDN_WIDTH), f32),
            pltpu.VMEM((tb + CC_TAIL, CC_CH), f32),
            pltpu.VMEM((GDN_HEADS, GDN_DK, GDN_DV), f32),
            pltpu.VMEM((tb, GDN_WIDTH + CC_CH), bf16),
        ],
        compiler_params=_cparams("parallel", "arbitrary"),
        name="mixer0_ln",
    )(x, *consts)


def _mixer0_dec_kernel(x_ref, s_ref, dnc_ref, ccc_ref, wmain_ref, wba_ref, dnw_ref, alog_ref, dtb_ref, ng_ref, ccw_ref,
                       ccb_ref, cclg_ref, cclb_ref, wout_ref, g_ref, b_ref,
                       o_ref, s_out_ref, dnc_out_ref, ccc_out_ref,
                       q_scr, k_scr, v_scr, z_scr, beta_scr, eg_scr, oc_scr, *, bs):
    i = pl.program_id(0)
    nq = 3 * GDN_WIDTH

    @pl.when(i == 0)
    def _():
        xb = x_ref[...].astype(bf16)
        proj = jnp.dot(xb, wmain_ref[...], preferred_element_type=f32)
        ba = jnp.dot(xb, wba_ref[...], preferred_element_type=f32)
        qkv_raw = proj[:, :nq]
        acc = dnw_ref[SCONV_W - 1:SCONV_W, :] * qkv_raw
        for j in range(SCONV_W - 1):
            acc = acc + dnw_ref[j:j + 1, :] * dnc_ref[:, j * nq:(j + 1) * nq]
        dnc_out_ref[:, :(SCONV_W - 2) * nq] = dnc_ref[:, nq:]
        dnc_out_ref[:, (SCONV_W - 2) * nq:] = qkv_raw
        qkv = _silu(acc)
        for h in range(GDN_HEADS):
            hs = slice(h * GDN_DK, (h + 1) * GDN_DK)
            q_scr[:, hs] = _l2n(qkv[:, h * GDN_DK:(h + 1) * GDN_DK]) * (GDN_DK ** -0.5)
            k_scr[:, hs] = _l2n(qkv[:, GDN_WIDTH + h * GDN_DK:GDN_WIDTH + (h + 1) * GDN_DK])
        v_scr[...] = qkv[:, 2 * GDN_WIDTH:]
        z_scr[...] = _silu(proj[:, nq:nq + GDN_WIDTH])
        beta, g = _gdn_gates(ba, alog_ref[...], dtb_ref[...])
        beta_scr[...] = beta
        eg_scr[...] = jnp.exp(g)

        ga = proj[:, nq + GDN_WIDTH:nq + GDN_WIDTH + CC_CH]
        gb = proj[:, nq + GDN_WIDTH + CC_CH:]
        glu = ga * jax.nn.sigmoid(gb)
        acc = ccw_ref[CC_W - 1:CC_W, :] * glu
        for j in range(CC_W - 1):
            acc = acc + ccw_ref[j:j + 1, :] * ccc_ref[:, j * CC_CH:(j + 1) * CC_CH]
        ccc_out_ref[:, :(CC_W - 2) * CC_CH] = ccc_ref[:, CC_CH:]
        ccc_out_ref[:, (CC_W - 2) * CC_CH:] = glu
        cc = _silu(_ln(acc + ccb_ref[...], cclg_ref[...], cclb_ref[...]))
        oc_scr[:, GDN_WIDTH:] = cc

    rows = pl.ds(pl.multiple_of(i * bs, bs), bs)
    q_blk, k_blk, v_blk, z_blk = q_scr[rows, :], k_scr[rows, :], v_scr[rows, :], z_scr[rows, :]
    beta_blk, eg_blk = beta_scr[rows, :], eg_scr[rows, :]
    o_rows = []
    for s in range(bs):
        o_heads = []
        for h in range(GDN_HEADS):
            hs = slice(h * GDN_DK, (h + 1) * GDN_DK)
            k_col = jnp.broadcast_to(k_blk[s:s + 1, hs], (GDN_DK, GDN_DK)).T
            q_col = jnp.broadcast_to(q_blk[s:s + 1, hs], (GDN_DK, GDN_DK)).T
            b1 = beta_blk[s:s + 1, h:h + 1]
            e1 = eg_blk[s:s + 1, h:h + 1]
            s_old = s_ref[s, h]
            ks = jnp.sum(k_col * s_old, 0, keepdims=True)
            v_new = b1 * (v_blk[s:s + 1, hs] - e1 * ks)
            s_new = s_old * e1 + k_col * v_new
            s_out_ref[s, h] = s_new
            o = jnp.sum(q_col * s_new, 0, keepdims=True)
            o = o * lax.rsqrt(jnp.mean(o * o, -1, keepdims=True) + 1e-6) * ng_ref[...]
            o_heads.append(o * z_blk[s:s + 1, hs])
        o_rows.append(jnp.concatenate(o_heads, axis=-1))
    oc_scr[rows, :GDN_WIDTH] = jnp.concatenate(o_rows, axis=0)

    @pl.when(i == pl.num_programs(0) - 1)
    def _():
        y = jnp.dot(oc_scr[...].astype(bf16), wout_ref[...], preferred_element_type=f32)
        o_ref[...] = _ln(DN_ALPHA * x_ref[...] + y, g_ref[...], b_ref[...])


def _mixer0_dec_ln(x, s, dnc, ccc, wmain, wba, dnw, alog, dtb, ng, ccw, ccb, cclg, cclb, wout, g, b, *, bs=8):
    N, D = x.shape
    consts = [wmain, wba, dnw, alog, dtb, ng, ccw, ccb, cclg, cclb, wout, g, b]
    sspec = pl.BlockSpec((bs, GDN_HEADS, GDN_DK, GDN_DV), lambda i: (i, 0, 0, 0))
    return pl.pallas_call(
        functools.partial(_mixer0_dec_kernel, bs=bs),
        grid=(N // bs,),
        in_specs=[_full((N, D)), sspec, _full(dnc.shape), _full(ccc.shape)] + [_full(c.shape) for c in consts],
        out_specs=[_full((N, D)), sspec, _full(dnc.shape), _full(ccc.shape)],
        out_shape=[jax.ShapeDtypeStruct((N, D), f32), jax.ShapeDtypeStruct(s.shape, f32),
                   jax.ShapeDtypeStruct(dnc.shape, f32), jax.ShapeDtypeStruct(ccc.shape, f32)],
        scratch_shapes=[pltpu.VMEM((N, GDN_WIDTH), f32)] * 4 + [pltpu.VMEM((N, LANES), f32)] * 2
        + [pltpu.VMEM((N, GDN_WIDTH + CC_CH), f32)],
        compiler_params=_cparams("arbitrary"),
        name="mixer0_dec_ln",
    )(x, s, dnc, ccc, *consts)


def _pad_lanes(v, n=LANES):
    return jnp.zeros((1, n), f32).at[0, :v.shape[0]].set(v.astype(f32))


def kernel(x_prompt, x_sample, mem_prompt, cache_mem_k, cache_mem_v, state_dn_S, state_dn_conv, state_cc_conv, ln_g, ln_b, ffn_w_gate, ffn_w_up, ffn_w_down, xa_wq, xa_wk, xa_wv, xa_wo, ab_w_in, dn_conv_w, dn_A_log, dn_dt_bias, dn_norm_g, cc_conv_w, cc_conv_b, cc_ln_g, cc_ln_b, ab_w_out, gm_w_in, gm_ln_g, gm_ln_b, gm_w_s, gm_b_s, gm_w_out):
    B, SEQ, D = x_prompt.shape
    N = x_sample.shape[0]
    row = lambda v: v.reshape(1, -1).astype(f32)

    wg, wu, wd = ffn_w_gate.astype(bf16), ffn_w_up.astype(bf16), ffn_w_down.astype(bf16)
    wq, wk, wv, wo = xa_wq.astype(bf16), xa_wk.astype(bf16), xa_wv.astype(bf16), xa_wo.astype(bf16)
    nq = 3 * GDN_WIDTH
    nz = nq + GDN_WIDTH
    w_main = jnp.concatenate([ab_w_in[:, :nz], ab_w_in[:, nz + 2 * GDN_HEADS:]], axis=1).astype(bf16)
    w_ba = jnp.zeros((D, 2 * LANES), f32)
    w_ba = w_ba.at[:, :GDN_HEADS].set(ab_w_in[:, nz:nz + GDN_HEADS])
    w_ba = w_ba.at[:, LANES:LANES + GDN_HEADS].set(ab_w_in[:, nz + GDN_HEADS:nz + 2 * GDN_HEADS]).astype(bf16)
    w_out = ab_w_out.astype(bf16)
    gw_in, gw_out = gm_w_in.astype(bf16), gm_w_out.astype(bf16)
    mixer_consts = (w_main, w_ba, dn_conv_w.astype(f32), _pad_lanes(dn_A_log), _pad_lanes(dn_dt_bias), row(dn_norm_g),
                    cc_conv_w.astype(f32), row(cc_conv_b), row(cc_ln_g), row(cc_ln_b), w_out)
    gm_consts = (gw_in, gm_w_s.astype(f32), gm_b_s.astype(f32)[:, :, None], row(gm_ln_g), row(gm_ln_b), gw_out)
    lng = lambda l, i: ln_g[l, i].reshape(1, D)
    lnb = lambda l, i: ln_b[l, i].reshape(1, D)

    mem_k, mem_v = _mem_kv(mem_prompt.reshape(B * N_MEM, D), wk, wv)

    def trunk(x, mk, mv, *, tm, prompt, state=None):
        gm_v = None
        for l in range(DEPTH):
            x = _ffn_ln(x, wg[l, 0], wu[l, 0], wd[l, 0], lng(l, 0), lnb(l, 0), tm=tm)
            if l % 2 == 0:
                if prompt:
                    x, dn_s, dn_c, cc_c = _mixer0_ln(x, *mixer_consts, lng(l, 1), lnb(l, 1), batch=B, seq=SEQ)
                else:
                    x, dn_s, dn_c, cc_c = _mixer0_dec_ln(x, *state, *mixer_consts, lng(l, 1), lnb(l, 1))
            else:
                x, gm_v = _gmlp_ln(x, *gm_consts, lng(l, 1), lnb(l, 1), tm=min(tm, 512), single=not prompt)
            if prompt:
                x = _xattn_ln(x, mk[l], mv[l], wq[l], wo[l], lng(l, 2), lnb(l, 2), seq=SEQ)
            else:
                x = _xattn_dec_ln(x, mk[l], mv[l], wq[l], wo[l], lng(l, 2), lnb(l, 2))
            x = _ffn_ln(x, wg[l, 1], wu[l, 1], wd[l, 1], lng(l, 3), lnb(l, 3), tm=tm)
        return x, dn_s, dn_c, cc_c, gm_v

    yp, dn_s_p, dn_c_p, cc_c_p, _ = trunk(x_prompt.reshape(B * SEQ, D), mem_k, mem_v, tm=1024, prompt=True)
    state = (state_dn_S, state_dn_conv.reshape(N, -1), state_cc_conv.reshape(N, -1))
    ys, dn_s_s, dn_c_s, cc_c_s, gm_v_s = trunk(
        x_sample.reshape(N, D), cache_mem_k.reshape(DEPTH, N, N_MEM, D), cache_mem_v.reshape(DEPTH, N, N_MEM, D),
        tm=N, prompt=False, state=state)

    kv_shape = (DEPTH, B, N_MEM, XA_HEADS, XA_HEAD_DIM)
    return (yp.reshape(B, SEQ, D), ys.reshape(N, 1, D), mem_k.reshape(kv_shape), mem_v.reshape(kv_shape),
            dn_s_p, dn_c_p[:, DN_TAIL - (SCONV_W - 1):], cc_c_p[:, CC_TAIL - (CC_W - 1):],
            dn_s_s, dn_c_s.reshape(N, SCONV_W - 1, nq), cc_c_s.reshape(N, CC_W - 1, CC_CH),
            gm_v_s.reshape(N, 1, GM_WIDTH))
```

```python
import functools

import jax
import jax.numpy as jnp
from jax import lax
from jax.experimental import pallas as pl
from jax.experimental.pallas import tpu as pltpu

D_MODEL = 1024
DEPTH = 2
DN_ALPHA = (2 * DEPTH) ** 0.25
LN_EPS = 1e-5
GDN_HEADS = 4
GDN_DK = 128
GDN_DV = 128
GDN_WIDTH = GDN_HEADS * GDN_DK
SCONV_W = 4
CC_CH = D_MODEL // 2
CC_W = 31
GM_WIDTH = D_MODEL
GM_GROUPS = 4
GM_CHUNK = 128
N_MEM = 256
XA_HEADS = 4
XA_HEAD_DIM = D_MODEL // XA_HEADS
D_FF = 2816
MIX_LN_ROW = 1
XA_LN_ROW = 2

LANES = 128
SUBLANES = 8
GDN_BLOCK = 128
DN_TAIL = SUBLANES
CC_TAIL = 32
VMEM_LIMIT = 56 * 1024 * 1024

bf16 = jnp.bfloat16
f32 = jnp.float32


def _cparams(*sem):
    return pltpu.CompilerParams(dimension_semantics=sem, vmem_limit_bytes=VMEM_LIMIT)


def _full(shape):
    n = len(shape)
    return pl.BlockSpec(shape, lambda *_: (0,) * n)


def _dot(a, b):
    return jnp.dot(a.astype(bf16), b.astype(bf16), preferred_element_type=f32)


def _dot_nt(a, b):
    return lax.dot_general(a.astype(bf16), b.astype(bf16), (((1,), (1,)), ((), ())), preferred_element_type=f32)


def _split3(a):
    a1 = a.astype(bf16)
    r = a - a1.astype(f32)
    a2 = r.astype(bf16)
    a3 = (r - a2.astype(f32)).astype(bf16)
    return a1, a2, a3


def _dot_f32(a, b):
    a1, a2, a3 = _split3(a)
    b1, b2, b3 = _split3(b)
    d = lambda x, y: jnp.dot(x, y, preferred_element_type=f32)
    small = d(a1, b3) + d(a3, b1) + d(a2, b2)
    mid = d(a1, b2) + d(a2, b1)
    return d(a1, b1) + (mid + small)


def _dot_exact_lhs(a_exact_bf16, b):
    b1, b2, b3 = _split3(b)
    d = lambda y: jnp.dot(a_exact_bf16, y, preferred_element_type=f32)
    return d(b1) + (d(b2) + d(b3))


def _ln(y, g, b):
    mu = jnp.mean(y, -1, keepdims=True)
    d = y - mu
    var = jnp.mean(d * d, -1, keepdims=True)
    return d * lax.rsqrt(var + LN_EPS) * g + b


def _silu(x):
    return x * jax.nn.sigmoid(x)


def _softplus(x):
    return jnp.maximum(x, 0.0) + jnp.log(1.0 + jnp.exp(-jnp.abs(x)))


def _ffn_kernel(x_ref, wg_ref, wu_ref, wd_ref, g_ref, b_ref, o_ref, xb_ref, acc_ref, *, ln_row):
    j = pl.program_id(1)
    r = slice(ln_row, ln_row + 1)

    @pl.when(j == 0)
    def _():
        xb_ref[...] = x_ref[...].astype(bf16)
        acc_ref[...] = jnp.zeros_like(acc_ref)

    xb = xb_ref[...]
    hg = jnp.dot(xb, wg_ref[...], preferred_element_type=f32)
    hu = jnp.dot(xb, wu_ref[...], preferred_element_type=f32)
    h = (_silu(hg) * hu).astype(bf16)
    acc_ref[...] += jnp.dot(h, wd_ref[...], preferred_element_type=f32)

    @pl.when(j == pl.num_programs(1) - 1)
    def _():
        o_ref[...] = _ln(DN_ALPHA * x_ref[...] + 0.5 * acc_ref[...], g_ref[r, :], b_ref[r, :])


def _ffn_ln(x, wg, wu, wd, g, b, *, l, half, tm, tf=256):
    T, D = x.shape
    tm = min(tm, T)
    F = wg.shape[-1]
    nln = g.shape[1]
    return pl.pallas_call(
        functools.partial(_ffn_kernel, ln_row=(nln - 1) * half),
        grid=(T // tm, F // tf),
        in_specs=[
            pl.BlockSpec((tm, D), lambda i, j: (i, 0)),
            pl.BlockSpec((None, None, D, tf), lambda i, j: (l, half, 0, j)),
            pl.BlockSpec((None, None, D, tf), lambda i, j: (l, half, 0, j)),
            pl.BlockSpec((None, None, tf, D), lambda i, j: (l, half, j, 0)),
            pl.BlockSpec((None, nln, D), lambda i, j: (l, 0, 0)),
            pl.BlockSpec((None, nln, D), lambda i, j: (l, 0, 0)),
        ],
        out_specs=pl.BlockSpec((tm, D), lambda i, j: (i, 0)),
        out_shape=jax.ShapeDtypeStruct((T, D), f32),
        scratch_shapes=[pltpu.VMEM((tm, D), bf16), pltpu.VMEM((tm, D), f32)],
        compiler_params=_cparams("parallel", "arbitrary"),
        name="ffn_ln",
    )(x, wg, wu, wd, g, b)


def _memkv_kernel(m_ref, wk_ref, wv_ref, k_ref, v_ref):
    nb = m_ref.shape[0]
    for b in range(nb):
        mb = m_ref[b].astype(bf16)
        k = jnp.dot(mb, wk_ref[...], preferred_element_type=f32)
        v = jnp.dot(mb, wv_ref[...], preferred_element_type=f32)
        for h in range(XA_HEADS):
            sl = slice(h * XA_HEAD_DIM, (h + 1) * XA_HEAD_DIM)
            k_ref[b, :, h, :] = k[:, sl]
            v_ref[b, :, h, :] = v[:, sl]


def _mem_kv(mem, wk, wv, *, nb=2):
    B, M, D = mem.shape
    nb = min(nb, B)
    L = wk.shape[0]
    out = jax.ShapeDtypeStruct((L, B, M, XA_HEADS, XA_HEAD_DIM), f32)
    ospec = pl.BlockSpec((None, nb, M, XA_HEADS, XA_HEAD_DIM), lambda l, i: (l, i, 0, 0, 0))
    return pl.pallas_call(
        _memkv_kernel,
        grid=(L, B // nb),
        in_specs=[
            pl.BlockSpec((nb, M, D), lambda l, i: (i, 0, 0)),
            pl.BlockSpec((None, D, D), lambda l, i: (l, 0, 0)),
            pl.BlockSpec((None, D, D), lambda l, i: (l, 0, 0)),
        ],
        out_specs=[ospec, ospec],
        out_shape=[out, out],
        compiler_params=_cparams("parallel", "parallel"),
        name="mem_kv",
    )(mem, wk, wv)


def _xattn_kernel(x_ref, k_ref, v_ref, wq_ref, wo_ref, g_ref, b_ref, o_ref, oh_ref):
    x = x_ref[...]
    q = jnp.dot(x.astype(bf16), wq_ref[...], preferred_element_type=f32).astype(bf16)
    for h in range(XA_HEADS):
        sl = slice(h * XA_HEAD_DIM, (h + 1) * XA_HEAD_DIM)
        s = _dot_nt(q[:, sl], k_ref[:, h, :]) * (XA_HEAD_DIM ** -0.5)
        s = s - jnp.max(s, -1, keepdims=True)
        e = jnp.exp(s)
        p = e / jnp.sum(e, -1, keepdims=True)
        oh_ref[:, sl] = jnp.dot(p.astype(bf16), v_ref[:, h, :].astype(bf16), preferred_element_type=f32).astype(bf16)
    att = jnp.dot(oh_ref[...], wo_ref[...], preferred_element_type=f32)
    o_ref[...] = _ln(DN_ALPHA * x + att, g_ref[XA_LN_ROW:XA_LN_ROW + 1, :], b_ref[XA_LN_ROW:XA_LN_ROW + 1, :])


def _layer_spec(shape, l):
    n = len(shape)
    return pl.BlockSpec((None,) + tuple(shape[1:]), lambda *_: (l,) + (0,) * (n - 1))


def _xattn_ln(x, mk, mv, wq, wo, g, b, *, l, seq, tq=512):
    T, D = x.shape
    tq = min(tq, seq)
    nq = seq // tq
    mspec = pl.BlockSpec((None, None, N_MEM, XA_HEADS, XA_HEAD_DIM), lambda i: (l, i // nq, 0, 0, 0))
    return pl.pallas_call(
        _xattn_kernel,
        grid=(T // tq,),
        in_specs=[
            pl.BlockSpec((tq, D), lambda i: (i, 0)),
            mspec, mspec,
            _layer_spec(wq.shape, l), _layer_spec(wo.shape, l), _layer_spec(g.shape, l), _layer_spec(b.shape, l),
        ],
        out_specs=pl.BlockSpec((tq, D), lambda i: (i, 0)),
        out_shape=jax.ShapeDtypeStruct((T, D), f32),
        scratch_shapes=[pltpu.VMEM((tq, D), bf16)],
        compiler_params=_cparams("parallel"),
        name="xattn_ln",
    )(x, mk, mv, wq, wo, g, b)


def _xattn_dec_kernel(x_ref, k_ref, v_ref, wq_ref, wo_ref, g_ref, b_ref, o_ref, q_scr, a_scr, *, bs):
    i = pl.program_id(0)

    @pl.when(i == 0)
    def _():
        q_scr[...] = jnp.dot(x_ref[...].astype(bf16), wq_ref[...], preferred_element_type=f32)

    rows = pl.ds(pl.multiple_of(i * bs, bs), bs)
    q_blk = q_scr[rows, :]
    o_rows = []
    for s in range(bs):
        q4 = jnp.concatenate([q_blk[s:s + 1, h * XA_HEAD_DIM:(h + 1) * XA_HEAD_DIM] for h in range(XA_HEADS)], axis=0)
        sc = jnp.sum(k_ref[s] * q4[None], -1, keepdims=True) * (XA_HEAD_DIM ** -0.5)
        e = jnp.exp(sc - jnp.max(sc, 0, keepdims=True))
        p = e / jnp.sum(e, 0, keepdims=True)
        o4 = jnp.sum(p * v_ref[s], 0)
        o_rows.append(jnp.concatenate([o4[h:h + 1, :] for h in range(XA_HEADS)], axis=-1))
    a_scr[rows, :] = jnp.concatenate(o_rows, axis=0)

    @pl.when(i == pl.num_programs(0) - 1)
    def _():
        att = jnp.dot(a_scr[...].astype(bf16), wo_ref[...], preferred_element_type=f32)
        o_ref[...] = _ln(DN_ALPHA * x_ref[...] + att, g_ref[XA_LN_ROW:XA_LN_ROW + 1, :], b_ref[XA_LN_ROW:XA_LN_ROW + 1, :])


def _xattn_dec_ln(x, ck, cv, wq, wo, g, b, *, l, bs=8):
    N, D = x.shape
    cspec = pl.BlockSpec((None, bs, N_MEM, XA_HEADS, XA_HEAD_DIM), lambda i: (l, i, 0, 0, 0))
    return pl.pallas_call(
        functools.partial(_xattn_dec_kernel, bs=bs),
        grid=(N // bs,),
        in_specs=[
            _full((N, D)), cspec, cspec,
            _layer_spec(wq.shape, l), _layer_spec(wo.shape, l), _layer_spec(g.shape, l), _layer_spec(b.shape, l),
        ],
        out_specs=_full((N, D)),
        out_shape=jax.ShapeDtypeStruct((N, D), f32),
        scratch_shapes=[pltpu.VMEM((N, D), f32), pltpu.VMEM((N, D), f32)],
        compiler_params=_cparams("arbitrary"),
        name="xattn_dec_ln",
    )(x, ck, cv, wq, wo, g, b)


def _gmlp_kernel(x_ref, win_ref, ws_ref, bs_ref, vg_ref, vb_ref, wout_ref, g_ref, b_ref, o_ref, v_ref, uf_ref, *, single):
    x = x_ref[...]
    tm = x.shape[0]
    pr = jax.nn.gelu(jnp.dot(x.astype(bf16), win_ref[...], preferred_element_type=f32))
    u = pr[:, :GM_WIDTH]
    v = _ln(pr[:, GM_WIDTH:], vg_ref[...], vb_ref[...])
    v_ref[...] = v
    gw = GM_WIDTH // GM_GROUPS
    if single:
        for g in range(GM_GROUPS):
            sl = slice(g * gw, (g + 1) * gw)
            f = ws_ref[g][0:1, 0:1] * v[:, sl] + bs_ref[g][0:1, 0:1]
            uf_ref[:, sl] = (u[:, sl] * f).astype(bf16)
    else:
        row = lax.broadcasted_iota(jnp.int32, (GM_CHUNK, GM_CHUNK), 0)
        col = lax.broadcasted_iota(jnp.int32, (GM_CHUNK, GM_CHUNK), 1)
        vb16 = v.astype(bf16)
        for g in range(GM_GROUPS):
            sl = slice(g * gw, (g + 1) * gw)
            wmask = jnp.where(col <= row, ws_ref[g], 0.0).astype(bf16)
            bias = bs_ref[g]
            for c in range(tm // GM_CHUNK):
                rs = slice(c * GM_CHUNK, (c + 1) * GM_CHUNK)
                f = jnp.dot(wmask, vb16[rs, sl], preferred_element_type=f32) + bias
                uf_ref[rs, sl] = (u[rs, sl] * f).astype(bf16)
    y = jnp.dot(uf_ref[...], wout_ref[...], preferred_element_type=f32)
    o_ref[...] = _ln(DN_ALPHA * x + y, g_ref[MIX_LN_ROW:MIX_LN_ROW + 1, :], b_ref[MIX_LN_ROW:MIX_LN_ROW + 1, :])


def _gmlp_ln(x, w_in, w_s, b_s, vg, vb, w_out, g, b, *, l, tm, single):
    T, D = x.shape
    tm = min(tm, T)
    return pl.pallas_call(
        functools.partial(_gmlp_kernel, single=single),
        grid=(T // tm,),
        in_specs=[
            pl.BlockSpec((tm, D), lambda i: (i, 0)),
            _full((D, 2 * GM_WIDTH)), _full(w_s.shape), _full(b_s.shape),
            _full((1, GM_WIDTH)), _full((1, GM_WIDTH)), _full((GM_WIDTH, D)), _layer_spec(g.shape, l), _layer_spec(b.shape, l),
        ],
        out_specs=[pl.BlockSpec((tm, D), lambda i: (i, 0)), pl.BlockSpec((tm, GM_WIDTH), lambda i: (i, 0))],
        out_shape=[jax.ShapeDtypeStruct((T, D), f32), jax.ShapeDtypeStruct((T, GM_WIDTH), f32)],
        scratch_shapes=[pltpu.VMEM((tm, GM_WIDTH), bf16)],
        compiler_params=_cparams("parallel"),
        name="gmlp_ln",
    )(x, w_in, w_s, b_s, vg, vb, w_out, g, b)


def _gdn_gates(ba, alog, dtb):
    beta = jax.nn.sigmoid(ba[:, :LANES])
    g = -jnp.exp(alog) * _softplus(ba[:, LANES:] + dtb)
    return beta, g


def _l2n(x):
    return x * lax.rsqrt(jnp.sum(x * x, -1, keepdims=True) + 1e-6)


def _unit_lower_inverse(n_mat):
    c = n_mat.shape[0]
    row = lax.broadcasted_iota(jnp.int32, (c, c), 0)
    col = lax.broadcasted_iota(jnp.int32, (c, c), 1)
    p = jnp.where(row == col, 1.0, 0.0) + n_mat
    nk = n_mat
    k = 1
    while 2 * k < c:
        nk = _dot_f32(nk, nk)
        p = p + _dot_f32(p, nk)
        k *= 2
    return p


def _mixer0_kernel(x_ref, wmain_ref, wba_ref, dnw_ref, alog_ref, dtb_ref, ng_ref, ccw_ref, ccb_ref, cclg_ref, cclb_ref,
                   wout_ref, g_ref, b_ref,
                   o_ref, s_out_ref, dnc_out_ref, ccc_out_ref,
                   qkv_ext, glu_ext, s_scr, oc_scr):
    blk = pl.program_id(1)
    tb = x_ref.shape[0]
    C = GDN_BLOCK

    @pl.when(blk == 0)
    def _():
        qkv_ext[0:DN_TAIL, :] = jnp.zeros((DN_TAIL, 3 * GDN_WIDTH), f32)
        glu_ext[0:CC_TAIL, :] = jnp.zeros((CC_TAIL, CC_CH), f32)
        s_scr[...] = jnp.zeros_like(s_scr)

    x = x_ref[...]
    xb = x.astype(bf16)
    proj = jnp.dot(xb, wmain_ref[...], preferred_element_type=f32)
    ba = jnp.dot(xb, wba_ref[...], preferred_element_type=f32)
    nq = 3 * GDN_WIDTH

    qkv_ext[DN_TAIL:DN_TAIL + tb, :] = proj[:, :nq]
    acc = dnw_ref[SCONV_W - 1:SCONV_W, :] * proj[:, :nq]
    for s in range(1, SCONV_W):
        acc = acc + dnw_ref[SCONV_W - 1 - s:SCONV_W - s, :] * qkv_ext[pl.ds(DN_TAIL - s, tb), :]
    tail = qkv_ext[tb:tb + DN_TAIL, :]
    qkv_ext[0:DN_TAIL, :] = tail
    dnc_out_ref[0] = tail
    qkv = _silu(acc)

    beta, g = _gdn_gates(ba, alog_ref[...], dtb_ref[...])

    row = lax.broadcasted_iota(jnp.int32, (C, C), 0)
    col = lax.broadcasted_iota(jnp.int32, (C, C), 1)
    causal = col <= row
    strict = col < row
    ltri = jnp.where(causal, 1.0, 0.0).astype(bf16)

    for c in range(tb // C):
        rs = slice(c * C, (c + 1) * C)
        gc = _dot_exact_lhs(ltri, g[rs])
        gct = gc.T
        eg = jnp.exp(gc)
        g_last = gc[C - 1:C, :]
        ekt = jnp.exp(g_last - gc)
        egl = jnp.exp(g_last)
        for h in range(GDN_HEADS):
            hs = slice(h * GDN_DK, (h + 1) * GDN_DK)
            q_h = _l2n(qkv[rs, h * GDN_DK:(h + 1) * GDN_DK]) * (GDN_DK ** -0.5)
            k_h = _l2n(qkv[rs, GDN_WIDTH + h * GDN_DK:GDN_WIDTH + (h + 1) * GDN_DK])
            v_h = qkv[rs, 2 * GDN_WIDTH + h * GDN_DV:2 * GDN_WIDTH + (h + 1) * GDN_DV]
            b_col = beta[rs, h:h + 1]
            diff = gc[:, h:h + 1] - gct[h:h + 1, :]
            decay = jnp.where(causal, jnp.exp(jnp.where(causal, diff, 0.0)), 0.0)
            kb = k_h * b_col
            n_mat = -jnp.where(strict, _dot_nt(kb, k_h) * decay, 0.0)
            t_inv = _unit_lower_inverse(n_mat)
            rhs = jnp.concatenate([v_h * b_col, kb * eg[:, h:h + 1]], axis=-1)
            sol = _dot_f32(t_inv, rhs)
            u, w = sol[:, :GDN_DV], sol[:, GDN_DV:]
            qk = _dot_nt(q_h, k_h) * decay
            s_h = s_scr[h]
            v_new = u - _dot(w, s_h)
            o = _dot(q_h * eg[:, h:h + 1], s_h) + _dot(qk, v_new)
            kt = k_h * ekt[:, h:h + 1]
            s_scr[h] = s_h * egl[:, h:h + 1] + _dot(kt.T, v_new)
            o = o * lax.rsqrt(jnp.mean(o * o, -1, keepdims=True) + 1e-6) * ng_ref[...]
            z_h = proj[rs, nq + h * GDN_DV:nq + (h + 1) * GDN_DV]
            oc_scr[rs, hs] = (o * _silu(z_h)).astype(bf16)

    s_out_ref[0] = s_scr[...]

    ga = proj[:, nq + GDN_WIDTH:nq + GDN_WIDTH + CC_CH]
    gb = proj[:, nq + GDN_WIDTH + CC_CH:]
    glu = ga * jax.nn.sigmoid(gb)
    glu_ext[CC_TAIL:CC_TAIL + tb, :] = glu
    acc = ccw_ref[CC_W - 1:CC_W, :] * glu
    for s in range(1, CC_W):
        acc = acc + ccw_ref[CC_W - 1 - s:CC_W - s, :] * glu_ext[pl.ds(CC_TAIL - s, tb), :]
    tail = glu_ext[tb:tb + CC_TAIL, :]
    glu_ext[0:CC_TAIL, :] = tail
    ccc_out_ref[0] = tail
    cc = _silu(_ln(acc + ccb_ref[...], cclg_ref[...], cclb_ref[...]))
    oc_scr[:, GDN_WIDTH:] = cc.astype(bf16)

    y = jnp.dot(oc_scr[...], wout_ref[...], preferred_element_type=f32)
    o_ref[...] = _ln(DN_ALPHA * x + y, g_ref[MIX_LN_ROW:MIX_LN_ROW + 1, :], b_ref[MIX_LN_ROW:MIX_LN_ROW + 1, :])


def _mixer0_ln(x, wmain, wba, dnw, alog, dtb, ng, ccw, ccb, cclg, cclb, wout, g, b, *, l, batch, seq, tb=512):
    T, D = x.shape
    tb = min(tb, seq)
    nb = seq // tb
    consts = [wmain, wba, dnw, alog, dtb, ng, ccw, ccb, cclg, cclb, wout]
    return pl.pallas_call(
        _mixer0_kernel,
        grid=(batch, nb),
        in_specs=[pl.BlockSpec((tb, D), lambda i, j: (i * nb + j, 0))] + [_full(c.shape) for c in consts]
        + [_layer_spec(g.shape, l), _layer_spec(b.shape, l)],
        out_specs=[
            pl.BlockSpec((tb, D), lambda i, j: (i * nb + j, 0)),
            pl.BlockSpec((1, GDN_HEADS, GDN_DK, GDN_DV), lambda i, j: (i, 0, 0, 0)),
            pl.BlockSpec((1, DN_TAIL, 3 * GDN_WIDTH), lambda i, j: (i, 0, 0)),
            pl.BlockSpec((1, CC_TAIL, CC_CH), lambda i, j: (i, 0, 0)),
        ],
        out_shape=[
            jax.ShapeDtypeStruct((T, D), f32),
            jax.ShapeDtypeStruct((batch, GDN_HEADS, GDN_DK, GDN_DV), f32),
            jax.ShapeDtypeStruct((batch, DN_TAIL, 3 * GDN_WIDTH), f32),
            jax.ShapeDtypeStruct((batch, CC_TAIL, CC_CH), f32),
        ],
        scratch_shapes=[
            pltpu.VMEM((tb + DN_TAIL, 3 * GDN_WIDTH), f32),
            pltpu.VMEM((tb + CC_TAIL, CC_CH), f32),
            pltpu.VMEM((GDN_HEADS, GDN_DK, GDN_DV), f32),
            pltpu.VMEM((tb, GDN_WIDTH + CC_CH), bf16),
        ],
        compiler_params=_cparams("parallel", "arbitrary"),
        name="mixer0_ln",
    )(x, *consts, g, b)


def _mixer0_dec_kernel(x_ref, s_ref, dnc_ref, ccc_ref, wmain_ref, wba_ref, dnw_ref, alog_ref, dtb_ref, ng_ref, ccw_ref,
                       ccb_ref, cclg_ref, cclb_ref, wout_ref, g_ref, b_ref,
                       o_ref, s_out_ref, dnc_out_ref, ccc_out_ref,
                       q_scr, k_scr, v_scr, z_scr, beta_scr, eg_scr, oc_scr, *, bs):
    i = pl.program_id(0)
    nq = 3 * GDN_WIDTH

    @pl.when(i == 0)
    def _():
        xb = x_ref[...].astype(bf16)
        proj = jnp.dot(xb, wmain_ref[...], preferred_element_type=f32)
        ba = jnp.dot(xb, wba_ref[...], preferred_element_type=f32)
        qkv_raw = proj[:, :nq]
        acc = dnw_ref[SCONV_W - 1:SCONV_W, :] * qkv_raw
        for j in range(SCONV_W - 1):
            acc = acc + dnw_ref[j:j + 1, :] * dnc_ref[:, j, :]
        for j in range(SCONV_W - 2):
            dnc_out_ref[:, j, :] = dnc_ref[:, j + 1, :]
        dnc_out_ref[:, SCONV_W - 2, :] = qkv_raw
        qkv = _silu(acc)
        for h in range(GDN_HEADS):
            hs = slice(h * GDN_DK, (h + 1) * GDN_DK)
            q_scr[:, hs] = _l2n(qkv[:, h * GDN_DK:(h + 1) * GDN_DK]) * (GDN_DK ** -0.5)
            k_scr[:, hs] = _l2n(qkv[:, GDN_WIDTH + h * GDN_DK:GDN_WIDTH + (h + 1) * GDN_DK])
        v_scr[...] = qkv[:, 2 * GDN_WIDTH:]
        z_scr[...] = _silu(proj[:, nq:nq + GDN_WIDTH])
        beta, g = _gdn_gates(ba, alog_ref[...], dtb_ref[...])
        beta_scr[...] = beta
        eg_scr[...] = jnp.exp(g)

        ga = proj[:, nq + GDN_WIDTH:nq + GDN_WIDTH + CC_CH]
        gb = proj[:, nq + GDN_WIDTH + CC_CH:]
        glu = ga * jax.nn.sigmoid(gb)
        acc = ccw_ref[CC_W - 1:CC_W, :] * glu
        for j in range(CC_W - 1):
            acc = acc + ccw_ref[j:j + 1, :] * ccc_ref[:, j, :]
        for j in range(CC_W - 2):
            ccc_out_ref[:, j, :] = ccc_ref[:, j + 1, :]
        ccc_out_ref[:, CC_W - 2, :] = glu
        cc = _silu(_ln(acc + ccb_ref[...], cclg_ref[...], cclb_ref[...]))
        oc_scr[:, GDN_WIDTH:] = cc

    rows = pl.ds(pl.multiple_of(i * bs, bs), bs)
    q_blk, k_blk, v_blk, z_blk = q_scr[rows, :], k_scr[rows, :], v_scr[rows, :], z_scr[rows, :]
    beta_blk, eg_blk = beta_scr[rows, :], eg_scr[rows, :]
    o_rows = []
    for s in range(bs):
        o_heads = []
        for h in range(GDN_HEADS):
            hs = slice(h * GDN_DK, (h + 1) * GDN_DK)
            k_col = jnp.broadcast_to(k_blk[s:s + 1, hs], (GDN_DK, GDN_DK)).T
            q_col = jnp.broadcast_to(q_blk[s:s + 1, hs], (GDN_DK, GDN_DK)).T
            b1 = beta_blk[s:s + 1, h:h + 1]
            e1 = eg_blk[s:s + 1, h:h + 1]
            s_old = s_ref[s, h]
            ks = jnp.sum(k_col * s_old, 0, keepdims=True)
            v_new = b1 * (v_blk[s:s + 1, hs] - e1 * ks)
            s_new = s_old * e1 + k_col * v_new
            s_out_ref[s, h] = s_new
            o = jnp.sum(q_col * s_new, 0, keepdims=True)
            o = o * lax.rsqrt(jnp.mean(o * o, -1, keepdims=True) + 1e-6) * ng_ref[...]
            o_heads.append(o * z_blk[s:s + 1, hs])
        o_rows.append(jnp.concatenate(o_heads, axis=-1))
    oc_scr[rows, :GDN_WIDTH] = jnp.concatenate(o_rows, axis=0)

    @pl.when(i == pl.num_programs(0) - 1)
    def _():
        y = jnp.dot(oc_scr[...].astype(bf16), wout_ref[...], preferred_element_type=f32)
        o_ref[...] = _ln(DN_ALPHA * x_ref[...] + y, g_ref[MIX_LN_ROW:MIX_LN_ROW + 1, :], b_ref[MIX_LN_ROW:MIX_LN_ROW + 1, :])


def _mixer0_dec_ln(x, s, dnc, ccc, wmain, wba, dnw, alog, dtb, ng, ccw, ccb, cclg, cclb, wout, g, b, *, l, bs=8):
    N, D = x.shape
    consts = [wmain, wba, dnw, alog, dtb, ng, ccw, ccb, cclg, cclb, wout]
    sspec = pl.BlockSpec((bs, GDN_HEADS, GDN_DK, GDN_DV), lambda i: (i, 0, 0, 0))
    return pl.pallas_call(
        functools.partial(_mixer0_dec_kernel, bs=bs),
        grid=(N // bs,),
        in_specs=[_full((N, D)), sspec, _full(dnc.shape), _full(ccc.shape)] + [_full(c.shape) for c in consts]
        + [_layer_spec(g.shape, l), _layer_spec(b.shape, l)],
        out_specs=[_full((N, D)), sspec, _full(dnc.shape), _full(ccc.shape)],
        out_shape=[jax.ShapeDtypeStruct((N, D), f32), jax.ShapeDtypeStruct(s.shape, f32),
                   jax.ShapeDtypeStruct(dnc.shape, f32), jax.ShapeDtypeStruct(ccc.shape, f32)],
        scratch_shapes=[pltpu.VMEM((N, GDN_WIDTH), f32)] * 4 + [pltpu.VMEM((N, LANES), f32)] * 2
        + [pltpu.VMEM((N, GDN_WIDTH + CC_CH), f32)],
        compiler_params=_cparams("arbitrary"),
        name="mixer0_dec_ln",
    )(x, s, dnc, ccc, *consts, g, b)


def _pad_lanes(v, n=LANES):
    return jnp.zeros((1, n), f32).at[0, :v.shape[0]].set(v.astype(f32))


def kernel(x_prompt, x_sample, mem_prompt, cache_mem_k, cache_mem_v, state_dn_S, state_dn_conv, state_cc_conv, ln_g, ln_b, ffn_w_gate, ffn_w_up, ffn_w_down, xa_wq, xa_wk, xa_wv, xa_wo, ab_w_in, dn_conv_w, dn_A_log, dn_dt_bias, dn_norm_g, cc_conv_w, cc_conv_b, cc_ln_g, cc_ln_b, ab_w_out, gm_w_in, gm_ln_g, gm_ln_b, gm_w_s, gm_b_s, gm_w_out):
    B, SEQ, D = x_prompt.shape
    N = x_sample.shape[0]
    row = lambda v: v.reshape(1, -1).astype(f32)

    wg, wu, wd = ffn_w_gate.astype(bf16), ffn_w_up.astype(bf16), ffn_w_down.astype(bf16)
    wq, wk, wv, wo = xa_wq.astype(bf16), xa_wk.astype(bf16), xa_wv.astype(bf16), xa_wo.astype(bf16)
    nq = 3 * GDN_WIDTH
    nz = nq + GDN_WIDTH
    w_main = jnp.concatenate([ab_w_in[:, :nz], ab_w_in[:, nz + 2 * GDN_HEADS:]], axis=1).astype(bf16)
    w_ba = jnp.zeros((D, 2 * LANES), f32)
    w_ba = w_ba.at[:, :GDN_HEADS].set(ab_w_in[:, nz:nz + GDN_HEADS])
    w_ba = w_ba.at[:, LANES:LANES + GDN_HEADS].set(ab_w_in[:, nz + GDN_HEADS:nz + 2 * GDN_HEADS]).astype(bf16)
    w_out = ab_w_out.astype(bf16)
    gw_in, gw_out = gm_w_in.astype(bf16), gm_w_out.astype(bf16)
    mixer_consts = (w_main, w_ba, dn_conv_w.astype(f32), _pad_lanes(dn_A_log), _pad_lanes(dn_dt_bias), row(dn_norm_g),
                    cc_conv_w.astype(f32), row(cc_conv_b), row(cc_ln_g), row(cc_ln_b), w_out)
    gm_consts = (gw_in, gm_w_s.astype(f32), gm_b_s.astype(f32)[:, :, None], row(gm_ln_g), row(gm_ln_b), gw_out)
    lng, lnb = ln_g.astype(f32), ln_b.astype(f32)

    mem_k, mem_v = _mem_kv(mem_prompt, wk, wv)

    def trunk(x, mk, mv, *, tm, prompt, state=None):
        gm_v = None
        for l in range(DEPTH):
            x = _ffn_ln(x, wg, wu, wd, lng, lnb, l=l, half=0, tm=tm)
            if l % 2 == 0:
                if prompt:
                    x, dn_s, dn_c, cc_c = _mixer0_ln(x, *mixer_consts, lng, lnb, l=l, batch=B, seq=SEQ)
                else:
                    x, dn_s, dn_c, cc_c = _mixer0_dec_ln(x, *state, *mixer_consts, lng, lnb, l=l)
            else:
                x, gm_v = _gmlp_ln(x, *gm_consts, lng, lnb, l=l, tm=min(tm, 512), single=not prompt)
            if prompt:
                x = _xattn_ln(x, mk, mv, wq, wo, lng, lnb, l=l, seq=SEQ)
            else:
                x = _xattn_dec_ln(x, mk, mv, wq, wo, lng, lnb, l=l)
            x = _ffn_ln(x, wg, wu, wd, lng, lnb, l=l, half=1, tm=tm)
        return x, dn_s, dn_c, cc_c, gm_v

    yp, dn_s_p, dn_c_p, cc_c_p, _ = trunk(x_prompt.reshape(B * SEQ, D), mem_k, mem_v, tm=1024, prompt=True)
    ys, dn_s_s, dn_c_s, cc_c_s, gm_v_s = trunk(
        x_sample.reshape(N, D), cache_mem_k, cache_mem_v, tm=N, prompt=False,
        state=(state_dn_S, state_dn_conv, state_cc_conv))

    return (yp.reshape(B, SEQ, D), ys.reshape(N, 1, D), mem_k, mem_v,
            dn_s_p, dn_c_p[:, DN_TAIL - (SCONV_W - 1):], cc_c_p[:, CC_TAIL - (CC_W - 1):],
            dn_s_s, dn_c_s, cc_c_s, gm_v_s.reshape(N, 1, GM_WIDTH))
```

```python
import functools

import jax
import jax.numpy as jnp
from jax import lax
from jax.experimental import pallas as pl
from jax.experimental.pallas import tpu as pltpu

D_MODEL = 1024
DEPTH = 2
DN_ALPHA = (2 * DEPTH) ** 0.25
LN_EPS = 1e-5
GDN_HEADS = 4
GDN_DK = 128
GDN_DV = 128
GDN_WIDTH = GDN_HEADS * GDN_DK
SCONV_W = 4
CC_CH = D_MODEL // 2
CC_W = 31
GM_WIDTH = D_MODEL
GM_GROUPS = 4
GM_CHUNK = 128
N_MEM = 256
XA_HEADS = 4
XA_HEAD_DIM = D_MODEL // XA_HEADS
D_FF = 2816
MIX_LN_ROW = 1
XA_LN_ROW = 2

LANES = 128
SUBLANES = 8
GDN_BLOCK = 128
DN_TAIL = SUBLANES
CC_TAIL = 32
CC_ROWS = 32
DN_ROWS = 16
VMEM_LIMIT = 56 * 1024 * 1024

bf16 = jnp.bfloat16
f32 = jnp.float32


def _cparams(*sem):
    return pltpu.CompilerParams(dimension_semantics=sem, vmem_limit_bytes=VMEM_LIMIT)


def _full(shape):
    n = len(shape)
    return pl.BlockSpec(shape, lambda *_: (0,) * n)


def _dot_nt(a, b):
    return lax.dot_general(a.astype(bf16), b.astype(bf16), (((1,), (1,)), ((), ())), preferred_element_type=f32)


def _split3(a):
    a1 = a.astype(bf16)
    r = a - a1.astype(f32)
    a2 = r.astype(bf16)
    a3 = (r - a2.astype(f32)).astype(bf16)
    return a1, a2, a3


def _dot_exact_lhs(a_exact_bf16, b):
    b1, b2, b3 = _split3(b)
    d = lambda y: jnp.dot(a_exact_bf16, y, preferred_element_type=f32)
    return d(b1) + (d(b2) + d(b3))


def _ln(y, g, b):
    mu = jnp.mean(y, -1, keepdims=True)
    d = y - mu
    var = jnp.mean(d * d, -1, keepdims=True)
    return d * lax.rsqrt(var + LN_EPS) * g + b


def _silu(x):
    return x * jax.nn.sigmoid(x)


def _softplus(x):
    return jnp.maximum(x, 0.0) + jnp.log(1.0 + jnp.exp(-jnp.abs(x)))


def _ffn_kernel(x_ref, wg_ref, wu_ref, wd_ref, g_ref, b_ref, o_ref, xb_ref, acc_ref, *, ln_row):
    j = pl.program_id(1)
    r = slice(ln_row, ln_row + 1)

    @pl.when(j == 0)
    def _():
        xb_ref[...] = x_ref[...].astype(bf16)
        acc_ref[...] = jnp.zeros_like(acc_ref)

    xb = xb_ref[...]
    hg = jnp.dot(xb, wg_ref[...], preferred_element_type=f32)
    hu = jnp.dot(xb, wu_ref[...], preferred_element_type=f32)
    h = (_silu(hg) * hu).astype(bf16)
    acc_ref[...] += jnp.dot(h, wd_ref[...], preferred_element_type=f32)

    @pl.when(j == pl.num_programs(1) - 1)
    def _():
        o_ref[...] = _ln(DN_ALPHA * x_ref[...] + 0.5 * acc_ref[...], g_ref[r, :], b_ref[r, :])


def _ffn_ln(x, wg, wu, wd, g, b, *, l, half, tm, tf=256):
    T, D = x.shape
    tm = min(tm, T)
    F = wg.shape[-1]
    nln = g.shape[1]
    return pl.pallas_call(
        functools.partial(_ffn_kernel, ln_row=(nln - 1) * half),
        grid=(T // tm, F // tf),
        in_specs=[
            pl.BlockSpec((tm, D), lambda i, j: (i, 0)),
            pl.BlockSpec((None, None, D, tf), lambda i, j: (l, half, 0, j)),
            pl.BlockSpec((None, None, D, tf), lambda i, j: (l, half, 0, j)),
            pl.BlockSpec((None, None, tf, D), lambda i, j: (l, half, j, 0)),
            pl.BlockSpec((None, nln, D), lambda i, j: (l, 0, 0)),
            pl.BlockSpec((None, nln, D), lambda i, j: (l, 0, 0)),
        ],
        out_specs=pl.BlockSpec((tm, D), lambda i, j: (i, 0)),
        out_shape=jax.ShapeDtypeStruct((T, D), f32),
        scratch_shapes=[pltpu.VMEM((tm, D), bf16), pltpu.VMEM((tm, D), f32)],
        compiler_params=_cparams("parallel", "arbitrary"),
        name="ffn_ln",
    )(x, wg, wu, wd, g, b)


def _memkv_kernel(m_ref, wk_ref, wv_ref, k_ref, v_ref):
    nb = m_ref.shape[0]
    for b in range(nb):
        mb = m_ref[b].astype(bf16)
        k = jnp.dot(mb, wk_ref[...], preferred_element_type=f32)
        v = jnp.dot(mb, wv_ref[...], preferred_element_type=f32)
        for h in range(XA_HEADS):
            sl = slice(h * XA_HEAD_DIM, (h + 1) * XA_HEAD_DIM)
            k_ref[b, :, h, :] = k[:, sl]
            v_ref[b, :, h, :] = v[:, sl]


def _mem_kv(mem, wk, wv, *, nb=2):
    B, M, D = mem.shape
    nb = min(nb, B)
    L = wk.shape[0]
    out = jax.ShapeDtypeStruct((L, B, M, XA_HEADS, XA_HEAD_DIM), f32)
    ospec = pl.BlockSpec((None, nb, M, XA_HEADS, XA_HEAD_DIM), lambda l, i: (l, i, 0, 0, 0))
    return pl.pallas_call(
        _memkv_kernel,
        grid=(L, B // nb),
        in_specs=[
            pl.BlockSpec((nb, M, D), lambda l, i: (i, 0, 0)),
            pl.BlockSpec((None, D, D), lambda l, i: (l, 0, 0)),
            pl.BlockSpec((None, D, D), lambda l, i: (l, 0, 0)),
        ],
        out_specs=[ospec, ospec],
        out_shape=[out, out],
        compiler_params=_cparams("parallel", "parallel"),
        name="mem_kv",
    )(mem, wk, wv)


def _xattn_kernel(x_ref, k_ref, v_ref, wq_ref, wo_ref, g_ref, b_ref, o_ref, oh_ref):
    x = x_ref[...]
    q = jnp.dot(x.astype(bf16), wq_ref[...], preferred_element_type=f32).astype(bf16)
    for h in range(XA_HEADS):
        sl = slice(h * XA_HEAD_DIM, (h + 1) * XA_HEAD_DIM)
        s = _dot_nt(q[:, sl], k_ref[:, h, :]) * (XA_HEAD_DIM ** -0.5)
        s = s - jnp.max(s, -1, keepdims=True)
        e = jnp.exp(s)
        p = e / jnp.sum(e, -1, keepdims=True)
        oh_ref[:, sl] = jnp.dot(p.astype(bf16), v_ref[:, h, :].astype(bf16), preferred_element_type=f32).astype(bf16)
    att = jnp.dot(oh_ref[...], wo_ref[...], preferred_element_type=f32)
    o_ref[...] = _ln(DN_ALPHA * x + att, g_ref[XA_LN_ROW:XA_LN_ROW + 1, :], b_ref[XA_LN_ROW:XA_LN_ROW + 1, :])


def _layer_spec(shape, l):
    n = len(shape)
    return pl.BlockSpec((None,) + tuple(shape[1:]), lambda *_: (l,) + (0,) * (n - 1))


def _xattn_ln(x, mk, mv, wq, wo, g, b, *, l, seq, tq=512):
    T, D = x.shape
    tq = min(tq, seq)
    nq = seq // tq
    mspec = pl.BlockSpec((None, None, N_MEM, XA_HEADS, XA_HEAD_DIM), lambda i: (l, i // nq, 0, 0, 0))
    return pl.pallas_call(
        _xattn_kernel,
        grid=(T // tq,),
        in_specs=[
            pl.BlockSpec((tq, D), lambda i: (i, 0)),
            mspec, mspec,
            _layer_spec(wq.shape, l), _layer_spec(wo.shape, l), _layer_spec(g.shape, l), _layer_spec(b.shape, l),
        ],
        out_specs=pl.BlockSpec((tq, D), lambda i: (i, 0)),
        out_shape=jax.ShapeDtypeStruct((T, D), f32),
        scratch_shapes=[pltpu.VMEM((tq, D), bf16)],
        compiler_params=_cparams("parallel"),
        name="xattn_ln",
    )(x, mk, mv, wq, wo, g, b)


def _xattn_dec_kernel(x_ref, k_ref, v_ref, wq_ref, wo_ref, g_ref, b_ref, o_ref, q_scr, a_scr, *, bs):
    i = pl.program_id(0)

    @pl.when(i == 0)
    def _():
        q_scr[...] = jnp.dot(x_ref[...].astype(bf16), wq_ref[...], preferred_element_type=f32)

    rows = pl.ds(pl.multiple_of(i * bs, bs), bs)
    q_blk = q_scr[rows, :]
    o_rows = []
    for s in range(bs):
        q4 = jnp.concatenate([q_blk[s:s + 1, h * XA_HEAD_DIM:(h + 1) * XA_HEAD_DIM] for h in range(XA_HEADS)], axis=0)
        sc = jnp.sum(k_ref[s] * q4[None], -1, keepdims=True) * (XA_HEAD_DIM ** -0.5)
        e = jnp.exp(sc - jnp.max(sc, 0, keepdims=True))
        p = e / jnp.sum(e, 0, keepdims=True)
        o4 = jnp.sum(p * v_ref[s], 0)
        o_rows.append(jnp.concatenate([o4[h:h + 1, :] for h in range(XA_HEADS)], axis=-1))
    a_scr[rows, :] = jnp.concatenate(o_rows, axis=0)

    @pl.when(i == pl.num_programs(0) - 1)
    def _():
        att = jnp.dot(a_scr[...].astype(bf16), wo_ref[...], preferred_element_type=f32)
        o_ref[...] = _ln(DN_ALPHA * x_ref[...] + att, g_ref[XA_LN_ROW:XA_LN_ROW + 1, :], b_ref[XA_LN_ROW:XA_LN_ROW + 1, :])


def _xattn_dec_ln(x, ck, cv, wq, wo, g, b, *, l, bs=8):
    N, D = x.shape
    cspec = pl.BlockSpec((None, bs, N_MEM, XA_HEADS, XA_HEAD_DIM), lambda i: (l, i, 0, 0, 0))
    return pl.pallas_call(
        functools.partial(_xattn_dec_kernel, bs=bs),
        grid=(N // bs,),
        in_specs=[
            _full((N, D)), cspec, cspec,
            _layer_spec(wq.shape, l), _layer_spec(wo.shape, l), _layer_spec(g.shape, l), _layer_spec(b.shape, l),
        ],
        out_specs=_full((N, D)),
        out_shape=jax.ShapeDtypeStruct((N, D), f32),
        scratch_shapes=[pltpu.VMEM((N, D), f32), pltpu.VMEM((N, D), f32)],
        compiler_params=_cparams("arbitrary"),
        name="xattn_dec_ln",
    )(x, ck, cv, wq, wo, g, b)


def _gmlp_kernel(x_ref, win_ref, ws_ref, bs_ref, vg_ref, vb_ref, wout_ref, g_ref, b_ref, o_ref, v_ref, uf_ref, *, single):
    x = x_ref[...]
    tm = x.shape[0]
    pr = jax.nn.gelu(jnp.dot(x.astype(bf16), win_ref[...], preferred_element_type=f32))
    u = pr[:, :GM_WIDTH]
    v = _ln(pr[:, GM_WIDTH:], vg_ref[...], vb_ref[...])
    v_ref[...] = v
    gw = GM_WIDTH // GM_GROUPS
    if single:
        for g in range(GM_GROUPS):
            sl = slice(g * gw, (g + 1) * gw)
            f = ws_ref[g][0:1, 0:1] * v[:, sl] + bs_ref[g][0:1, 0:1]
            uf_ref[:, sl] = (u[:, sl] * f).astype(bf16)
    else:
        row = lax.broadcasted_iota(jnp.int32, (GM_CHUNK, GM_CHUNK), 0)
        col = lax.broadcasted_iota(jnp.int32, (GM_CHUNK, GM_CHUNK), 1)
        vb16 = v.astype(bf16)
        for g in range(GM_GROUPS):
            sl = slice(g * gw, (g + 1) * gw)
            wmask = jnp.where(col <= row, ws_ref[g], 0.0).astype(bf16)
            bias = bs_ref[g]
            for c in range(tm // GM_CHUNK):
                rs = slice(c * GM_CHUNK, (c + 1) * GM_CHUNK)
                f = jnp.dot(wmask, vb16[rs, sl], preferred_element_type=f32) + bias
                uf_ref[rs, sl] = (u[rs, sl] * f).astype(bf16)
    y = jnp.dot(uf_ref[...], wout_ref[...], preferred_element_type=f32)
    o_ref[...] = _ln(DN_ALPHA * x + y, g_ref[MIX_LN_ROW:MIX_LN_ROW + 1, :], b_ref[MIX_LN_ROW:MIX_LN_ROW + 1, :])


def _gmlp_ln(x, w_in, w_s, b_s, vg, vb, w_out, g, b, *, l, tm, single):
    T, D = x.shape
    tm = min(tm, T)
    return pl.pallas_call(
        functools.partial(_gmlp_kernel, single=single),
        grid=(T // tm,),
        in_specs=[
            pl.BlockSpec((tm, D), lambda i: (i, 0)),
            _full((D, 2 * GM_WIDTH)), _full(w_s.shape), _full(b_s.shape),
            _full((1, GM_WIDTH)), _full((1, GM_WIDTH)), _full((GM_WIDTH, D)), _layer_spec(g.shape, l), _layer_spec(b.shape, l),
        ],
        out_specs=[pl.BlockSpec((tm, D), lambda i: (i, 0)), pl.BlockSpec((tm, GM_WIDTH), lambda i: (i, 0))],
        out_shape=[jax.ShapeDtypeStruct((T, D), f32), jax.ShapeDtypeStruct((T, GM_WIDTH), f32)],
        scratch_shapes=[pltpu.VMEM((tm, GM_WIDTH), bf16)],
        compiler_params=_cparams("parallel"),
        name="gmlp_ln",
    )(x, w_in, w_s, b_s, vg, vb, w_out, g, b)


def _gdn_gates(ba, alog, dtb):
    beta = jax.nn.sigmoid(ba[:, :LANES])
    g = -jnp.exp(alog) * _softplus(ba[:, LANES:] + dtb)
    return beta, g


def _l2n(x):
    return x * lax.rsqrt(jnp.sum(x * x, -1, keepdims=True) + 1e-6)


def _mixer0_kernel(x_ref, wmain_ref, wba_ref, dnw_ref, alog_ref, dtb_ref, ng_ref, ccw_ref, ccb_ref, cclg_ref, cclb_ref,
                   wout_ref, g_ref, b_ref,
                   o_ref, s_out_ref, dnc_out_ref, ccc_out_ref,
                   qkv_ext, glu_ext, rot_scr, s_scr, oc_scr):
    blk = pl.program_id(1)
    tb = x_ref.shape[0]
    C = GDN_BLOCK

    @pl.when(blk == 0)
    def _():
        qkv_ext[0:DN_TAIL, :] = jnp.zeros((DN_TAIL, 3 * GDN_WIDTH), f32)
        glu_ext[0:CC_TAIL, :] = jnp.zeros((CC_TAIL, CC_CH), f32)
        s_scr[...] = jnp.zeros_like(s_scr)

    x = x_ref[...]
    xb = x.astype(bf16)
    proj = jnp.dot(xb, wmain_ref[...], preferred_element_type=f32)
    ba = jnp.dot(xb, wba_ref[...], preferred_element_type=f32)
    nq = 3 * GDN_WIDTH

    qkv_ext[DN_TAIL:DN_TAIL + tb, :] = proj[:, :nq]
    blocks = []
    for i in range(tb // DN_ROWS):
        acc = None
        for s in range(SCONV_W):
            term = dnw_ref[SCONV_W - 1 - s:SCONV_W - s, :] * qkv_ext[pl.ds(DN_TAIL - s + i * DN_ROWS, DN_ROWS), :]
            acc = term if acc is None else acc + term
        blocks.append(_silu(acc))
    qkv = jnp.concatenate(blocks, axis=0)
    tail = qkv_ext[tb:tb + DN_TAIL, :]
    qkv_ext[0:DN_TAIL, :] = tail
    dnc_out_ref[0] = tail

    beta, g = _gdn_gates(ba, alog_ref[...], dtb_ref[...])

    P = 2 * GDN_DK
    pairs = range(GDN_HEADS // 2)
    chunks = range(tb // C)
    row = lax.broadcasted_iota(jnp.int32, (C, P), 0)
    col = lax.broadcasted_iota(jnp.int32, (C, P), 1)
    col = jnp.where(col >= C, col - C, col)
    causal = col <= row
    strict = col < row
    eye2 = jnp.where(col == row, 1.0, 0.0)
    ltri = jnp.where(causal[:, :C], 1.0, 0.0).astype(bf16)
    qn = [_l2n(qkv[:, h * GDN_DK:(h + 1) * GDN_DK]) * (GDN_DK ** -0.5) for h in range(GDN_HEADS)]
    kn = [_l2n(qkv[:, GDN_WIDTH + h * GDN_DK:GDN_WIDTH + (h + 1) * GDN_DK]) for h in range(GDN_HEADS)]

    def pair_cols(m, h0):
        return jnp.concatenate([jnp.broadcast_to(m[:, h0:h0 + 1], (C, GDN_DK)),
                                jnp.broadcast_to(m[:, h0 + 1:h0 + 2], (C, GDN_DK))], axis=1)

    def bdiag(m):
        z = jnp.zeros((C, C), m.dtype)
        return jnp.concatenate([jnp.concatenate([m[:, :C], z], axis=1), jnp.concatenate([z, m[:, C:]], axis=1)], axis=0)

    def split2(a):
        a1 = a.astype(bf16)
        return a1, (a - a1.astype(f32)).astype(bf16)

    mm = lambda a, b_: jnp.dot(a, b_, preferred_element_type=f32)

    prob = {}
    for c in chunks:
        rs = slice(c * C, (c + 1) * C)
        gc = _dot_exact_lhs(ltri, g[rs])
        gct = gc.T
        eg = jnp.exp(gc)
        g_last = gc[C - 1:C, :]
        ekt = jnp.exp(g_last - gc)
        egl = jnp.exp(g_last)
        for pr in pairs:
            h0 = 2 * pr
            k_pair = jnp.concatenate([kn[h0][rs], kn[h0 + 1][rs]], axis=1)
            q_pair = jnp.concatenate([qn[h0][rs], qn[h0 + 1][rs]], axis=1)
            v_pair = qkv[rs, 2 * GDN_WIDTH + pr * P:2 * GDN_WIDTH + (pr + 1) * P]
            beta_pair = pair_cols(beta[rs], h0)
            grow = jnp.concatenate([gct[h0:h0 + 1, :], gct[h0 + 1:h0 + 2, :]], axis=1)
            decay = jnp.where(causal, jnp.exp(jnp.where(causal, pair_cols(gc, h0) - grow, 0.0)), 0.0)
            kb = k_pair * beta_pair
            prod = _dot_nt(jnp.concatenate([kb, q_pair], axis=0), bdiag(k_pair.astype(bf16)))
            eg_pair = pair_cols(eg, h0)
            kt = k_pair * pair_cols(ekt, h0)
            prob[c, pr] = dict(
                n=-jnp.where(strict, prod[:C] * decay, 0.0),
                qk=(prod[C:] * decay).astype(bf16),
                vb=(v_pair * beta_pair).astype(bf16),
                kbe=(kb * eg_pair).astype(bf16),
                qg=(q_pair * eg_pair).astype(bf16),
                ktt=jnp.concatenate([kt[:, :C].T, kt[:, C:].T], axis=1).astype(bf16),
                egl=jnp.concatenate([jnp.broadcast_to(egl[:, h0:h0 + 1], (1, GDN_DK)),
                                     jnp.broadcast_to(egl[:, h0 + 1:h0 + 2], (1, GDN_DK))], axis=1))

    pk = {key: eye2 + pb["n"] for key, pb in prob.items()}
    nk = {key: pb["n"] for key, pb in prob.items()}
    levels = C.bit_length() - 1
    for j in range(levels):
        first, last = j == 0, j == levels - 1
        for key in prob:
            n1, n2 = split2(nk[key])
            if first:
                l1, l2 = n1, n2
            else:
                p1, p2 = split2(pk[key])
                l1 = p1 if last else jnp.concatenate([n1, p1], axis=0)
                l2 = p2 if last else jnp.concatenate([n2, p2], axis=0)
            r = l1.shape[0]
            o = mm(jnp.concatenate([l1, l2], axis=0), bdiag(n1))
            res = o[:r] + (o[r:] + mm(l1, bdiag(n2)))
            if first:
                nk[key] = res
            elif last:
                pk[key] = pk[key] + res
            else:
                nk[key] = res[:C]
                pk[key] = pk[key] + res[C:]

    for key, pb in prob.items():
        t16 = pk[key].astype(bf16)
        pb["u"] = mm(t16, bdiag(pb["vb"]))
        pb["w"] = mm(t16, bdiag(pb["kbe"])).astype(bf16)

    for c in chunks:
        rs = slice(c * C, (c + 1) * C)
        for pr in pairs:
            pb = prob[c, pr]
            s_pair = s_scr[pr]
            o2 = mm(jnp.concatenate([pb["w"], pb["qg"]], axis=0), bdiag(s_pair.astype(bf16)))
            vbd = bdiag((pb["u"] - o2[:C]).astype(bf16))
            o = o2[C:] + mm(pb["qk"], vbd)
            s_scr[pr] = s_pair * pb["egl"] + mm(pb["ktt"], vbd)
            on = [o[:, j * GDN_DV:(j + 1) * GDN_DV] for j in range(2)]
            on = [t * lax.rsqrt(jnp.mean(t * t, -1, keepdims=True) + 1e-6) * ng_ref[...] for t in on]
            z_pair = proj[rs, nq + pr * P:nq + (pr + 1) * P]
            oc_scr[rs, pr * P:(pr + 1) * P] = (jnp.concatenate(on, axis=1) * _silu(z_pair)).astype(bf16)

    for pr in pairs:
        s_out_ref[0, 2 * pr] = s_scr[pr][:, :GDN_DV]
        s_out_ref[0, 2 * pr + 1] = s_scr[pr][:, GDN_DV:]

    ga = proj[:, nq + GDN_WIDTH:nq + GDN_WIDTH + CC_CH]
    gb = proj[:, nq + GDN_WIDTH + CC_CH:]
    glu = ga * jax.nn.sigmoid(gb)
    glu_ext[CC_TAIL:CC_TAIL + tb, :] = glu
    span = CC_TAIL - SUBLANES
    for r in range(1, SUBLANES):
        rot_scr[r - 1] = glu_ext[pl.ds(SUBLANES - r, tb + span), :]

    def cc_rows(i, carry):
        base = pl.multiple_of(i * CC_ROWS, CC_ROWS)
        acc = None
        for s in range(CC_W):
            a, r = divmod(s, SUBLANES)
            off = span - SUBLANES * a
            if r == 0:
                xs = glu_ext[pl.ds(base + SUBLANES + off, CC_ROWS), :]
            else:
                xs = rot_scr[r - 1, pl.ds(base + off, CC_ROWS), :]
            term = ccw_ref[CC_W - 1 - s:CC_W - s, :] * xs
            acc = term if acc is None else acc + term
        cc = _silu(_ln(acc + ccb_ref[...], cclg_ref[...], cclb_ref[...]))
        oc_scr[pl.ds(base, CC_ROWS), GDN_WIDTH:] = cc.astype(bf16)
        return carry

    lax.fori_loop(0, tb // CC_ROWS, cc_rows, 0)
    tail = glu_ext[tb:tb + CC_TAIL, :]
    glu_ext[0:CC_TAIL, :] = tail
    ccc_out_ref[0] = tail

    y = jnp.dot(oc_scr[...], wout_ref[...], preferred_element_type=f32)
    o_ref[...] = _ln(DN_ALPHA * x + y, g_ref[MIX_LN_ROW:MIX_LN_ROW + 1, :], b_ref[MIX_LN_ROW:MIX_LN_ROW + 1, :])


def _mixer0_ln(x, wmain, wba, dnw, alog, dtb, ng, ccw, ccb, cclg, cclb, wout, g, b, *, l, batch, seq, tb=512):
    T, D = x.shape
    tb = min(tb, seq)
    nb = seq // tb
    consts = [wmain, wba, dnw, alog, dtb, ng, ccw, ccb, cclg, cclb, wout]
    return pl.pallas_call(
        _mixer0_kernel,
        grid=(batch, nb),
        in_specs=[pl.BlockSpec((tb, D), lambda i, j: (i * nb + j, 0))] + [_full(c.shape) for c in consts]
        + [_layer_spec(g.shape, l), _layer_spec(b.shape, l)],
        out_specs=[
            pl.BlockSpec((tb, D), lambda i, j: (i * nb + j, 0)),
            pl.BlockSpec((1, GDN_HEADS, GDN_DK, GDN_DV), lambda i, j: (i, 0, 0, 0)),
            pl.BlockSpec((1, DN_TAIL, 3 * GDN_WIDTH), lambda i, j: (i, 0, 0)),
            pl.BlockSpec((1, CC_TAIL, CC_CH), lambda i, j: (i, 0, 0)),
        ],
        out_shape=[
            jax.ShapeDtypeStruct((T, D), f32),
            jax.ShapeDtypeStruct((batch, GDN_HEADS, GDN_DK, GDN_DV), f32),
            jax.ShapeDtypeStruct((batch, DN_TAIL, 3 * GDN_WIDTH), f32),
            jax.ShapeDtypeStruct((batch, CC_TAIL, CC_CH), f32),
        ],
        scratch_shapes=[
            pltpu.VMEM((tb + DN_TAIL, 3 * GDN_WIDTH), f32),
            pltpu.VMEM((tb + CC_TAIL, CC_CH), f32),
            pltpu.VMEM((SUBLANES - 1, tb + CC_TAIL - SUBLANES, CC_CH), f32),
            pltpu.VMEM((GDN_HEADS // 2, GDN_DK, 2 * GDN_DV), f32),
            pltpu.VMEM((tb, GDN_WIDTH + CC_CH), bf16),
        ],
        compiler_params=_cparams("parallel", "arbitrary"),
        name="mixer0_ln",
    )(x, *consts, g, b)


def _mixer0_dec_kernel(x_ref, s_ref, dnc_ref, ccc_ref, wmain_ref, wba_ref, dnw_ref, alog_ref, dtb_ref, ng_ref, ccw_ref,
                       ccb_ref, cclg_ref, cclb_ref, wout_ref, g_ref, b_ref,
                       o_ref, s_out_ref, dnc_out_ref, ccc_out_ref,
                       q_scr, k_scr, v_scr, z_scr, beta_scr, eg_scr, oc_scr, *, bs):
    i = pl.program_id(0)
    nq = 3 * GDN_WIDTH

    @pl.when(i == 0)
    def _():
        xb = x_ref[...].astype(bf16)
        proj = jnp.dot(xb, wmain_ref[...], preferred_element_type=f32)
        ba = jnp.dot(xb, wba_ref[...], preferred_element_type=f32)
        qkv_raw = proj[:, :nq]
        acc = dnw_ref[SCONV_W - 1:SCONV_W, :] * qkv_raw
        for j in range(SCONV_W - 1):
            acc = acc + dnw_ref[j:j + 1, :] * dnc_ref[:, j, :]
        for j in range(SCONV_W - 2):
            dnc_out_ref[:, j, :] = dnc_ref[:, j + 1, :]
        dnc_out_ref[:, SCONV_W - 2, :] = qkv_raw
        qkv = _silu(acc)
        for h in range(GDN_HEADS):
            hs = slice(h * GDN_DK, (h + 1) * GDN_DK)
            q_scr[:, hs] = _l2n(qkv[:, h * GDN_DK:(h + 1) * GDN_DK]) * (GDN_DK ** -0.5)
            k_scr[:, hs] = _l2n(qkv[:, GDN_WIDTH + h * GDN_DK:GDN_WIDTH + (h + 1) * GDN_DK])
        v_scr[...] = qkv[:, 2 * GDN_WIDTH:]
        z_scr[...] = _silu(proj[:, nq:nq + GDN_WIDTH])
        beta, g = _gdn_gates(ba, alog_ref[...], dtb_ref[...])
        beta_scr[...] = beta
        eg_scr[...] = jnp.exp(g)

        ga = proj[:, nq + GDN_WIDTH:nq + GDN_WIDTH + CC_CH]
        gb = proj[:, nq + GDN_WIDTH + CC_CH:]
        glu = ga * jax.nn.sigmoid(gb)
        acc = ccw_ref[CC_W - 1:CC_W, :] * glu
        for j in range(CC_W - 1):
            acc = acc + ccw_ref[j:j + 1, :] * ccc_ref[:, j, :]
        for j in range(CC_W - 2):
            ccc_out_ref[:, j, :] = ccc_ref[:, j + 1, :]
        ccc_out_ref[:, CC_W - 2, :] = glu
        cc = _silu(_ln(acc + ccb_ref[...], cclg_ref[...], cclb_ref[...]))
        oc_scr[:, GDN_WIDTH:] = cc

    rows = pl.ds(pl.multiple_of(i * bs, bs), bs)
    q_blk, k_blk, v_blk, z_blk = q_scr[rows, :], k_scr[rows, :], v_scr[rows, :], z_scr[rows, :]
    beta_blk, eg_blk = beta_scr[rows, :], eg_scr[rows, :]
    o_rows = []
    for s in range(bs):
        o_heads = []
        for h in range(GDN_HEADS):
            hs = slice(h * GDN_DK, (h + 1) * GDN_DK)
            k_col = jnp.broadcast_to(k_blk[s:s + 1, hs], (GDN_DK, GDN_DK)).T
            q_col = jnp.broadcast_to(q_blk[s:s + 1, hs], (GDN_DK, GDN_DK)).T
            b1 = beta_blk[s:s + 1, h:h + 1]
            e1 = eg_blk[s:s + 1, h:h + 1]
            s_old = s_ref[s, h]
            ks = jnp.sum(k_col * s_old, 0, keepdims=True)
            v_new = b1 * (v_blk[s:s + 1, hs] - e1 * ks)
            s_new = s_old * e1 + k_col * v_new
            s_out_ref[s, h] = s_new
            o = jnp.sum(q_col * s_new, 0, keepdims=True)
            o = o * lax.rsqrt(jnp.mean(o * o, -1, keepdims=True) + 1e-6) * ng_ref[...]
            o_heads.append(o * z_blk[s:s + 1, hs])
        o_rows.append(jnp.concatenate(o_heads, axis=-1))
    oc_scr[rows, :GDN_WIDTH] = jnp.concatenate(o_rows, axis=0)

    @pl.when(i == pl.num_programs(0) - 1)
    def _():
        y = jnp.dot(oc_scr[...].astype(bf16), wout_ref[...], preferred_element_type=f32)
        o_ref[...] = _ln(DN_ALPHA * x_ref[...] + y, g_ref[MIX_LN_ROW:MIX_LN_ROW + 1, :], b_ref[MIX_LN_ROW:MIX_LN_ROW + 1, :])


def _mixer0_dec_ln(x, s, dnc, ccc, wmain, wba, dnw, alog, dtb, ng, ccw, ccb, cclg, cclb, wout, g, b, *, l, bs=8):
    N, D = x.shape
    consts = [wmain, wba, dnw, alog, dtb, ng, ccw, ccb, cclg, cclb, wout]
    sspec = pl.BlockSpec((bs, GDN_HEADS, GDN_DK, GDN_DV), lambda i: (i, 0, 0, 0))
    return pl.pallas_call(
        functools.partial(_mixer0_dec_kernel, bs=bs),
        grid=(N // bs,),
        in_specs=[_full((N, D)), sspec, _full(dnc.shape), _full(ccc.shape)] + [_full(c.shape) for c in consts]
        + [_layer_spec(g.shape, l), _layer_spec(b.shape, l)],
        out_specs=[_full((N, D)), sspec, _full(dnc.shape), _full(ccc.shape)],
        out_shape=[jax.ShapeDtypeStruct((N, D), f32), jax.ShapeDtypeStruct(s.shape, f32),
                   jax.ShapeDtypeStruct(dnc.shape, f32), jax.ShapeDtypeStruct(ccc.shape, f32)],
        scratch_shapes=[pltpu.VMEM((N, GDN_WIDTH), f32)] * 4 + [pltpu.VMEM((N, LANES), f32)] * 2
        + [pltpu.VMEM((N, GDN_WIDTH + CC_CH), f32)],
        compiler_params=_cparams("arbitrary"),
        name="mixer0_dec_ln",
    )(x, s, dnc, ccc, *consts, g, b)


def _pad_lanes(v, n=LANES):
    return jnp.zeros((1, n), f32).at[0, :v.shape[0]].set(v.astype(f32))


def kernel(x_prompt, x_sample, mem_prompt, cache_mem_k, cache_mem_v, state_dn_S, state_dn_conv, state_cc_conv, ln_g, ln_b, ffn_w_gate, ffn_w_up, ffn_w_down, xa_wq, xa_wk, xa_wv, xa_wo, ab_w_in, dn_conv_w, dn_A_log, dn_dt_bias, dn_norm_g, cc_conv_w, cc_conv_b, cc_ln_g, cc_ln_b, ab_w_out, gm_w_in, gm_ln_g, gm_ln_b, gm_w_s, gm_b_s, gm_w_out):
    B, SEQ, D = x_prompt.shape
    N = x_sample.shape[0]
    row = lambda v: v.reshape(1, -1).astype(f32)

    wg, wu, wd = ffn_w_gate.astype(bf16), ffn_w_up.astype(bf16), ffn_w_down.astype(bf16)
    wq, wk, wv, wo = xa_wq.astype(bf16), xa_wk.astype(bf16), xa_wv.astype(bf16), xa_wo.astype(bf16)
    nq = 3 * GDN_WIDTH
    nz = nq + GDN_WIDTH
    w_main = jnp.concatenate([ab_w_in[:, :nz], ab_w_in[:, nz + 2 * GDN_HEADS:]], axis=1).astype(bf16)
    w_ba = jnp.zeros((D, 2 * LANES), f32)
    w_ba = w_ba.at[:, :GDN_HEADS].set(ab_w_in[:, nz:nz + GDN_HEADS])
    w_ba = w_ba.at[:, LANES:LANES + GDN_HEADS].set(ab_w_in[:, nz + GDN_HEADS:nz + 2 * GDN_HEADS]).astype(bf16)
    w_out = ab_w_out.astype(bf16)
    gw_in, gw_out = gm_w_in.astype(bf16), gm_w_out.astype(bf16)
    mixer_consts = (w_main, w_ba, dn_conv_w.astype(f32), _pad_lanes(dn_A_log), _pad_lanes(dn_dt_bias), row(dn_norm_g),
                    cc_conv_w.astype(f32), row(cc_conv_b), row(cc_ln_g), row(cc_ln_b), w_out)
    gm_consts = (gw_in, gm_w_s.astype(f32), gm_b_s.astype(f32)[:, :, None], row(gm_ln_g), row(gm_ln_b), gw_out)
    lng, lnb = ln_g.astype(f32), ln_b.astype(f32)

    mem_k, mem_v = _mem_kv(mem_prompt, wk, wv)

    def trunk(x, mk, mv, *, tm, prompt, state=None):
        gm_v = None
        for l in range(DEPTH):
            x = _ffn_ln(x, wg, wu, wd, lng, lnb, l=l, half=0, tm=tm)
            if l % 2 == 0:
                if prompt:
                    x, dn_s, dn_c, cc_c = _mixer0_ln(x, *mixer_consts, lng, lnb, l=l, batch=B, seq=SEQ)
                else:
                    x, dn_s, dn_c, cc_c = _mixer0_dec_ln(x, *state, *mixer_consts, lng, lnb, l=l)
            else:
                x, gm_v = _gmlp_ln(x, *gm_consts, lng, lnb, l=l, tm=min(tm, 512), single=not prompt)
            if prompt:
                x = _xattn_ln(x, mk, mv, wq, wo, lng, lnb, l=l, seq=SEQ)
            else:
                x = _xattn_dec_ln(x, mk, mv, wq, wo, lng, lnb, l=l)
            x = _ffn_ln(x, wg, wu, wd, lng, lnb, l=l, half=1, tm=tm)
        return x, dn_s, dn_c, cc_c, gm_v

    yp, dn_s_p, dn_c_p, cc_c_p, _ = trunk(x_prompt.reshape(B * SEQ, D), mem_k, mem_v, tm=1024, prompt=True)
    ys, dn_s_s, dn_c_s, cc_c_s, gm_v_s = trunk(
        x_sample.reshape(N, D), cache_mem_k, cache_mem_v, tm=N, prompt=False,
        state=(state_dn_S, state_dn_conv, state_cc_conv))

    return (yp.reshape(B, SEQ, D), ys.reshape(N, 1, D), mem_k, mem_v,
            dn_s_p, dn_c_p[:, DN_TAIL - (SCONV_W - 1):], cc_c_p[:, CC_TAIL - (CC_W - 1):],
            dn_s_s, dn_c_s, cc_c_s, gm_v_s.reshape(N, 1, GM_WIDTH))
```

```python
import functools

import jax
import jax.numpy as jnp
from jax import lax
from jax.experimental import pallas as pl
from jax.experimental.pallas import tpu as pltpu

D_MODEL = 1024
DEPTH = 2
DN_ALPHA = (2 * DEPTH) ** 0.25
LN_EPS = 1e-5
GDN_HEADS = 4
GDN_DK = 128
GDN_DV = 128
GDN_WIDTH = GDN_HEADS * GDN_DK
SCONV_W = 4
CC_CH = D_MODEL // 2
CC_W = 31
GM_WIDTH = D_MODEL
GM_GROUPS = 4
GM_CHUNK = 128
N_MEM = 256
XA_HEADS = 4
XA_HEAD_DIM = D_MODEL // XA_HEADS
D_FF = 2816
MIX_LN_ROW = 1
XA_LN_ROW = 2

LANES = 128
SUBLANES = 8
GDN_BLOCK = 128
DN_TAIL = SUBLANES
CC_TAIL = 32
CC_ROWS = 32
DN_ROWS = 16
VMEM_LIMIT = 56 * 1024 * 1024

bf16 = jnp.bfloat16
f32 = jnp.float32


def _cparams(*sem):
    return pltpu.CompilerParams(dimension_semantics=sem, vmem_limit_bytes=VMEM_LIMIT)


def _full(shape):
    n = len(shape)
    return pl.BlockSpec(shape, lambda *_: (0,) * n)


def _dot_nt(a, b):
    return lax.dot_general(a.astype(bf16), b.astype(bf16), (((1,), (1,)), ((), ())), preferred_element_type=f32)


def _split3(a):
    a1 = a.astype(bf16)
    r = a - a1.astype(f32)
    a2 = r.astype(bf16)
    a3 = (r - a2.astype(f32)).astype(bf16)
    return a1, a2, a3


def _dot_exact_lhs(a_exact_bf16, b):
    b1, b2, b3 = _split3(b)
    d = lambda y: jnp.dot(a_exact_bf16, y, preferred_element_type=f32)
    return d(b1) + (d(b2) + d(b3))


def _ln(y, g, b):
    mu = jnp.mean(y, -1, keepdims=True)
    d = y - mu
    var = jnp.mean(d * d, -1, keepdims=True)
    return d * lax.rsqrt(var + LN_EPS) * g + b


def _silu(x):
    return x * jax.nn.sigmoid(x)


def _softplus(x):
    return jnp.maximum(x, 0.0) + jnp.log(1.0 + jnp.exp(-jnp.abs(x)))


def _ffn_kernel(x_ref, wg_ref, wu_ref, wd_ref, g_ref, b_ref, o_ref, *, ln_row, rows, tf):
    r = slice(ln_row, ln_row + 1)
    tm = x_ref.shape[0]
    nf = wg_ref.shape[1] // tf
    for r0 in range(0, tm, rows):
        x = x_ref[r0:r0 + rows, :]
        xb = x.astype(bf16)
        acc = None
        for j in range(nf):
            cs = slice(j * tf, (j + 1) * tf)
            hg = jnp.dot(xb, wg_ref[:, cs], preferred_element_type=f32)
            hu = jnp.dot(xb, wu_ref[:, cs], preferred_element_type=f32)
            d = jnp.dot((_silu(hg) * hu).astype(bf16), wd_ref[cs, :], preferred_element_type=f32)
            acc = d if acc is None else acc + d
        o_ref[r0:r0 + rows, :] = _ln(DN_ALPHA * x + 0.5 * acc, g_ref[r, :], b_ref[r, :])


def _ffn_ln(x, wg, wu, wd, g, b, *, l, half, tm, rows=512, tf=256):
    T, D = x.shape
    tm = min(tm, T)
    rows = min(rows, tm)
    F = wg.shape[-1]
    nln = g.shape[1]
    once = pl.Buffered(1)
    return pl.pallas_call(
        functools.partial(_ffn_kernel, ln_row=(nln - 1) * half, rows=rows, tf=tf),
        grid=(T // tm,),
        in_specs=[
            pl.BlockSpec((tm, D), lambda i: (i, 0)),
            pl.BlockSpec((None, None, D, F), lambda i: (l, half, 0, 0), pipeline_mode=once),
            pl.BlockSpec((None, None, D, F), lambda i: (l, half, 0, 0), pipeline_mode=once),
            pl.BlockSpec((None, None, F, D), lambda i: (l, half, 0, 0), pipeline_mode=once),
            pl.BlockSpec((None, nln, D), lambda i: (l, 0, 0)),
            pl.BlockSpec((None, nln, D), lambda i: (l, 0, 0)),
        ],
        out_specs=pl.BlockSpec((tm, D), lambda i: (i, 0)),
        out_shape=jax.ShapeDtypeStruct((T, D), f32),
        compiler_params=_cparams("parallel"),
        name="ffn_ln",
    )(x, wg, wu, wd, g, b)


def _memkv_kernel(m_ref, wk_ref, wv_ref, k_ref, v_ref, kh_ref, vh_ref):
    nb = m_ref.shape[0]
    for b in range(nb):
        mb = m_ref[b].astype(bf16)
        k = jnp.dot(mb, wk_ref[...], preferred_element_type=f32)
        v = jnp.dot(mb, wv_ref[...], preferred_element_type=f32)
        for h in range(XA_HEADS):
            sl = slice(h * XA_HEAD_DIM, (h + 1) * XA_HEAD_DIM)
            k_ref[b, :, h, :] = k[:, sl]
            v_ref[b, :, h, :] = v[:, sl]
            kh_ref[b, h] = k[:, sl].astype(bf16)
            vh_ref[b, h] = v[:, sl].astype(bf16)


def _mem_kv(mem, wk, wv, *, nb=2):
    B, M, D = mem.shape
    nb = min(nb, B)
    L = wk.shape[0]
    out = jax.ShapeDtypeStruct((L, B, M, XA_HEADS, XA_HEAD_DIM), f32)
    outh = jax.ShapeDtypeStruct((L, B, XA_HEADS, M, XA_HEAD_DIM), bf16)
    ospec = pl.BlockSpec((None, nb, M, XA_HEADS, XA_HEAD_DIM), lambda l, i: (l, i, 0, 0, 0))
    hspec = pl.BlockSpec((None, nb, XA_HEADS, M, XA_HEAD_DIM), lambda l, i: (l, i, 0, 0, 0))
    return pl.pallas_call(
        _memkv_kernel,
        grid=(L, B // nb),
        in_specs=[
            pl.BlockSpec((nb, M, D), lambda l, i: (i, 0, 0)),
            pl.BlockSpec((None, D, D), lambda l, i: (l, 0, 0)),
            pl.BlockSpec((None, D, D), lambda l, i: (l, 0, 0)),
        ],
        out_specs=[ospec, ospec, hspec, hspec],
        out_shape=[out, out, outh, outh],
        compiler_params=_cparams("parallel", "parallel"),
        name="mem_kv",
    )(mem, wk, wv)


def _xattn_kernel(x_ref, k_ref, v_ref, wq_ref, wo_ref, g_ref, b_ref, o_ref, oh_ref, *, rows):
    for r0 in range(0, x_ref.shape[0], rows):
        rs = slice(r0, r0 + rows)
        x = x_ref[rs, :]
        q = jnp.dot(x.astype(bf16), wq_ref[...], preferred_element_type=f32).astype(bf16)
        for h in range(XA_HEADS):
            sl = slice(h * XA_HEAD_DIM, (h + 1) * XA_HEAD_DIM)
            s = _dot_nt(q[:, sl], k_ref[h]) * (XA_HEAD_DIM ** -0.5)
            s = s - jnp.max(s, -1, keepdims=True)
            e = jnp.exp(s)
            p = e / jnp.sum(e, -1, keepdims=True)
            oh_ref[rs, sl] = jnp.dot(p.astype(bf16), v_ref[h], preferred_element_type=f32).astype(bf16)
        att = jnp.dot(oh_ref[rs, :], wo_ref[...], preferred_element_type=f32)
        o_ref[rs, :] = _ln(DN_ALPHA * x + att, g_ref[XA_LN_ROW:XA_LN_ROW + 1, :], b_ref[XA_LN_ROW:XA_LN_ROW + 1, :])


def _layer_spec(shape, l):
    n = len(shape)
    return pl.BlockSpec((None,) + tuple(shape[1:]), lambda *_: (l,) + (0,) * (n - 1))


def _xattn_ln(x, mk, mv, wq, wo, g, b, *, l, seq, tq=1024, rows=256):
    T, D = x.shape
    tq = min(tq, seq)
    rows = min(rows, tq)
    nq = seq // tq
    mspec = pl.BlockSpec((None, None, XA_HEADS, N_MEM, XA_HEAD_DIM), lambda i: (l, i // nq, 0, 0, 0))
    return pl.pallas_call(
        functools.partial(_xattn_kernel, rows=rows),
        grid=(T // tq,),
        in_specs=[
            pl.BlockSpec((tq, D), lambda i: (i, 0)),
            mspec, mspec,
            _layer_spec(wq.shape, l), _layer_spec(wo.shape, l), _layer_spec(g.shape, l), _layer_spec(b.shape, l),
        ],
        out_specs=pl.BlockSpec((tq, D), lambda i: (i, 0)),
        out_shape=jax.ShapeDtypeStruct((T, D), f32),
        scratch_shapes=[pltpu.VMEM((tq, D), bf16)],
        compiler_params=_cparams("parallel"),
        name="xattn_ln",
    )(x, mk, mv, wq, wo, g, b)


def _xattn_dec_kernel(x_ref, k_ref, v_ref, wq_ref, wo_ref, g_ref, b_ref, o_ref, q_scr, a_scr, *, bs):
    i = pl.program_id(0)

    @pl.when(i == 0)
    def _():
        q_scr[...] = jnp.dot(x_ref[...].astype(bf16), wq_ref[...], preferred_element_type=f32)

    rows = pl.ds(pl.multiple_of(i * bs, bs), bs)
    q_blk = q_scr[rows, :]
    o_rows = []
    grp = SUBLANES // XA_HEADS
    for s in range(bs):
        q4 = jnp.concatenate([q_blk[s:s + 1, h * XA_HEAD_DIM:(h + 1) * XA_HEAD_DIM] for h in range(XA_HEADS)], axis=0)
        q8 = jnp.concatenate([q4] * grp, axis=0)
        both = lambda t: sum(t[j * XA_HEADS:(j + 1) * XA_HEADS] for j in range(grp))
        k3 = k_ref[s].reshape(N_MEM // grp, grp * XA_HEADS, XA_HEAD_DIM)
        v3 = v_ref[s].reshape(N_MEM // grp, grp * XA_HEADS, XA_HEAD_DIM)
        sc = jnp.sum(k3 * q8[None], -1, keepdims=True) * (XA_HEAD_DIM ** -0.5)
        mx = jnp.max(sc, 0)
        mx = functools.reduce(jnp.maximum, [mx[j * XA_HEADS:(j + 1) * XA_HEADS] for j in range(grp)])
        e = jnp.exp(sc - jnp.concatenate([mx] * grp, axis=0)[None])
        den = both(jnp.sum(e, 0))
        p = e / jnp.concatenate([den] * grp, axis=0)[None]
        o4 = both(jnp.sum(p * v3, 0))
        o_rows.append(jnp.concatenate([o4[h:h + 1, :] for h in range(XA_HEADS)], axis=-1))
    a_scr[rows, :] = jnp.concatenate(o_rows, axis=0)

    @pl.when(i == pl.num_programs(0) - 1)
    def _():
        att = jnp.dot(a_scr[...].astype(bf16), wo_ref[...], preferred_element_type=f32)
        o_ref[...] = _ln(DN_ALPHA * x_ref[...] + att, g_ref[XA_LN_ROW:XA_LN_ROW + 1, :], b_ref[XA_LN_ROW:XA_LN_ROW + 1, :])


def _xattn_dec_ln(x, ck, cv, wq, wo, g, b, *, l, bs=8):
    N, D = x.shape
    cspec = pl.BlockSpec((None, bs, N_MEM, XA_HEADS, XA_HEAD_DIM), lambda i: (l, i, 0, 0, 0))
    return pl.pallas_call(
        functools.partial(_xattn_dec_kernel, bs=bs),
        grid=(N // bs,),
        in_specs=[
            _full((N, D)), cspec, cspec,
            _layer_spec(wq.shape, l), _layer_spec(wo.shape, l), _layer_spec(g.shape, l), _layer_spec(b.shape, l),
        ],
        out_specs=_full((N, D)),
        out_shape=jax.ShapeDtypeStruct((N, D), f32),
        scratch_shapes=[pltpu.VMEM((N, D), f32), pltpu.VMEM((N, D), f32)],
        compiler_params=_cparams("arbitrary"),
        name="xattn_dec_ln",
    )(x, ck, cv, wq, wo, g, b)


def _gmlp_kernel(x_ref, win_ref, ws_ref, bs_ref, vg_ref, vb_ref, wout_ref, g_ref, b_ref, o_ref, *rest, single, rows):
    v_ref, uf_ref = rest if single else (None, rest[0])
    gw = GM_WIDTH // GM_GROUPS
    if not single:
        row = lax.broadcasted_iota(jnp.int32, (GM_CHUNK, GM_CHUNK), 0)
        col = lax.broadcasted_iota(jnp.int32, (GM_CHUNK, GM_CHUNK), 1)
        wmask = [jnp.where(col <= row, ws_ref[g], 0.0).astype(bf16) for g in range(GM_GROUPS)]
    for r0 in range(0, x_ref.shape[0], rows):
        x = x_ref[r0:r0 + rows, :]
        pr = jax.nn.gelu(jnp.dot(x.astype(bf16), win_ref[...], preferred_element_type=f32))
        u = pr[:, :GM_WIDTH]
        v = _ln(pr[:, GM_WIDTH:], vg_ref[...], vb_ref[...])
        if single:
            v_ref[r0:r0 + rows, :] = v
            for g in range(GM_GROUPS):
                sl = slice(g * gw, (g + 1) * gw)
                f = ws_ref[g][0:1, 0:1] * v[:, sl] + bs_ref[g][0:1, 0:1]
                uf_ref[r0:r0 + rows, sl] = (u[:, sl] * f).astype(bf16)
        else:
            vb16 = v.astype(bf16)
            for g in range(GM_GROUPS):
                sl = slice(g * gw, (g + 1) * gw)
                for c in range(rows // GM_CHUNK):
                    cs = slice(c * GM_CHUNK, (c + 1) * GM_CHUNK)
                    f = jnp.dot(wmask[g], vb16[cs, sl], preferred_element_type=f32) + bs_ref[g]
                    uf_ref[r0 + c * GM_CHUNK:r0 + (c + 1) * GM_CHUNK, sl] = (u[cs, sl] * f).astype(bf16)
        y = jnp.dot(uf_ref[r0:r0 + rows, :], wout_ref[...], preferred_element_type=f32)
        o_ref[r0:r0 + rows, :] = _ln(DN_ALPHA * x + y, g_ref[MIX_LN_ROW:MIX_LN_ROW + 1, :], b_ref[MIX_LN_ROW:MIX_LN_ROW + 1, :])


def _gmlp_ln(x, w_in, w_s, b_s, vg, vb, w_out, g, b, *, l, tm, single, rows=256):
    T, D = x.shape
    tm = min(tm, T)
    rows = min(rows, tm)
    xspec = pl.BlockSpec((tm, D), lambda i: (i, 0))
    out = jax.ShapeDtypeStruct((T, D), f32)
    res = pl.pallas_call(
        functools.partial(_gmlp_kernel, single=single, rows=rows),
        grid=(T // tm,),
        in_specs=[
            xspec, _full((D, 2 * GM_WIDTH)), _full(w_s.shape), _full(b_s.shape),
            _full((1, GM_WIDTH)), _full((1, GM_WIDTH)), _full((GM_WIDTH, D)), _layer_spec(g.shape, l), _layer_spec(b.shape, l),
        ],
        out_specs=[xspec, pl.BlockSpec((tm, GM_WIDTH), lambda i: (i, 0))] if single else xspec,
        out_shape=[out, jax.ShapeDtypeStruct((T, GM_WIDTH), f32)] if single else out,
        scratch_shapes=[pltpu.VMEM((tm, GM_WIDTH), bf16)],
        compiler_params=_cparams("parallel"),
        name="gmlp_ln",
    )(x, w_in, w_s, b_s, vg, vb, w_out, g, b)
    return res if single else (res, None)


def _gdn_gates(ba, alog, dtb):
    beta = jax.nn.sigmoid(ba[:, :LANES])
    g = -jnp.exp(alog) * _softplus(ba[:, LANES:] + dtb)
    return beta, g


def _l2n(x):
    return x * lax.rsqrt(jnp.sum(x * x, -1, keepdims=True) + 1e-6)


def _mixer0_kernel(x_ref, wmain_ref, wba_ref, dnw_ref, alog_ref, dtb_ref, ng_ref, ccw_ref, ccb_ref, cclg_ref, cclb_ref,
                   wout_ref, g_ref, b_ref,
                   o_ref, s_out_ref, dnc_out_ref, ccc_out_ref,
                   qkv_ext, glu_ext, rot_scr, s_scr, oc_scr):
    blk = pl.program_id(1)
    tb = x_ref.shape[0]
    C = GDN_BLOCK

    @pl.when(blk == 0)
    def _():
        qkv_ext[0:DN_TAIL, :] = jnp.zeros((DN_TAIL, 3 * GDN_WIDTH), f32)
        glu_ext[0:CC_TAIL, :] = jnp.zeros((CC_TAIL, CC_CH), f32)
        s_scr[...] = jnp.zeros_like(s_scr)

    x = x_ref[...]
    xb = x.astype(bf16)
    proj = jnp.dot(xb, wmain_ref[...], preferred_element_type=f32)
    ba = jnp.dot(xb, wba_ref[...], preferred_element_type=f32)
    nq = 3 * GDN_WIDTH

    qkv_ext[DN_TAIL:DN_TAIL + tb, :] = proj[:, :nq]
    blocks = []
    for i in range(tb // DN_ROWS):
        acc = None
        for s in range(SCONV_W):
            term = dnw_ref[SCONV_W - 1 - s:SCONV_W - s, :] * qkv_ext[pl.ds(DN_TAIL - s + i * DN_ROWS, DN_ROWS), :]
            acc = term if acc is None else acc + term
        blocks.append(_silu(acc))
    qkv = jnp.concatenate(blocks, axis=0)
    tail = qkv_ext[tb:tb + DN_TAIL, :]
    qkv_ext[0:DN_TAIL, :] = tail
    dnc_out_ref[0] = tail

    beta, g = _gdn_gates(ba, alog_ref[...], dtb_ref[...])

    P = 2 * GDN_DK
    pairs = range(GDN_HEADS // 2)
    chunks = range(tb // C)
    row = lax.broadcasted_iota(jnp.int32, (C, P), 0)
    col = lax.broadcasted_iota(jnp.int32, (C, P), 1)
    col = jnp.where(col >= C, col - C, col)
    causal = col <= row
    strict = col < row
    eye2 = jnp.where(col == row, 1.0, 0.0)
    ltri = jnp.where(causal[:, :C], 1.0, 0.0).astype(bf16)
    qn = [_l2n(qkv[:, h * GDN_DK:(h + 1) * GDN_DK]) * (GDN_DK ** -0.5) for h in range(GDN_HEADS)]
    kn = [_l2n(qkv[:, GDN_WIDTH + h * GDN_DK:GDN_WIDTH + (h + 1) * GDN_DK]) for h in range(GDN_HEADS)]

    def pair_cols(m, h0):
        return jnp.concatenate([jnp.broadcast_to(m[:, h0:h0 + 1], (C, GDN_DK)),
                                jnp.broadcast_to(m[:, h0 + 1:h0 + 2], (C, GDN_DK))], axis=1)

    def bdiag(m):
        z = jnp.zeros((C, C), m.dtype)
        return jnp.concatenate([jnp.concatenate([m[:, :C], z], axis=1), jnp.concatenate([z, m[:, C:]], axis=1)], axis=0)

    def split2(a):
        a1 = a.astype(bf16)
        return a1, (a - a1.astype(f32)).astype(bf16)

    mm = lambda a, b_: jnp.dot(a, b_, preferred_element_type=f32)

    prob = {}
    for c in chunks:
        rs = slice(c * C, (c + 1) * C)
        gc = _dot_exact_lhs(ltri, g[rs])
        gct = gc.T
        eg = jnp.exp(gc)
        g_last = gc[C - 1:C, :]
        ekt = jnp.exp(g_last - gc)
        egl = jnp.exp(g_last)
        for pr in pairs:
            h0 = 2 * pr
            k_pair = jnp.concatenate([kn[h0][rs], kn[h0 + 1][rs]], axis=1)
            q_pair = jnp.concatenate([qn[h0][rs], qn[h0 + 1][rs]], axis=1)
            v_pair = qkv[rs, 2 * GDN_WIDTH + pr * P:2 * GDN_WIDTH + (pr + 1) * P]
            beta_pair = pair_cols(beta[rs], h0)
            grow = jnp.concatenate([gct[h0:h0 + 1, :], gct[h0 + 1:h0 + 2, :]], axis=1)
            decay = jnp.where(causal, jnp.exp(jnp.where(causal, pair_cols(gc, h0) - grow, 0.0)), 0.0)
            kb = k_pair * beta_pair
            prod = _dot_nt(jnp.concatenate([kb, q_pair], axis=0), bdiag(k_pair.astype(bf16)))
            eg_pair = pair_cols(eg, h0)
            kt = k_pair * pair_cols(ekt, h0)
            prob[c, pr] = dict(
                n=-jnp.where(strict, prod[:C] * decay, 0.0),
                qk=(prod[C:] * decay).astype(bf16),
                vb=(v_pair * beta_pair).astype(bf16),
                kbe=(kb * eg_pair).astype(bf16),
                qg=(q_pair * eg_pair).astype(bf16),
                ktt=jnp.concatenate([kt[:, :C].T, kt[:, C:].T], axis=1).astype(bf16),
                egl=jnp.concatenate([jnp.broadcast_to(egl[:, h0:h0 + 1], (1, GDN_DK)),
                                     jnp.broadcast_to(egl[:, h0 + 1:h0 + 2], (1, GDN_DK))], axis=1))

    pk = {key: eye2 + pb["n"] for key, pb in prob.items()}
    nk = {key: pb["n"] for key, pb in prob.items()}
    levels = C.bit_length() - 1
    for j in range(levels):
        first, last = j == 0, j == levels - 1
        for key in prob:
            n1, n2 = split2(nk[key])
            if first:
                l1, l2 = n1, n2
            else:
                p1, p2 = split2(pk[key])
                l1 = p1 if last else jnp.concatenate([n1, p1], axis=0)
                l2 = p2 if last else jnp.concatenate([n2, p2], axis=0)
            r = l1.shape[0]
            o = mm(jnp.concatenate([l1, l2], axis=0), bdiag(n1))
            res = o[:r] + (o[r:] + mm(l1, bdiag(n2)))
            if first:
                nk[key] = res
            elif last:
                pk[key] = pk[key] + res
            else:
                nk[key] = res[:C]
                pk[key] = pk[key] + res[C:]

    for key, pb in prob.items():
        t16 = pk[key].astype(bf16)
        pb["u"] = mm(t16, bdiag(pb["vb"]))
        pb["w"] = mm(t16, bdiag(pb["kbe"])).astype(bf16)

    for c in chunks:
        rs = slice(c * C, (c + 1) * C)
        for pr in pairs:
            pb = prob[c, pr]
            s_pair = s_scr[pr]
            o2 = mm(jnp.concatenate([pb["w"], pb["qg"]], axis=0), bdiag(s_pair.astype(bf16)))
            vbd = bdiag((pb["u"] - o2[:C]).astype(bf16))
            o = o2[C:] + mm(pb["qk"], vbd)
            s_scr[pr] = s_pair * pb["egl"] + mm(pb["ktt"], vbd)
            on = [o[:, j * GDN_DV:(j + 1) * GDN_DV] for j in range(2)]
            on = [t * lax.rsqrt(jnp.mean(t * t, -1, keepdims=True) + 1e-6) * ng_ref[...] for t in on]
            z_pair = proj[rs, nq + pr * P:nq + (pr + 1) * P]
            oc_scr[rs, pr * P:(pr + 1) * P] = (jnp.concatenate(on, axis=1) * _silu(z_pair)).astype(bf16)

    for pr in pairs:
        s_out_ref[0, 2 * pr] = s_scr[pr][:, :GDN_DV]
        s_out_ref[0, 2 * pr + 1] = s_scr[pr][:, GDN_DV:]

    ga = proj[:, nq + GDN_WIDTH:nq + GDN_WIDTH + CC_CH]
    gb = proj[:, nq + GDN_WIDTH + CC_CH:]
    glu = ga * jax.nn.sigmoid(gb)
    glu_ext[CC_TAIL:CC_TAIL + tb, :] = glu
    span = CC_TAIL - SUBLANES
    for r in range(1, SUBLANES):
        rot_scr[r - 1] = glu_ext[pl.ds(SUBLANES - r, tb + span), :]

    def cc_rows(i, carry):
        base = pl.multiple_of(i * CC_ROWS, CC_ROWS)
        acc = None
        for s in range(CC_W):
            a, r = divmod(s, SUBLANES)
            off = span - SUBLANES * a
            if r == 0:
                xs = glu_ext[pl.ds(base + SUBLANES + off, CC_ROWS), :]
            else:
                xs = rot_scr[r - 1, pl.ds(base + off, CC_ROWS), :]
            term = ccw_ref[CC_W - 1 - s:CC_W - s, :] * xs
            acc = term if acc is None else acc + term
        cc = _silu(_ln(acc + ccb_ref[...], cclg_ref[...], cclb_ref[...]))
        oc_scr[pl.ds(base, CC_ROWS), GDN_WIDTH:] = cc.astype(bf16)
        return carry

    lax.fori_loop(0, tb // CC_ROWS, cc_rows, 0)
    tail = glu_ext[tb:tb + CC_TAIL, :]
    glu_ext[0:CC_TAIL, :] = tail
    ccc_out_ref[0] = tail

    y = jnp.dot(oc_scr[...], wout_ref[...], preferred_element_type=f32)
    o_ref[...] = _ln(DN_ALPHA * x + y, g_ref[MIX_LN_ROW:MIX_LN_ROW + 1, :], b_ref[MIX_LN_ROW:MIX_LN_ROW + 1, :])


def _mixer0_ln(x, wmain, wba, dnw, alog, dtb, ng, ccw, ccb, cclg, cclb, wout, g, b, *, l, batch, seq, tb=512):
    T, D = x.shape
    tb = min(tb, seq)
    nb = seq // tb
    consts = [wmain, wba, dnw, alog, dtb, ng, ccw, ccb, cclg, cclb, wout]
    return pl.pallas_call(
        _mixer0_kernel,
        grid=(batch, nb),
        in_specs=[pl.BlockSpec((tb, D), lambda i, j: (i * nb + j, 0))] + [_full(c.shape) for c in consts]
        + [_layer_spec(g.shape, l), _layer_spec(b.shape, l)],
        out_specs=[
            pl.BlockSpec((tb, D), lambda i, j: (i * nb + j, 0)),
            pl.BlockSpec((1, GDN_HEADS, GDN_DK, GDN_DV), lambda i, j: (i, 0, 0, 0)),
            pl.BlockSpec((1, DN_TAIL, 3 * GDN_WIDTH), lambda i, j: (i, 0, 0)),
            pl.BlockSpec((1, CC_TAIL, CC_CH), lambda i, j: (i, 0, 0)),
        ],
        out_shape=[
            jax.ShapeDtypeStruct((T, D), f32),
            jax.ShapeDtypeStruct((batch, GDN_HEADS, GDN_DK, GDN_DV), f32),
            jax.ShapeDtypeStruct((batch, DN_TAIL, 3 * GDN_WIDTH), f32),
            jax.ShapeDtypeStruct((batch, CC_TAIL, CC_CH), f32),
        ],
        scratch_shapes=[
            pltpu.VMEM((tb + DN_TAIL, 3 * GDN_WIDTH), f32),
            pltpu.VMEM((tb + CC_TAIL, CC_CH), f32),
            pltpu.VMEM((SUBLANES - 1, tb + CC_TAIL - SUBLANES, CC_CH), f32),
            pltpu.VMEM((GDN_HEADS // 2, GDN_DK, 2 * GDN_DV), f32),
            pltpu.VMEM((tb, GDN_WIDTH + CC_CH), bf16),
        ],
        compiler_params=_cparams("parallel", "arbitrary"),
        name="mixer0_ln",
    )(x, *consts, g, b)


def _mixer0_dec_kernel(x_ref, s_ref, dnc_ref, ccc_ref, wmain_ref, wba_ref, dnw_ref, alog_ref, dtb_ref, ng_ref, ccw_ref,
                       ccb_ref, cclg_ref, cclb_ref, wout_ref, g_ref, b_ref,
                       o_ref, s_out_ref, dnc_out_ref, ccc_out_ref,
                       q_scr, k_scr, v_scr, z_scr, beta_scr, eg_scr, oc_scr, *, bs):
    i = pl.program_id(0)
    nq = 3 * GDN_WIDTH

    @pl.when(i == 0)
    def _():
        xb = x_ref[...].astype(bf16)
        proj = jnp.dot(xb, wmain_ref[...], preferred_element_type=f32)
        ba = jnp.dot(xb, wba_ref[...], preferred_element_type=f32)
        qkv_raw = proj[:, :nq]
        acc = dnw_ref[SCONV_W - 1:SCONV_W, :] * qkv_raw
        for j in range(SCONV_W - 1):
            acc = acc + dnw_ref[j:j + 1, :] * dnc_ref[:, j, :]
        for j in range(SCONV_W - 2):
            dnc_out_ref[:, j, :] = dnc_ref[:, j + 1, :]
        dnc_out_ref[:, SCONV_W - 2, :] = qkv_raw
        qkv = _silu(acc)
        for h in range(GDN_HEADS):
            hs = slice(h * GDN_DK, (h + 1) * GDN_DK)
            q_scr[:, hs] = _l2n(qkv[:, h * GDN_DK:(h + 1) * GDN_DK]) * (GDN_DK ** -0.5)
            k_scr[:, hs] = _l2n(qkv[:, GDN_WIDTH + h * GDN_DK:GDN_WIDTH + (h + 1) * GDN_DK])
        v_scr[...] = qkv[:, 2 * GDN_WIDTH:]
        z_scr[...] = _silu(proj[:, nq:nq + GDN_WIDTH])
        beta, g = _gdn_gates(ba, alog_ref[...], dtb_ref[...])
        beta_scr[...] = beta
        eg_scr[...] = jnp.exp(g)

        ga = proj[:, nq + GDN_WIDTH:nq + GDN_WIDTH + CC_CH]
        gb = proj[:, nq + GDN_WIDTH + CC_CH:]
        glu = ga * jax.nn.sigmoid(gb)
        acc = ccw_ref[CC_W - 1:CC_W, :] * glu
        for j in range(CC_W - 1):
            acc = acc + ccw_ref[j:j + 1, :] * ccc_ref[:, j, :]
        for j in range(CC_W - 2):
            ccc_out_ref[:, j, :] = ccc_ref[:, j + 1, :]
        ccc_out_ref[:, CC_W - 2, :] = glu
        cc = _silu(_ln(acc + ccb_ref[...], cclg_ref[...], cclb_ref[...]))
        oc_scr[:, GDN_WIDTH:] = cc

    rows = pl.ds(pl.multiple_of(i * bs, bs), bs)
    q_blk, k_blk, v_blk, z_blk = q_scr[rows, :], k_scr[rows, :], v_scr[rows, :], z_scr[rows, :]
    beta_blk, eg_blk = beta_scr[rows, :], eg_scr[rows, :]
    o_rows = []
    for s in range(bs):
        o_heads = []
        for h in range(GDN_HEADS):
            hs = slice(h * GDN_DK, (h + 1) * GDN_DK)
            k_col = jnp.broadcast_to(k_blk[s:s + 1, hs], (GDN_DK, GDN_DK)).T
            q_col = jnp.broadcast_to(q_blk[s:s + 1, hs], (GDN_DK, GDN_DK)).T
            b1 = beta_blk[s:s + 1, h:h + 1]
            e1 = eg_blk[s:s + 1, h:h + 1]
            s_old = s_ref[s, h]
            ks = jnp.sum(k_col * s_old, 0, keepdims=True)
            v_new = b1 * (v_blk[s:s + 1, hs] - e1 * ks)
            s_new = s_old * e1 + k_col * v_new
            s_out_ref[s, h] = s_new
            o = jnp.sum(q_col * s_new, 0, keepdims=True)
            o = o * lax.rsqrt(jnp.mean(o * o, -1, keepdims=True) + 1e-6) * ng_ref[...]
            o_heads.append(o * z_blk[s:s + 1, hs])
        o_rows.append(jnp.concatenate(o_heads, axis=-1))
    oc_scr[rows, :GDN_WIDTH] = jnp.concatenate(o_rows, axis=0)

    @pl.when(i == pl.num_programs(0) - 1)
    def _():
        y = jnp.dot(oc_scr[...].astype(bf16), wout_ref[...], preferred_element_type=f32)
        o_ref[...] = _ln(DN_ALPHA * x_ref[...] + y, g_ref[MIX_LN_ROW:MIX_LN_ROW + 1, :], b_ref[MIX_LN_ROW:MIX_LN_ROW + 1, :])


def _mixer0_dec_ln(x, s, dnc, ccc, wmain, wba, dnw, alog, dtb, ng, ccw, ccb, cclg, cclb, wout, g, b, *, l, bs=8):
    N, D = x.shape
    consts = [wmain, wba, dnw, alog, dtb, ng, ccw, ccb, cclg, cclb, wout]
    sspec = pl.BlockSpec((bs, GDN_HEADS, GDN_DK, GDN_DV), lambda i: (i, 0, 0, 0))
    return pl.pallas_call(
        functools.partial(_mixer0_dec_kernel, bs=bs),
        grid=(N // bs,),
        in_specs=[_full((N, D)), sspec, _full(dnc.shape), _full(ccc.shape)] + [_full(c.shape) for c in consts]
        + [_layer_spec(g.shape, l), _layer_spec(b.shape, l)],
        out_specs=[_full((N, D)), sspec, _full(dnc.shape), _full(ccc.shape)],
        out_shape=[jax.ShapeDtypeStruct((N, D), f32), jax.ShapeDtypeStruct(s.shape, f32),
                   jax.ShapeDtypeStruct(dnc.shape, f32), jax.ShapeDtypeStruct(ccc.shape, f32)],
        scratch_shapes=[pltpu.VMEM((N, GDN_WIDTH), f32)] * 4 + [pltpu.VMEM((N, LANES), f32)] * 2
        + [pltpu.VMEM((N, GDN_WIDTH + CC_CH), f32)],
        compiler_params=_cparams("arbitrary"),
        name="mixer0_dec_ln",
    )(x, s, dnc, ccc, *consts, g, b)


def _pad_lanes(v, n=LANES):
    return jnp.zeros((1, n), f32).at[0, :v.shape[0]].set(v.astype(f32))


def kernel(x_prompt, x_sample, mem_prompt, cache_mem_k, cache_mem_v, state_dn_S, state_dn_conv, state_cc_conv, ln_g, ln_b, ffn_w_gate, ffn_w_up, ffn_w_down, xa_wq, xa_wk, xa_wv, xa_wo, ab_w_in, dn_conv_w, dn_A_log, dn_dt_bias, dn_norm_g, cc_conv_w, cc_conv_b, cc_ln_g, cc_ln_b, ab_w_out, gm_w_in, gm_ln_g, gm_ln_b, gm_w_s, gm_b_s, gm_w_out):
    B, SEQ, D = x_prompt.shape
    N = x_sample.shape[0]
    row = lambda v: v.reshape(1, -1).astype(f32)

    wg, wu, wd = ffn_w_gate.astype(bf16), ffn_w_up.astype(bf16), ffn_w_down.astype(bf16)
    wq, wk, wv, wo = xa_wq.astype(bf16), xa_wk.astype(bf16), xa_wv.astype(bf16), xa_wo.astype(bf16)
    nq = 3 * GDN_WIDTH
    nz = nq + GDN_WIDTH
    w_main = jnp.concatenate([ab_w_in[:, :nz], ab_w_in[:, nz + 2 * GDN_HEADS:]], axis=1).astype(bf16)
    w_ba = jnp.zeros((D, 2 * LANES), f32)
    w_ba = w_ba.at[:, :GDN_HEADS].set(ab_w_in[:, nz:nz + GDN_HEADS])
    w_ba = w_ba.at[:, LANES:LANES + GDN_HEADS].set(ab_w_in[:, nz + GDN_HEADS:nz + 2 * GDN_HEADS]).astype(bf16)
    w_out = ab_w_out.astype(bf16)
    gw_in, gw_out = gm_w_in.astype(bf16), gm_w_out.astype(bf16)
    mixer_consts = (w_main, w_ba, dn_conv_w.astype(f32), _pad_lanes(dn_A_log), _pad_lanes(dn_dt_bias), row(dn_norm_g),
                    cc_conv_w.astype(f32), row(cc_conv_b), row(cc_ln_g), row(cc_ln_b), w_out)
    gm_consts = (gw_in, gm_w_s.astype(f32), gm_b_s.astype(f32)[:, :, None], row(gm_ln_g), row(gm_ln_b), gw_out)
    lng, lnb = ln_g.astype(f32), ln_b.astype(f32)

    mem_k, mem_v, mem_kh, mem_vh = _mem_kv(mem_prompt, wk, wv)

    def trunk(x, mk, mv, *, tm, prompt, state=None):
        gm_v = None
        for l in range(DEPTH):
            x = _ffn_ln(x, wg, wu, wd, lng, lnb, l=l, half=0, tm=tm)
            if l % 2 == 0:
                if prompt:
                    x, dn_s, dn_c, cc_c = _mixer0_ln(x, *mixer_consts, lng, lnb, l=l, batch=B, seq=SEQ)
                else:
                    x, dn_s, dn_c, cc_c = _mixer0_dec_ln(x, *state, *mixer_consts, lng, lnb, l=l)
            else:
                x, gm_v = _gmlp_ln(x, *gm_consts, lng, lnb, l=l, tm=tm, single=not prompt)
            if prompt:
                x = _xattn_ln(x, mk, mv, wq, wo, lng, lnb, l=l, seq=SEQ)
            else:
                x = _xattn_dec_ln(x, mk, mv, wq, wo, lng, lnb, l=l)
            x = _ffn_ln(x, wg, wu, wd, lng, lnb, l=l, half=1, tm=tm)
        return x, dn_s, dn_c, cc_c, gm_v

    yp, dn_s_p, dn_c_p, cc_c_p, _ = trunk(x_prompt.reshape(B * SEQ, D), mem_kh, mem_vh, tm=1024, prompt=True)
    ys, dn_s_s, dn_c_s, cc_c_s, gm_v_s = trunk(
        x_sample.reshape(N, D), cache_mem_k, cache_mem_v, tm=N, prompt=False,
        state=(state_dn_S, state_dn_conv, state_cc_conv))

    return (yp.reshape(B, SEQ, D), ys.reshape(N, 1, D), mem_k, mem_v,
            dn_s_p, dn_c_p[:, DN_TAIL - (SCONV_W - 1):], cc_c_p[:, CC_TAIL - (CC_W - 1):],
            dn_s_s, dn_c_s, cc_c_s, gm_v_s.reshape(N, 1, GM_WIDTH))
```

```python
import functools

import jax
import jax.numpy as jnp
from jax import lax
from jax.experimental import pallas as pl
from jax.experimental.pallas import tpu as pltpu

D_MODEL = 1024
DEPTH = 2
DN_ALPHA = (2 * DEPTH) ** 0.25
LN_EPS = 1e-5
GDN_HEADS = 4
GDN_DK = 128
GDN_DV = 128
GDN_WIDTH = GDN_HEADS * GDN_DK
SCONV_W = 4
CC_CH = D_MODEL // 2
CC_W = 31
GM_WIDTH = D_MODEL
GM_GROUPS = 4
GM_CHUNK = 128
N_MEM = 256
XA_HEADS = 4
XA_HEAD_DIM = D_MODEL // XA_HEADS
D_FF = 2816
MIX_LN_ROW = 1
XA_LN_ROW = 2

LANES = 128
SUBLANES = 8
GDN_BLOCK = 128
DN_TAIL = SUBLANES
CC_TAIL = 32
CC_ROWS = 32
DN_ROWS = 16
VMEM_LIMIT = 56 * 1024 * 1024

bf16 = jnp.bfloat16
f32 = jnp.float32


def _cparams(*sem):
    return pltpu.CompilerParams(dimension_semantics=sem, vmem_limit_bytes=VMEM_LIMIT)


def _full(shape):
    n = len(shape)
    return pl.BlockSpec(shape, lambda *_: (0,) * n)


def _dot_nt(a, b):
    return lax.dot_general(a.astype(bf16), b.astype(bf16), (((1,), (1,)), ((), ())), preferred_element_type=f32)


def _split3(a):
    a1 = a.astype(bf16)
    r = a - a1.astype(f32)
    a2 = r.astype(bf16)
    a3 = (r - a2.astype(f32)).astype(bf16)
    return a1, a2, a3


def _dot_exact_lhs(a_exact_bf16, b):
    b1, b2, b3 = _split3(b)
    d = lambda y: jnp.dot(a_exact_bf16, y, preferred_element_type=f32)
    return d(b1) + (d(b2) + d(b3))


def _ln(y, g, b):
    mu = jnp.mean(y, -1, keepdims=True)
    d = y - mu
    var = jnp.mean(d * d, -1, keepdims=True)
    return d * lax.rsqrt(var + LN_EPS) * g + b


def _silu(x):
    return x * jax.nn.sigmoid(x)


def _softplus(x):
    return jnp.maximum(x, 0.0) + jnp.log(1.0 + jnp.exp(-jnp.abs(x)))


def _ffn_kernel(x_ref, wg_ref, wu_ref, wd_ref, g_ref, b_ref, o_ref, *, ln_row, rows, tf):
    r = slice(ln_row, ln_row + 1)
    tm = x_ref.shape[0]
    nf = wg_ref.shape[1] // tf
    for r0 in range(0, tm, rows):
        x = x_ref[r0:r0 + rows, :]
        xb = x.astype(bf16)
        acc = None
        for j in range(nf):
            cs = slice(j * tf, (j + 1) * tf)
            hg = jnp.dot(xb, wg_ref[:, cs], preferred_element_type=f32)
            hu = jnp.dot(xb, wu_ref[:, cs], preferred_element_type=f32)
            d = jnp.dot((_silu(hg) * hu).astype(bf16), wd_ref[cs, :], preferred_element_type=f32)
            acc = d if acc is None else acc + d
        o_ref[r0:r0 + rows, :] = _ln(DN_ALPHA * x + 0.5 * acc, g_ref[r, :], b_ref[r, :])


def _ffn_ln(x, wg, wu, wd, g, b, *, l, half, tm, rows=512, tf=256):
    T, D = x.shape
    tm = min(tm, T)
    rows = min(rows, tm)
    F = wg.shape[-1]
    nln = g.shape[1]
    once = pl.Buffered(1)
    return pl.pallas_call(
        functools.partial(_ffn_kernel, ln_row=(nln - 1) * half, rows=rows, tf=tf),
        grid=(T // tm,),
        in_specs=[
            pl.BlockSpec((tm, D), lambda i: (i, 0)),
            pl.BlockSpec((None, None, D, F), lambda i: (l, half, 0, 0), pipeline_mode=once),
            pl.BlockSpec((None, None, D, F), lambda i: (l, half, 0, 0), pipeline_mode=once),
            pl.BlockSpec((None, None, F, D), lambda i: (l, half, 0, 0), pipeline_mode=once),
            pl.BlockSpec((None, nln, D), lambda i: (l, 0, 0)),
            pl.BlockSpec((None, nln, D), lambda i: (l, 0, 0)),
        ],
        out_specs=pl.BlockSpec((tm, D), lambda i: (i, 0)),
        out_shape=jax.ShapeDtypeStruct((T, D), f32),
        compiler_params=_cparams("parallel"),
        name="ffn_ln",
    )(x, wg, wu, wd, g, b)


def _memkv_kernel(m_ref, wk_ref, wv_ref, k_ref, v_ref, kh_ref, vh_ref):
    nb = m_ref.shape[0]
    for b in range(nb):
        mb = m_ref[b].astype(bf16)
        k = jnp.dot(mb, wk_ref[...], preferred_element_type=f32)
        v = jnp.dot(mb, wv_ref[...], preferred_element_type=f32)
        for h in range(XA_HEADS):
            sl = slice(h * XA_HEAD_DIM, (h + 1) * XA_HEAD_DIM)
            k_ref[b, :, h, :] = k[:, sl]
            v_ref[b, :, h, :] = v[:, sl]
            kh_ref[b, h] = k[:, sl].astype(bf16)
            vh_ref[b, h] = v[:, sl].astype(bf16)


def _mem_kv(mem, wk, wv, *, nb=2):
    B, M, D = mem.shape
    nb = min(nb, B)
    L = wk.shape[0]
    out = jax.ShapeDtypeStruct((L, B, M, XA_HEADS, XA_HEAD_DIM), f32)
    outh = jax.ShapeDtypeStruct((L, B, XA_HEADS, M, XA_HEAD_DIM), bf16)
    ospec = pl.BlockSpec((None, nb, M, XA_HEADS, XA_HEAD_DIM), lambda l, i: (l, i, 0, 0, 0))
    hspec = pl.BlockSpec((None, nb, XA_HEADS, M, XA_HEAD_DIM), lambda l, i: (l, i, 0, 0, 0))
    return pl.pallas_call(
        _memkv_kernel,
        grid=(L, B // nb),
        in_specs=[
            pl.BlockSpec((nb, M, D), lambda l, i: (i, 0, 0)),
            pl.BlockSpec((None, D, D), lambda l, i: (l, 0, 0)),
            pl.BlockSpec((None, D, D), lambda l, i: (l, 0, 0)),
        ],
        out_specs=[ospec, ospec, hspec, hspec],
        out_shape=[out, out, outh, outh],
        compiler_params=_cparams("parallel", "parallel"),
        name="mem_kv",
    )(mem, wk, wv)


def _xattn_kernel(x_ref, k_ref, v_ref, wq_ref, wo_ref, g_ref, b_ref, o_ref, oh_ref, *, rows):
    for r0 in range(0, x_ref.shape[0], rows):
        rs = slice(r0, r0 + rows)
        x = x_ref[rs, :]
        q = jnp.dot(x.astype(bf16), wq_ref[...], preferred_element_type=f32).astype(bf16)
        for h in range(XA_HEADS):
            sl = slice(h * XA_HEAD_DIM, (h + 1) * XA_HEAD_DIM)
            s = _dot_nt(q[:, sl], k_ref[h]) * (XA_HEAD_DIM ** -0.5)
            s = s - jnp.max(s, -1, keepdims=True)
            e = jnp.exp(s)
            p = e / jnp.sum(e, -1, keepdims=True)
            oh_ref[rs, sl] = jnp.dot(p.astype(bf16), v_ref[h], preferred_element_type=f32).astype(bf16)
        att = jnp.dot(oh_ref[rs, :], wo_ref[...], preferred_element_type=f32)
        o_ref[rs, :] = _ln(DN_ALPHA * x + att, g_ref[XA_LN_ROW:XA_LN_ROW + 1, :], b_ref[XA_LN_ROW:XA_LN_ROW + 1, :])


def _layer_spec(shape, l):
    n = len(shape)
    return pl.BlockSpec((None,) + tuple(shape[1:]), lambda *_: (l,) + (0,) * (n - 1))


def _xattn_ln(x, mk, mv, wq, wo, g, b, *, l, seq, tq=1024, rows=256):
    T, D = x.shape
    tq = min(tq, seq)
    rows = min(rows, tq)
    nq = seq // tq
    mspec = pl.BlockSpec((None, None, XA_HEADS, N_MEM, XA_HEAD_DIM), lambda i: (l, i // nq, 0, 0, 0))
    return pl.pallas_call(
        functools.partial(_xattn_kernel, rows=rows),
        grid=(T // tq,),
        in_specs=[
            pl.BlockSpec((tq, D), lambda i: (i, 0)),
            mspec, mspec,
            _layer_spec(wq.shape, l), _layer_spec(wo.shape, l), _layer_spec(g.shape, l), _layer_spec(b.shape, l),
        ],
        out_specs=pl.BlockSpec((tq, D), lambda i: (i, 0)),
        out_shape=jax.ShapeDtypeStruct((T, D), f32),
        scratch_shapes=[pltpu.VMEM((tq, D), bf16)],
        compiler_params=_cparams("parallel"),
        name="xattn_ln",
    )(x, mk, mv, wq, wo, g, b)


def _xattn_dec_kernel(x_ref, k_ref, v_ref, wq_ref, wo_ref, g_ref, b_ref, o_ref, q_scr, a_scr, *, bs):
    i = pl.program_id(0)

    @pl.when(i == 0)
    def _():
        q_scr[...] = jnp.dot(x_ref[...].astype(bf16), wq_ref[...], preferred_element_type=f32)

    rows = pl.ds(pl.multiple_of(i * bs, bs), bs)
    q_blk = q_scr[rows, :]
    o_rows = []
    grp = SUBLANES // XA_HEADS
    for s in range(bs):
        q4 = jnp.concatenate([q_blk[s:s + 1, h * XA_HEAD_DIM:(h + 1) * XA_HEAD_DIM] for h in range(XA_HEADS)], axis=0)
        q8 = jnp.concatenate([q4] * grp, axis=0)
        both = lambda t: sum(t[j * XA_HEADS:(j + 1) * XA_HEADS] for j in range(grp))
        k3 = k_ref[s].reshape(N_MEM // grp, grp * XA_HEADS, XA_HEAD_DIM)
        v3 = v_ref[s].reshape(N_MEM // grp, grp * XA_HEADS, XA_HEAD_DIM)
        sc = jnp.sum(k3 * q8[None], -1, keepdims=True) * (XA_HEAD_DIM ** -0.5)
        mx = jnp.max(sc, 0)
        mx = functools.reduce(jnp.maximum, [mx[j * XA_HEADS:(j + 1) * XA_HEADS] for j in range(grp)])
        e = jnp.exp(sc - jnp.concatenate([mx] * grp, axis=0)[None])
        den = both(jnp.sum(e, 0))
        p = e / jnp.concatenate([den] * grp, axis=0)[None]
        o4 = both(jnp.sum(p * v3, 0))
        o_rows.append(jnp.concatenate([o4[h:h + 1, :] for h in range(XA_HEADS)], axis=-1))
    a_scr[rows, :] = jnp.concatenate(o_rows, axis=0)

    @pl.when(i == pl.num_programs(0) - 1)
    def _():
        att = jnp.dot(a_scr[...].astype(bf16), wo_ref[...], preferred_element_type=f32)
        o_ref[...] = _ln(DN_ALPHA * x_ref[...] + att, g_ref[XA_LN_ROW:XA_LN_ROW + 1, :], b_ref[XA_LN_ROW:XA_LN_ROW + 1, :])


def _xattn_dec_ln(x, ck, cv, wq, wo, g, b, *, l, bs=8):
    N, D = x.shape
    cspec = pl.BlockSpec((None, bs, N_MEM, XA_HEADS, XA_HEAD_DIM), lambda i: (l, i, 0, 0, 0))
    return pl.pallas_call(
        functools.partial(_xattn_dec_kernel, bs=bs),
        grid=(N // bs,),
        in_specs=[
            _full((N, D)), cspec, cspec,
            _layer_spec(wq.shape, l), _layer_spec(wo.shape, l), _layer_spec(g.shape, l), _layer_spec(b.shape, l),
        ],
        out_specs=_full((N, D)),
        out_shape=jax.ShapeDtypeStruct((N, D), f32),
        scratch_shapes=[pltpu.VMEM((N, D), f32), pltpu.VMEM((N, D), f32)],
        compiler_params=_cparams("arbitrary"),
        name="xattn_dec_ln",
    )(x, ck, cv, wq, wo, g, b)


def _gmlp_kernel(x_ref, win_ref, ws_ref, bs_ref, vg_ref, vb_ref, wout_ref, g_ref, b_ref, o_ref, *rest, single, rows):
    v_ref, uf_ref = rest if single else (None, rest[0])
    gw = GM_WIDTH // GM_GROUPS
    if not single:
        row = lax.broadcasted_iota(jnp.int32, (GM_CHUNK, GM_CHUNK), 0)
        col = lax.broadcasted_iota(jnp.int32, (GM_CHUNK, GM_CHUNK), 1)
        wmask = [jnp.where(col <= row, ws_ref[g], 0.0).astype(bf16) for g in range(GM_GROUPS)]
    for r0 in range(0, x_ref.shape[0], rows):
        x = x_ref[r0:r0 + rows, :]
        pr = jax.nn.gelu(jnp.dot(x.astype(bf16), win_ref[...], preferred_element_type=f32))
        u = pr[:, :GM_WIDTH]
        v = _ln(pr[:, GM_WIDTH:], vg_ref[...], vb_ref[...])
        if single:
            v_ref[r0:r0 + rows, :] = v
            for g in range(GM_GROUPS):
                sl = slice(g * gw, (g + 1) * gw)
                f = ws_ref[g][0:1, 0:1] * v[:, sl] + bs_ref[g][0:1, 0:1]
                uf_ref[r0:r0 + rows, sl] = (u[:, sl] * f).astype(bf16)
        else:
            vb16 = v.astype(bf16)
            for g in range(GM_GROUPS):
                sl = slice(g * gw, (g + 1) * gw)
                for c in range(rows // GM_CHUNK):
                    cs = slice(c * GM_CHUNK, (c + 1) * GM_CHUNK)
                    f = jnp.dot(wmask[g], vb16[cs, sl], preferred_element_type=f32) + bs_ref[g]
                    uf_ref[r0 + c * GM_CHUNK:r0 + (c + 1) * GM_CHUNK, sl] = (u[cs, sl] * f).astype(bf16)
        y = jnp.dot(uf_ref[r0:r0 + rows, :], wout_ref[...], preferred_element_type=f32)
        o_ref[r0:r0 + rows, :] = _ln(DN_ALPHA * x + y, g_ref[MIX_LN_ROW:MIX_LN_ROW + 1, :], b_ref[MIX_LN_ROW:MIX_LN_ROW + 1, :])


def _gmlp_ln(x, w_in, w_s, b_s, vg, vb, w_out, g, b, *, l, tm, single, rows=256):
    T, D = x.shape
    tm = min(tm, T)
    rows = min(rows, tm)
    xspec = pl.BlockSpec((tm, D), lambda i: (i, 0))
    out = jax.ShapeDtypeStruct((T, D), f32)
    res = pl.pallas_call(
        functools.partial(_gmlp_kernel, single=single, rows=rows),
        grid=(T // tm,),
        in_specs=[
            xspec, _full((D, 2 * GM_WIDTH)), _full(w_s.shape), _full(b_s.shape),
            _full((1, GM_WIDTH)), _full((1, GM_WIDTH)), _full((GM_WIDTH, D)), _layer_spec(g.shape, l), _layer_spec(b.shape, l),
        ],
        out_specs=[xspec, pl.BlockSpec((tm, GM_WIDTH), lambda i: (i, 0))] if single else xspec,
        out_shape=[out, jax.ShapeDtypeStruct((T, GM_WIDTH), f32)] if single else out,
        scratch_shapes=[pltpu.VMEM((tm, GM_WIDTH), bf16)],
        compiler_params=_cparams("parallel"),
        name="gmlp_ln",
    )(x, w_in, w_s, b_s, vg, vb, w_out, g, b)
    return res if single else (res, None)


def _gdn_gates(ba, alog, dtb):
    beta = jax.nn.sigmoid(ba[:, :LANES])
    g = -jnp.exp(alog) * _softplus(ba[:, LANES:] + dtb)
    return beta, g


def _l2n(x):
    return x * lax.rsqrt(jnp.sum(x * x, -1, keepdims=True) + 1e-6)


def _mixer0_kernel(x_ref, wmain_ref, wba_ref, dnw_ref, alog_ref, dtb_ref, ng_ref, ccw_ref, ccb_ref, cclg_ref, cclb_ref,
                   wout_ref, g_ref, b_ref,
                   o_ref, s_out_ref, dnc_out_ref, ccc_out_ref,
                   qkv_ext, glu_ext, rot_scr, s_scr, oc_scr):
    blk = pl.program_id(1)
    tb = x_ref.shape[0]
    C = GDN_BLOCK

    @pl.when(blk == 0)
    def _():
        qkv_ext[0:DN_TAIL, :] = jnp.zeros((DN_TAIL, 3 * GDN_WIDTH), f32)
        glu_ext[0:CC_TAIL, :] = jnp.zeros((CC_TAIL, CC_CH), f32)
        s_scr[...] = jnp.zeros_like(s_scr)

    x = x_ref[...]
    xb = x.astype(bf16)
    proj = jnp.dot(xb, wmain_ref[...], preferred_element_type=f32)
    ba = jnp.dot(xb, wba_ref[...], preferred_element_type=f32)
    nq = 3 * GDN_WIDTH

    qkv_ext[DN_TAIL:DN_TAIL + tb, :] = proj[:, :nq]
    blocks = []
    for i in range(tb // DN_ROWS):
        acc = None
        for s in range(SCONV_W):
            term = dnw_ref[SCONV_W - 1 - s:SCONV_W - s, :] * qkv_ext[pl.ds(DN_TAIL - s + i * DN_ROWS, DN_ROWS), :]
            acc = term if acc is None else acc + term
        blocks.append(_silu(acc))
    qkv = jnp.concatenate(blocks, axis=0)
    tail = qkv_ext[tb:tb + DN_TAIL, :]
    qkv_ext[0:DN_TAIL, :] = tail
    dnc_out_ref[0] = tail

    beta, g = _gdn_gates(ba, alog_ref[...], dtb_ref[...])

    P = 2 * GDN_DK
    pairs = range(GDN_HEADS // 2)
    chunks = range(tb // C)
    row = lax.broadcasted_iota(jnp.int32, (C, P), 0)
    col = lax.broadcasted_iota(jnp.int32, (C, P), 1)
    col = jnp.where(col >= C, col - C, col)
    causal = col <= row
    strict = col < row
    ltri = jnp.where(causal[:, :C], 1.0, 0.0).astype(bf16)
    qn = [_l2n(qkv[:, h * GDN_DK:(h + 1) * GDN_DK]) * (GDN_DK ** -0.5) for h in range(GDN_HEADS)]
    kn = [_l2n(qkv[:, GDN_WIDTH + h * GDN_DK:GDN_WIDTH + (h + 1) * GDN_DK]) for h in range(GDN_HEADS)]

    def pair_cols(m, h0):
        return jnp.concatenate([jnp.broadcast_to(m[:, h0:h0 + 1], (C, GDN_DK)),
                                jnp.broadcast_to(m[:, h0 + 1:h0 + 2], (C, GDN_DK))], axis=1)

    def bdiag(m):
        z = jnp.zeros((C, C), m.dtype)
        return jnp.concatenate([jnp.concatenate([m[:, :C], z], axis=1), jnp.concatenate([z, m[:, C:]], axis=1)], axis=0)

    def split2(a):
        a1 = a.astype(bf16)
        return a1, (a - a1.astype(f32)).astype(bf16)

    mm = lambda a, b_: jnp.dot(a, b_, preferred_element_type=f32)

    prob = {}
    for c in chunks:
        rs = slice(c * C, (c + 1) * C)
        gc = _dot_exact_lhs(ltri, g[rs])
        gct = gc.T
        eg = jnp.exp(gc)
        g_last = gc[C - 1:C, :]
        ekt = jnp.exp(g_last - gc)
        egl = jnp.exp(g_last)
        for pr in pairs:
            h0 = 2 * pr
            k_pair = jnp.concatenate([kn[h0][rs], kn[h0 + 1][rs]], axis=1)
            q_pair = jnp.concatenate([qn[h0][rs], qn[h0 + 1][rs]], axis=1)
            v_pair = qkv[rs, 2 * GDN_WIDTH + pr * P:2 * GDN_WIDTH + (pr + 1) * P]
            beta_pair = pair_cols(beta[rs], h0)
            grow = jnp.concatenate([gct[h0:h0 + 1, :], gct[h0 + 1:h0 + 2, :]], axis=1)
            decay = jnp.where(causal, jnp.exp(jnp.where(causal, pair_cols(gc, h0) - grow, 0.0)), 0.0)
            kb = k_pair * beta_pair
            prod = _dot_nt(jnp.concatenate([kb, q_pair], axis=0), bdiag(k_pair.astype(bf16)))
            eg_pair = pair_cols(eg, h0)
            kt = k_pair * pair_cols(ekt, h0)
            prob[c, pr] = dict(
                n=-jnp.where(strict, prod[:C] * decay, 0.0),
                qk=(prod[C:] * decay).astype(bf16),
                vb=(v_pair * beta_pair).astype(bf16),
                kbe=(kb * eg_pair).astype(bf16),
                qg=(q_pair * eg_pair).astype(bf16),
                ktt=jnp.concatenate([kt[:, :C].T, kt[:, C:].T], axis=1).astype(bf16),
                egl=jnp.concatenate([jnp.broadcast_to(egl[:, h0:h0 + 1], (1, GDN_DK)),
                                     jnp.broadcast_to(egl[:, h0 + 1:h0 + 2], (1, GDN_DK))], axis=1))

    Hc = C // 2
    lane = lax.broadcasted_iota(jnp.int32, (Hc, P), 1)
    prow = lax.broadcasted_iota(jnp.int32, (Hc, P), 0)
    first_half = jnp.where(lane >= C, lane - C, lane) < Hc
    eye4 = jnp.where((lane & (Hc - 1)) == prow, 1.0, 0.0)
    quarter = [(lane >= q * Hc) & (lane < (q + 1) * Hc) for q in range(P // Hc)]

    def bdiag4(m):
        return jnp.concatenate([jnp.where(qm, m, jnp.zeros_like(m)) for qm in quarter], axis=0)

    def dot3(l1, l2, w1, w2):
        r = l1.shape[0]
        o = mm(jnp.concatenate([l1, l2], axis=0), w1)
        return o[:r] + (o[r:] + mm(l1, w2))

    nk = {key: jnp.where(first_half, pb["n"][:Hc], pb["n"][Hc:]) for key, pb in prob.items()}
    pk = {key: eye4 + nk[key] for key in prob}
    levels = Hc.bit_length() - 1
    for j in range(levels):
        first, last = j == 0, j == levels - 1
        for key in prob:
            n1, n2 = split2(nk[key])
            if first:
                l1, l2 = n1, n2
            else:
                p1, p2 = split2(pk[key])
                l1 = p1 if last else jnp.concatenate([n1, p1], axis=0)
                l2 = p2 if last else jnp.concatenate([n2, p2], axis=0)
            res = dot3(l1, l2, bdiag4(n1), bdiag4(n2))
            if first:
                nk[key] = res
            elif last:
                pk[key] = pk[key] + res
            else:
                nk[key] = res[:Hc]
                pk[key] = pk[key] + res[Hc:]
    crow = lax.broadcasted_iota(jnp.int32, (C, P), 0)
    c_block = (crow >= Hc) & (col < Hc)
    for key, pb in prob.items():
        t_pan = pk[key]
        t1, t2 = split2(t_pan)
        c1, c2 = split2(jnp.where(c_block, pb["n"], 0.0))
        x1, x2 = split2(dot3(t1, t2, bdiag(c1), bdiag(c2)))
        zero = jnp.zeros_like(t1)
        low = dot3(x1, x2, bdiag4(jnp.where(first_half, t1, zero)), bdiag4(jnp.where(first_half, t2, zero)))
        pk[key] = jnp.concatenate([jnp.where(first_half, t_pan, 0.0), low + jnp.where(first_half, 0.0, t_pan)], axis=0)

    for key, pb in prob.items():
        t16 = pk[key].astype(bf16)
        pb["u"] = mm(t16, bdiag(pb["vb"]))
        pb["w"] = mm(t16, bdiag(pb["kbe"])).astype(bf16)

    for c in chunks:
        rs = slice(c * C, (c + 1) * C)
        for pr in pairs:
            pb = prob[c, pr]
            s_pair = s_scr[pr]
            o2 = mm(jnp.concatenate([pb["w"], pb["qg"]], axis=0), bdiag(s_pair.astype(bf16)))
            vbd = bdiag((pb["u"] - o2[:C]).astype(bf16))
            o = o2[C:] + mm(pb["qk"], vbd)
            s_scr[pr] = s_pair * pb["egl"] + mm(pb["ktt"], vbd)
            on = [o[:, j * GDN_DV:(j + 1) * GDN_DV] for j in range(2)]
            on = [t * lax.rsqrt(jnp.mean(t * t, -1, keepdims=True) + 1e-6) * ng_ref[...] for t in on]
            z_pair = proj[rs, nq + pr * P:nq + (pr + 1) * P]
            oc_scr[rs, pr * P:(pr + 1) * P] = (jnp.concatenate(on, axis=1) * _silu(z_pair)).astype(bf16)

    for pr in pairs:
        s_out_ref[0, 2 * pr] = s_scr[pr][:, :GDN_DV]
        s_out_ref[0, 2 * pr + 1] = s_scr[pr][:, GDN_DV:]

    ga = proj[:, nq + GDN_WIDTH:nq + GDN_WIDTH + CC_CH]
    gb = proj[:, nq + GDN_WIDTH + CC_CH:]
    glu = ga * jax.nn.sigmoid(gb)
    glu_ext[CC_TAIL:CC_TAIL + tb, :] = glu
    span = CC_TAIL - SUBLANES
    for r in range(1, SUBLANES):
        rot_scr[r - 1] = glu_ext[pl.ds(SUBLANES - r, tb + span), :]

    for base in range(0, tb, CC_ROWS):
        acc = None
        for s in range(CC_W):
            a, r = divmod(s, SUBLANES)
            off = span - SUBLANES * a
            if r == 0:
                xs = glu_ext[base + SUBLANES + off:base + SUBLANES + off + CC_ROWS, :]
            else:
                xs = rot_scr[r - 1, base + off:base + off + CC_ROWS, :]
            term = ccw_ref[CC_W - 1 - s:CC_W - s, :] * xs
            acc = term if acc is None else acc + term
        cc = _silu(_ln(acc + ccb_ref[...], cclg_ref[...], cclb_ref[...]))
        oc_scr[base:base + CC_ROWS, GDN_WIDTH:] = cc.astype(bf16)
    tail = glu_ext[tb:tb + CC_TAIL, :]
    glu_ext[0:CC_TAIL, :] = tail
    ccc_out_ref[0] = tail

    y = jnp.dot(oc_scr[...], wout_ref[...], preferred_element_type=f32)
    o_ref[...] = _ln(DN_ALPHA * x + y, g_ref[MIX_LN_ROW:MIX_LN_ROW + 1, :], b_ref[MIX_LN_ROW:MIX_LN_ROW + 1, :])


def _mixer0_ln(x, wmain, wba, dnw, alog, dtb, ng, ccw, ccb, cclg, cclb, wout, g, b, *, l, batch, seq, tb=512):
    T, D = x.shape
    tb = min(tb, seq)
    nb = seq // tb
    consts = [wmain, wba, dnw, alog, dtb, ng, ccw, ccb, cclg, cclb, wout]
    return pl.pallas_call(
        _mixer0_kernel,
        grid=(batch, nb),
        in_specs=[pl.BlockSpec((tb, D), lambda i, j: (i * nb + j, 0))] + [_full(c.shape) for c in consts]
        + [_layer_spec(g.shape, l), _layer_spec(b.shape, l)],
        out_specs=[
            pl.BlockSpec((tb, D), lambda i, j: (i * nb + j, 0)),
            pl.BlockSpec((1, GDN_HEADS, GDN_DK, GDN_DV), lambda i, j: (i, 0, 0, 0)),
            pl.BlockSpec((1, DN_TAIL, 3 * GDN_WIDTH), lambda i, j: (i, 0, 0)),
            pl.BlockSpec((1, CC_TAIL, CC_CH), lambda i, j: (i, 0, 0)),
        ],
        out_shape=[
            jax.ShapeDtypeStruct((T, D), f32),
            jax.ShapeDtypeStruct((batch, GDN_HEADS, GDN_DK, GDN_DV), f32),
            jax.ShapeDtypeStruct((batch, DN_TAIL, 3 * GDN_WIDTH), f32),
            jax.ShapeDtypeStruct((batch, CC_TAIL, CC_CH), f32),
        ],
        scratch_shapes=[
            pltpu.VMEM((tb + DN_TAIL, 3 * GDN_WIDTH), f32),
            pltpu.VMEM((tb + CC_TAIL, CC_CH), f32),
            pltpu.VMEM((SUBLANES - 1, tb + CC_TAIL - SUBLANES, CC_CH), f32),
            pltpu.VMEM((GDN_HEADS // 2, GDN_DK, 2 * GDN_DV), f32),
            pltpu.VMEM((tb, GDN_WIDTH + CC_CH), bf16),
        ],
        compiler_params=_cparams("parallel", "arbitrary"),
        name="mixer0_ln",
    )(x, *consts, g, b)


def _mixer0_dec_kernel(x_ref, s_ref, dnc_ref, ccc_ref, wmain_ref, wba_ref, dnw_ref, alog_ref, dtb_ref, ng_ref, ccw_ref,
                       ccb_ref, cclg_ref, cclb_ref, wout_ref, g_ref, b_ref,
                       o_ref, s_out_ref, dnc_out_ref, ccc_out_ref,
                       q_scr, k_scr, v_scr, z_scr, beta_scr, eg_scr, oc_scr, *, bs):
    i = pl.program_id(0)
    nq = 3 * GDN_WIDTH

    @pl.when(i == 0)
    def _():
        xb = x_ref[...].astype(bf16)
        proj = jnp.dot(xb, wmain_ref[...], preferred_element_type=f32)
        ba = jnp.dot(xb, wba_ref[...], preferred_element_type=f32)
        qkv_raw = proj[:, :nq]
        acc = dnw_ref[SCONV_W - 1:SCONV_W, :] * qkv_raw
        for j in range(SCONV_W - 1):
            acc = acc + dnw_ref[j:j + 1, :] * dnc_ref[:, j, :]
        for j in range(SCONV_W - 2):
            dnc_out_ref[:, j, :] = dnc_ref[:, j + 1, :]
        dnc_out_ref[:, SCONV_W - 2, :] = qkv_raw
        qkv = _silu(acc)
        for h in range(GDN_HEADS):
            hs = slice(h * GDN_DK, (h + 1) * GDN_DK)
            q_scr[:, hs] = _l2n(qkv[:, h * GDN_DK:(h + 1) * GDN_DK]) * (GDN_DK ** -0.5)
            k_scr[:, hs] = _l2n(qkv[:, GDN_WIDTH + h * GDN_DK:GDN_WIDTH + (h + 1) * GDN_DK])
        v_scr[...] = qkv[:, 2 * GDN_WIDTH:]
        z_scr[...] = _silu(proj[:, nq:nq + GDN_WIDTH])
        beta, g = _gdn_gates(ba, alog_ref[...], dtb_ref[...])
        beta_scr[...] = beta
        eg_scr[...] = jnp.exp(g)

        ga = proj[:, nq + GDN_WIDTH:nq + GDN_WIDTH + CC_CH]
        gb = proj[:, nq + GDN_WIDTH + CC_CH:]
        glu = ga * jax.nn.sigmoid(gb)
        acc = ccw_ref[CC_W - 1:CC_W, :] * glu
        for j in range(CC_W - 1):
            acc = acc + ccw_ref[j:j + 1, :] * ccc_ref[:, j, :]
        for j in range(CC_W - 2):
            ccc_out_ref[:, j, :] = ccc_ref[:, j + 1, :]
        ccc_out_ref[:, CC_W - 2, :] = glu
        cc = _silu(_ln(acc + ccb_ref[...], cclg_ref[...], cclb_ref[...]))
        oc_scr[:, GDN_WIDTH:] = cc

    rows = pl.ds(pl.multiple_of(i * bs, bs), bs)
    q_blk, k_blk, v_blk, z_blk = q_scr[rows, :], k_scr[rows, :], v_scr[rows, :], z_scr[rows, :]
    beta_blk, eg_blk = beta_scr[rows, :], eg_scr[rows, :]
    o_rows = []
    for s in range(bs):
        o_heads = []
        for h in range(GDN_HEADS):
            hs = slice(h * GDN_DK, (h + 1) * GDN_DK)
            k_col = jnp.broadcast_to(k_blk[s:s + 1, hs], (GDN_DK, GDN_DK)).T
            q_col = jnp.broadcast_to(q_blk[s:s + 1, hs], (GDN_DK, GDN_DK)).T
            b1 = beta_blk[s:s + 1, h:h + 1]
            e1 = eg_blk[s:s + 1, h:h + 1]
            s_old = s_ref[s, h]
            ks = jnp.sum(k_col * s_old, 0, keepdims=True)
            v_new = b1 * (v_blk[s:s + 1, hs] - e1 * ks)
            s_new = s_old * e1 + k_col * v_new
            s_out_ref[s, h] = s_new
            o = jnp.sum(q_col * s_new, 0, keepdims=True)
            o = o * lax.rsqrt(jnp.mean(o * o, -1, keepdims=True) + 1e-6) * ng_ref[...]
            o_heads.append(o * z_blk[s:s + 1, hs])
        o_rows.append(jnp.concatenate(o_heads, axis=-1))
    oc_scr[rows, :GDN_WIDTH] = jnp.concatenate(o_rows, axis=0)

    @pl.when(i == pl.num_programs(0) - 1)
    def _():
        y = jnp.dot(oc_scr[...].astype(bf16), wout_ref[...], preferred_element_type=f32)
        o_ref[...] = _ln(DN_ALPHA * x_ref[...] + y, g_ref[MIX_LN_ROW:MIX_LN_ROW + 1, :], b_ref[MIX_LN_ROW:MIX_LN_ROW + 1, :])


def _mixer0_dec_ln(x, s, dnc, ccc, wmain, wba, dnw, alog, dtb, ng, ccw, ccb, cclg, cclb, wout, g, b, *, l, bs=8):
    N, D = x.shape
    consts = [wmain, wba, dnw, alog, dtb, ng, ccw, ccb, cclg, cclb, wout]
    sspec = pl.BlockSpec((bs, GDN_HEADS, GDN_DK, GDN_DV), lambda i: (i, 0, 0, 0))
    return pl.pallas_call(
        functools.partial(_mixer0_dec_kernel, bs=bs),
        grid=(N // bs,),
        in_specs=[_full((N, D)), sspec, _full(dnc.shape), _full(ccc.shape)] + [_full(c.shape) for c in consts]
        + [_layer_spec(g.shape, l), _layer_spec(b.shape, l)],
        out_specs=[_full((N, D)), sspec, _full(dnc.shape), _full(ccc.shape)],
        out_shape=[jax.ShapeDtypeStruct((N, D), f32), jax.ShapeDtypeStruct(s.shape, f32),
                   jax.ShapeDtypeStruct(dnc.shape, f32), jax.ShapeDtypeStruct(ccc.shape, f32)],
        scratch_shapes=[pltpu.VMEM((N, GDN_WIDTH), f32)] * 4 + [pltpu.VMEM((N, LANES), f32)] * 2
        + [pltpu.VMEM((N, GDN_WIDTH + CC_CH), f32)],
        compiler_params=_cparams("arbitrary"),
        name="mixer0_dec_ln",
    )(x, s, dnc, ccc, *consts, g, b)


def _pad_lanes(v, n=LANES):
    return jnp.zeros((1, n), f32).at[0, :v.shape[0]].set(v.astype(f32))


def kernel(x_prompt, x_sample, mem_prompt, cache_mem_k, cache_mem_v, state_dn_S, state_dn_conv, state_cc_conv, ln_g, ln_b, ffn_w_gate, ffn_w_up, ffn_w_down, xa_wq, xa_wk, xa_wv, xa_wo, ab_w_in, dn_conv_w, dn_A_log, dn_dt_bias, dn_norm_g, cc_conv_w, cc_conv_b, cc_ln_g, cc_ln_b, ab_w_out, gm_w_in, gm_ln_g, gm_ln_b, gm_w_s, gm_b_s, gm_w_out):
    B, SEQ, D = x_prompt.shape
    N = x_sample.shape[0]
    row = lambda v: v.reshape(1, -1).astype(f32)

    wg, wu, wd = ffn_w_gate.astype(bf16), ffn_w_up.astype(bf16), ffn_w_down.astype(bf16)
    wq, wk, wv, wo = xa_wq.astype(bf16), xa_wk.astype(bf16), xa_wv.astype(bf16), xa_wo.astype(bf16)
    nq = 3 * GDN_WIDTH
    nz = nq + GDN_WIDTH
    w_main = jnp.concatenate([ab_w_in[:, :nz], ab_w_in[:, nz + 2 * GDN_HEADS:]], axis=1).astype(bf16)
    w_ba = jnp.zeros((D, 2 * LANES), f32)
    w_ba = w_ba.at[:, :GDN_HEADS].set(ab_w_in[:, nz:nz + GDN_HEADS])
    w_ba = w_ba.at[:, LANES:LANES + GDN_HEADS].set(ab_w_in[:, nz + GDN_HEADS:nz + 2 * GDN_HEADS]).astype(bf16)
    w_out = ab_w_out.astype(bf16)
    gw_in, gw_out = gm_w_in.astype(bf16), gm_w_out.astype(bf16)
    mixer_consts = (w_main, w_ba, dn_conv_w.astype(f32), _pad_lanes(dn_A_log), _pad_lanes(dn_dt_bias), row(dn_norm_g),
                    cc_conv_w.astype(f32), row(cc_conv_b), row(cc_ln_g), row(cc_ln_b), w_out)
    gm_consts = (gw_in, gm_w_s.astype(f32), gm_b_s.astype(f32)[:, :, None], row(gm_ln_g), row(gm_ln_b), gw_out)
    lng, lnb = ln_g.astype(f32), ln_b.astype(f32)

    mem_k, mem_v, mem_kh, mem_vh = _mem_kv(mem_prompt, wk, wv)

    def trunk(x, mk, mv, *, tm, prompt, state=None):
        gm_v = None
        for l in range(DEPTH):
            x = _ffn_ln(x, wg, wu, wd, lng, lnb, l=l, half=0, tm=tm)
            if l % 2 == 0:
                if prompt:
                    x, dn_s, dn_c, cc_c = _mixer0_ln(x, *mixer_consts, lng, lnb, l=l, batch=B, seq=SEQ)
                else:
                    x, dn_s, dn_c, cc_c = _mixer0_dec_ln(x, *state, *mixer_consts, lng, lnb, l=l)
            else:
                x, gm_v = _gmlp_ln(x, *gm_consts, lng, lnb, l=l, tm=tm, single=not prompt)
            if prompt:
                x = _xattn_ln(x, mk, mv, wq, wo, lng, lnb, l=l, seq=SEQ)
            else:
                x = _xattn_dec_ln(x, mk, mv, wq, wo, lng, lnb, l=l)
            x = _ffn_ln(x, wg, wu, wd, lng, lnb, l=l, half=1, tm=tm)
        return x, dn_s, dn_c, cc_c, gm_v

    yp, dn_s_p, dn_c_p, cc_c_p, _ = trunk(x_prompt.reshape(B * SEQ, D), mem_kh, mem_vh, tm=1024, prompt=True)
    ys, dn_s_s, dn_c_s, cc_c_s, gm_v_s = trunk(
        x_sample.reshape(N, D), cache_mem_k, cache_mem_v, tm=N, prompt=False,
        state=(state_dn_S, state_dn_conv, state_cc_conv))

    return (yp.reshape(B, SEQ, D), ys.reshape(N, 1, D), mem_k, mem_v,
            dn_s_p, dn_c_p[:, DN_TAIL - (SCONV_W - 1):], cc_c_p[:, CC_TAIL - (CC_W - 1):],
            dn_s_s, dn_c_s, cc_c_s, gm_v_s.reshape(N, 1, GM_WIDTH))
```

```python
import functools

import jax
import jax.numpy as jnp
from jax import lax
from jax.experimental import pallas as pl
from jax.experimental.pallas import tpu as pltpu

D_MODEL = 1024
DEPTH = 2
DN_ALPHA = (2 * DEPTH) ** 0.25
LN_EPS = 1e-5
GDN_HEADS = 4
GDN_DK = 128
GDN_DV = 128
GDN_WIDTH = GDN_HEADS * GDN_DK
SCONV_W = 4
CC_CH = D_MODEL // 2
CC_W = 31
GM_WIDTH = D_MODEL
GM_GROUPS = 4
GM_CHUNK = 128
N_MEM = 256
XA_HEADS = 4
XA_HEAD_DIM = D_MODEL // XA_HEADS
D_FF = 2816
MIX_LN_ROW = 1
XA_LN_ROW = 2

LANES = 128
SUBLANES = 8
GDN_BLOCK = 128
DN_TAIL = SUBLANES
CC_TAIL = 32
CC_ROWS = 32
DN_ROWS = 16
VMEM_LIMIT = 56 * 1024 * 1024

bf16 = jnp.bfloat16
f32 = jnp.float32


def _cparams(*sem):
    return pltpu.CompilerParams(dimension_semantics=sem, vmem_limit_bytes=VMEM_LIMIT)


def _full(shape):
    n = len(shape)
    return pl.BlockSpec(shape, lambda *_: (0,) * n)


def _dot_nt(a, b):
    return lax.dot_general(a.astype(bf16), b.astype(bf16), (((1,), (1,)), ((), ())), preferred_element_type=f32)


def _split3(a):
    a1 = a.astype(bf16)
    r = a - a1.astype(f32)
    a2 = r.astype(bf16)
    a3 = (r - a2.astype(f32)).astype(bf16)
    return a1, a2, a3


def _dot_exact_lhs(a_exact_bf16, b):
    b1, b2, b3 = _split3(b)
    d = lambda y: jnp.dot(a_exact_bf16, y, preferred_element_type=f32)
    return d(b1) + (d(b2) + d(b3))


def _ln(y, g, b):
    mu = jnp.mean(y, -1, keepdims=True)
    d = y - mu
    var = jnp.mean(d * d, -1, keepdims=True)
    return d * lax.rsqrt(var + LN_EPS) * g + b


def _silu(x):
    return x * jax.nn.sigmoid(x)


def _softplus(x):
    return jnp.maximum(x, 0.0) + jnp.log(1.0 + jnp.exp(-jnp.abs(x)))


def _ffn_kernel(x_ref, wg_ref, wu_ref, wd_ref, g_ref, b_ref, o_ref, *, ln_row, rows, tf):
    r = slice(ln_row, ln_row + 1)
    tm = x_ref.shape[0]
    nf = wg_ref.shape[1] // tf
    for r0 in range(0, tm, rows):
        x = x_ref[r0:r0 + rows, :]
        xb = x.astype(bf16)
        acc = None
        for j in range(nf):
            cs = slice(j * tf, (j + 1) * tf)
            hg = jnp.dot(xb, wg_ref[:, cs], preferred_element_type=f32)
            hu = jnp.dot(xb, wu_ref[:, cs], preferred_element_type=f32)
            d = jnp.dot((_silu(hg) * hu).astype(bf16), wd_ref[cs, :], preferred_element_type=f32)
            acc = d if acc is None else acc + d
        o_ref[r0:r0 + rows, :] = _ln(DN_ALPHA * x + 0.5 * acc, g_ref[r, :], b_ref[r, :])


def _ffn_ln(x, wg, wu, wd, g, b, *, l, half, tm, rows=512, tf=256):
    T, D = x.shape
    tm = min(tm, T)
    rows = min(rows, tm)
    F = wg.shape[-1]
    nln = g.shape[1]
    once = pl.Buffered(1)
    return pl.pallas_call(
        functools.partial(_ffn_kernel, ln_row=(nln - 1) * half, rows=rows, tf=tf),
        grid=(T // tm,),
        in_specs=[
            pl.BlockSpec((tm, D), lambda i: (i, 0)),
            pl.BlockSpec((None, None, D, F), lambda i: (l, half, 0, 0), pipeline_mode=once),
            pl.BlockSpec((None, None, D, F), lambda i: (l, half, 0, 0), pipeline_mode=once),
            pl.BlockSpec((None, None, F, D), lambda i: (l, half, 0, 0), pipeline_mode=once),
            pl.BlockSpec((None, nln, D), lambda i: (l, 0, 0)),
            pl.BlockSpec((None, nln, D), lambda i: (l, 0, 0)),
        ],
        out_specs=pl.BlockSpec((tm, D), lambda i: (i, 0)),
        out_shape=jax.ShapeDtypeStruct((T, D), f32),
        compiler_params=_cparams("parallel"),
        name="ffn_ln",
    )(x, wg, wu, wd, g, b)


def _memkv_kernel(m_ref, wk_ref, wv_ref, k_ref, v_ref, kh_ref, vh_ref):
    nb = m_ref.shape[0]
    for b in range(nb):
        mb = m_ref[b].astype(bf16)
        k = jnp.dot(mb, wk_ref[...], preferred_element_type=f32)
        v = jnp.dot(mb, wv_ref[...], preferred_element_type=f32)
        for h in range(XA_HEADS):
            sl = slice(h * XA_HEAD_DIM, (h + 1) * XA_HEAD_DIM)
            k_ref[b, :, h, :] = k[:, sl]
            v_ref[b, :, h, :] = v[:, sl]
            kh_ref[b, h] = k[:, sl].astype(bf16)
            vh_ref[b, h] = v[:, sl].astype(bf16)


def _mem_kv(mem, wk, wv, *, nb=2):
    B, M, D = mem.shape
    nb = min(nb, B)
    L = wk.shape[0]
    out = jax.ShapeDtypeStruct((L, B, M, XA_HEADS, XA_HEAD_DIM), f32)
    outh = jax.ShapeDtypeStruct((L, B, XA_HEADS, M, XA_HEAD_DIM), bf16)
    ospec = pl.BlockSpec((None, nb, M, XA_HEADS, XA_HEAD_DIM), lambda l, i: (l, i, 0, 0, 0))
    hspec = pl.BlockSpec((None, nb, XA_HEADS, M, XA_HEAD_DIM), lambda l, i: (l, i, 0, 0, 0))
    return pl.pallas_call(
        _memkv_kernel,
        grid=(L, B // nb),
        in_specs=[
            pl.BlockSpec((nb, M, D), lambda l, i: (i, 0, 0)),
            pl.BlockSpec((None, D, D), lambda l, i: (l, 0, 0)),
            pl.BlockSpec((None, D, D), lambda l, i: (l, 0, 0)),
        ],
        out_specs=[ospec, ospec, hspec, hspec],
        out_shape=[out, out, outh, outh],
        compiler_params=_cparams("parallel", "parallel"),
        name="mem_kv",
    )(mem, wk, wv)


def _xattn_kernel(x_ref, k_ref, v_ref, wq_ref, wo_ref, g_ref, b_ref, o_ref, oh_ref, *, rows):
    starts = list(range(0, x_ref.shape[0], rows))
    proj = lambda r: jnp.dot(x_ref[r:r + rows, :].astype(bf16), wq_ref[...], preferred_element_type=f32).astype(bf16)
    nxt = proj(starts[0])
    for i, r0 in enumerate(starts):
        rs = slice(r0, r0 + rows)
        x = x_ref[rs, :]
        q = nxt
        if i + 1 < len(starts):
            nxt = proj(starts[i + 1])
        for h in range(XA_HEADS):
            sl = slice(h * XA_HEAD_DIM, (h + 1) * XA_HEAD_DIM)
            s = _dot_nt(q[:, sl], k_ref[h]) * (XA_HEAD_DIM ** -0.5)
            s = s - jnp.max(s, -1, keepdims=True)
            e = jnp.exp(s)
            p = e / jnp.sum(e, -1, keepdims=True)
            oh_ref[rs, sl] = jnp.dot(p.astype(bf16), v_ref[h], preferred_element_type=f32).astype(bf16)
        att = jnp.dot(oh_ref[rs, :], wo_ref[...], preferred_element_type=f32)
        o_ref[rs, :] = _ln(DN_ALPHA * x + att, g_ref[XA_LN_ROW:XA_LN_ROW + 1, :], b_ref[XA_LN_ROW:XA_LN_ROW + 1, :])


def _layer_spec(shape, l):
    n = len(shape)
    return pl.BlockSpec((None,) + tuple(shape[1:]), lambda *_: (l,) + (0,) * (n - 1))


def _xattn_ln(x, mk, mv, wq, wo, g, b, *, l, seq, tq=1024, rows=512):
    T, D = x.shape
    tq = min(tq, seq)
    rows = min(rows, tq)
    nq = seq // tq
    mspec = pl.BlockSpec((None, None, XA_HEADS, N_MEM, XA_HEAD_DIM), lambda i: (l, i // nq, 0, 0, 0))
    return pl.pallas_call(
        functools.partial(_xattn_kernel, rows=rows),
        grid=(T // tq,),
        in_specs=[
            pl.BlockSpec((tq, D), lambda i: (i, 0)),
            mspec, mspec,
            _layer_spec(wq.shape, l), _layer_spec(wo.shape, l), _layer_spec(g.shape, l), _layer_spec(b.shape, l),
        ],
        out_specs=pl.BlockSpec((tq, D), lambda i: (i, 0)),
        out_shape=jax.ShapeDtypeStruct((T, D), f32),
        scratch_shapes=[pltpu.VMEM((tq, D), bf16)],
        compiler_params=_cparams("parallel"),
        name="xattn_ln",
    )(x, mk, mv, wq, wo, g, b)


def _xattn_dec_kernel(x_ref, k_ref, v_ref, wq_ref, wo_ref, g_ref, b_ref, o_ref, q_scr, a_scr, *, bs):
    i = pl.program_id(0)

    @pl.when(i == 0)
    def _():
        q_scr[...] = jnp.dot(x_ref[...].astype(bf16), wq_ref[...], preferred_element_type=f32)

    rows = pl.ds(pl.multiple_of(i * bs, bs), bs)
    q_blk = q_scr[rows, :]
    o_rows = []
    grp = SUBLANES // XA_HEADS
    for s in range(bs):
        q4 = jnp.concatenate([q_blk[s:s + 1, h * XA_HEAD_DIM:(h + 1) * XA_HEAD_DIM] for h in range(XA_HEADS)], axis=0)
        q8 = jnp.concatenate([q4] * grp, axis=0)
        both = lambda t: sum(t[j * XA_HEADS:(j + 1) * XA_HEADS] for j in range(grp))
        k3 = k_ref[s].reshape(N_MEM // grp, grp * XA_HEADS, XA_HEAD_DIM)
        v3 = v_ref[s].reshape(N_MEM // grp, grp * XA_HEADS, XA_HEAD_DIM)
        sc = jnp.sum(k3 * q8[None], -1, keepdims=True) * (XA_HEAD_DIM ** -0.5)
        mx = jnp.max(sc, 0)
        mx = functools.reduce(jnp.maximum, [mx[j * XA_HEADS:(j + 1) * XA_HEADS] for j in range(grp)])
        e = jnp.exp(sc - jnp.concatenate([mx] * grp, axis=0)[None])
        den = both(jnp.sum(e, 0))
        p = e / jnp.concatenate([den] * grp, axis=0)[None]
        o4 = both(jnp.sum(p * v3, 0))
        o_rows.append(jnp.concatenate([o4[h:h + 1, :] for h in range(XA_HEADS)], axis=-1))
    a_scr[rows, :] = jnp.concatenate(o_rows, axis=0)

    @pl.when(i == pl.num_programs(0) - 1)
    def _():
        att = jnp.dot(a_scr[...].astype(bf16), wo_ref[...], preferred_element_type=f32)
        o_ref[...] = _ln(DN_ALPHA * x_ref[...] + att, g_ref[XA_LN_ROW:XA_LN_ROW + 1, :], b_ref[XA_LN_ROW:XA_LN_ROW + 1, :])


def _xattn_dec_ln(x, ck, cv, wq, wo, g, b, *, l, bs=8):
    N, D = x.shape
    cspec = pl.BlockSpec((None, bs, N_MEM, XA_HEADS, XA_HEAD_DIM), lambda i: (l, i, 0, 0, 0))
    return pl.pallas_call(
        functools.partial(_xattn_dec_kernel, bs=bs),
        grid=(N // bs,),
        in_specs=[
            _full((N, D)), cspec, cspec,
            _layer_spec(wq.shape, l), _layer_spec(wo.shape, l), _layer_spec(g.shape, l), _layer_spec(b.shape, l),
        ],
        out_specs=_full((N, D)),
        out_shape=jax.ShapeDtypeStruct((N, D), f32),
        scratch_shapes=[pltpu.VMEM((N, D), f32), pltpu.VMEM((N, D), f32)],
        compiler_params=_cparams("arbitrary"),
        name="xattn_dec_ln",
    )(x, ck, cv, wq, wo, g, b)


def _gmlp_kernel(x_ref, win_ref, ws_ref, bs_ref, vg_ref, vb_ref, wout_ref, g_ref, b_ref, o_ref, *rest, single, rows):
    v_ref, uf_ref = rest if single else (None, rest[0])
    gw = GM_WIDTH // GM_GROUPS
    if not single:
        row = lax.broadcasted_iota(jnp.int32, (GM_CHUNK, GM_CHUNK), 0)
        col = lax.broadcasted_iota(jnp.int32, (GM_CHUNK, GM_CHUNK), 1)
        wmask = [jnp.where(col <= row, ws_ref[g], 0.0).astype(bf16) for g in range(GM_GROUPS)]
    starts = list(range(0, x_ref.shape[0], rows))
    proj = lambda r: jnp.dot(x_ref[r:r + rows, :].astype(bf16), win_ref[...], preferred_element_type=f32)
    nxt = proj(starts[0])
    for i, r0 in enumerate(starts):
        x = x_ref[r0:r0 + rows, :]
        cur = nxt
        if i + 1 < len(starts):
            nxt = proj(starts[i + 1])
        pr = jax.nn.gelu(cur)
        u = pr[:, :GM_WIDTH]
        v = _ln(pr[:, GM_WIDTH:], vg_ref[...], vb_ref[...])
        if single:
            v_ref[r0:r0 + rows, :] = v
            for g in range(GM_GROUPS):
                sl = slice(g * gw, (g + 1) * gw)
                f = ws_ref[g][0:1, 0:1] * v[:, sl] + bs_ref[g][0:1, 0:1]
                uf_ref[r0:r0 + rows, sl] = (u[:, sl] * f).astype(bf16)
        else:
            vb16 = v.astype(bf16)
            for g in range(GM_GROUPS):
                sl = slice(g * gw, (g + 1) * gw)
                for c in range(rows // GM_CHUNK):
                    cs = slice(c * GM_CHUNK, (c + 1) * GM_CHUNK)
                    f = jnp.dot(wmask[g], vb16[cs, sl], preferred_element_type=f32) + bs_ref[g]
                    uf_ref[r0 + c * GM_CHUNK:r0 + (c + 1) * GM_CHUNK, sl] = (u[cs, sl] * f).astype(bf16)
        y = jnp.dot(uf_ref[r0:r0 + rows, :], wout_ref[...], preferred_element_type=f32)
        o_ref[r0:r0 + rows, :] = _ln(DN_ALPHA * x + y, g_ref[MIX_LN_ROW:MIX_LN_ROW + 1, :], b_ref[MIX_LN_ROW:MIX_LN_ROW + 1, :])


def _gmlp_ln(x, w_in, w_s, b_s, vg, vb, w_out, g, b, *, l, tm, single, rows=512):
    T, D = x.shape
    tm = min(tm, T)
    rows = min(rows, tm)
    xspec = pl.BlockSpec((tm, D), lambda i: (i, 0))
    out = jax.ShapeDtypeStruct((T, D), f32)
    res = pl.pallas_call(
        functools.partial(_gmlp_kernel, single=single, rows=rows),
        grid=(T // tm,),
        in_specs=[
            xspec, _full((D, 2 * GM_WIDTH)), _full(w_s.shape), _full(b_s.shape),
            _full((1, GM_WIDTH)), _full((1, GM_WIDTH)), _full((GM_WIDTH, D)), _layer_spec(g.shape, l), _layer_spec(b.shape, l),
        ],
        out_specs=[xspec, pl.BlockSpec((tm, GM_WIDTH), lambda i: (i, 0))] if single else xspec,
        out_shape=[out, jax.ShapeDtypeStruct((T, GM_WIDTH), f32)] if single else out,
        scratch_shapes=[pltpu.VMEM((tm, GM_WIDTH), bf16)],
        compiler_params=_cparams("parallel"),
        name="gmlp_ln",
    )(x, w_in, w_s, b_s, vg, vb, w_out, g, b)
    return res if single else (res, None)


def _gdn_gates(ba, alog, dtb):
    beta = jax.nn.sigmoid(ba[:, :LANES])
    g = -jnp.exp(alog) * _softplus(ba[:, LANES:] + dtb)
    return beta, g


def _l2n(x):
    return x * lax.rsqrt(jnp.sum(x * x, -1, keepdims=True) + 1e-6)


def _mixer0_kernel(x_ref, wmain_ref, wba_ref, dnw_ref, alog_ref, dtb_ref, ng_ref, ccw_ref, ccb_ref, cclg_ref, cclb_ref,
                   wout_ref, g_ref, b_ref,
                   o_ref, s_out_ref, dnc_out_ref, ccc_out_ref,
                   qkv_ext, glu_ext, rot_scr, s_scr, oc_scr):
    blk = pl.program_id(1)
    tb = x_ref.shape[0]
    C = GDN_BLOCK

    @pl.when(blk == 0)
    def _():
        qkv_ext[0:DN_TAIL, :] = jnp.zeros((DN_TAIL, 3 * GDN_WIDTH), f32)
        glu_ext[0:CC_TAIL, :] = jnp.zeros((CC_TAIL, CC_CH), f32)
        s_scr[...] = jnp.zeros_like(s_scr)

    x = x_ref[...]
    xb = x.astype(bf16)
    proj = jnp.dot(xb, wmain_ref[...], preferred_element_type=f32)
    ba = jnp.dot(xb, wba_ref[...], preferred_element_type=f32)
    nq = 3 * GDN_WIDTH

    qkv_ext[DN_TAIL:DN_TAIL + tb, :] = proj[:, :nq]
    blocks = []
    for i in range(tb // DN_ROWS):
        acc = None
        for s in range(SCONV_W):
            term = dnw_ref[SCONV_W - 1 - s:SCONV_W - s, :] * qkv_ext[pl.ds(DN_TAIL - s + i * DN_ROWS, DN_ROWS), :]
            acc = term if acc is None else acc + term
        blocks.append(_silu(acc))
    qkv = jnp.concatenate(blocks, axis=0)
    tail = qkv_ext[tb:tb + DN_TAIL, :]
    qkv_ext[0:DN_TAIL, :] = tail
    dnc_out_ref[0] = tail

    beta, g = _gdn_gates(ba, alog_ref[...], dtb_ref[...])

    P = 2 * GDN_DK
    pairs = range(GDN_HEADS // 2)
    chunks = range(tb // C)
    row = lax.broadcasted_iota(jnp.int32, (C, P), 0)
    col = lax.broadcasted_iota(jnp.int32, (C, P), 1)
    col = jnp.where(col >= C, col - C, col)
    causal = col <= row
    strict = col < row
    ltri = jnp.where(causal[:, :C], 1.0, 0.0).astype(bf16)
    qn = [_l2n(qkv[:, h * GDN_DK:(h + 1) * GDN_DK]) * (GDN_DK ** -0.5) for h in range(GDN_HEADS)]
    kn = [_l2n(qkv[:, GDN_WIDTH + h * GDN_DK:GDN_WIDTH + (h + 1) * GDN_DK]) for h in range(GDN_HEADS)]

    def pair_cols(m, h0):
        return jnp.concatenate([jnp.broadcast_to(m[:, h0:h0 + 1], (C, GDN_DK)),
                                jnp.broadcast_to(m[:, h0 + 1:h0 + 2], (C, GDN_DK))], axis=1)

    def bdiag(m):
        z = jnp.zeros((C, C), m.dtype)
        return jnp.concatenate([jnp.concatenate([m[:, :C], z], axis=1), jnp.concatenate([z, m[:, C:]], axis=1)], axis=0)

    def split2(a):
        a1 = a.astype(bf16)
        return a1, (a - a1.astype(f32)).astype(bf16)

    mm = lambda a, b_: jnp.dot(a, b_, preferred_element_type=f32)

    prob = {}
    for c in chunks:
        rs = slice(c * C, (c + 1) * C)
        gc = _dot_exact_lhs(ltri, g[rs])
        gct = gc.T
        eg = jnp.exp(gc)
        g_last = gc[C - 1:C, :]
        ekt = jnp.exp(g_last - gc)
        egl = jnp.exp(g_last)
        for pr in pairs:
            h0 = 2 * pr
            k_pair = jnp.concatenate([kn[h0][rs], kn[h0 + 1][rs]], axis=1)
            q_pair = jnp.concatenate([qn[h0][rs], qn[h0 + 1][rs]], axis=1)
            v_pair = qkv[rs, 2 * GDN_WIDTH + pr * P:2 * GDN_WIDTH + (pr + 1) * P]
            beta_pair = pair_cols(beta[rs], h0)
            grow = jnp.concatenate([gct[h0:h0 + 1, :], gct[h0 + 1:h0 + 2, :]], axis=1)
            decay = jnp.where(causal, jnp.exp(jnp.where(causal, pair_cols(gc, h0) - grow, 0.0)), 0.0)
            kb = k_pair * beta_pair
            prod = _dot_nt(jnp.concatenate([kb, q_pair], axis=0), bdiag(k_pair.astype(bf16)))
            eg_pair = pair_cols(eg, h0)
            kt = k_pair * pair_cols(ekt, h0)
            prob[c, pr] = dict(
                n=-jnp.where(strict, prod[:C] * decay, 0.0),
                qk=(prod[C:] * decay).astype(bf16),
                vb=(v_pair * beta_pair).astype(bf16),
                kbe=(kb * eg_pair).astype(bf16),
                qg=(q_pair * eg_pair).astype(bf16),
                ktt=jnp.concatenate([kt[:, :C].T, kt[:, C:].T], axis=1).astype(bf16),
                egl=jnp.concatenate([jnp.broadcast_to(egl[:, h0:h0 + 1], (1, GDN_DK)),
                                     jnp.broadcast_to(egl[:, h0 + 1:h0 + 2], (1, GDN_DK))], axis=1))

    Hc = C // 2
    lane = lax.broadcasted_iota(jnp.int32, (Hc, P), 1)
    prow = lax.broadcasted_iota(jnp.int32, (Hc, P), 0)
    first_half = jnp.where(lane >= C, lane - C, lane) < Hc
    eye4 = jnp.where((lane & (Hc - 1)) == prow, 1.0, 0.0)
    quarter = [(lane >= q * Hc) & (lane < (q + 1) * Hc) for q in range(P // Hc)]

    def bdiag4(m):
        return jnp.concatenate([jnp.where(qm, m, jnp.zeros_like(m)) for qm in quarter], axis=0)

    def dot3(l1, l2, w1, w2):
        r = l1.shape[0]
        o = mm(jnp.concatenate([l1, l2], axis=0), w1)
        return o[:r] + (o[r:] + mm(l1, w2))

    nk = {key: jnp.where(first_half, pb["n"][:Hc], pb["n"][Hc:]) for key, pb in prob.items()}
    pk = {key: eye4 + nk[key] for key in prob}
    levels = Hc.bit_length() - 1
    for j in range(levels):
        first, last = j == 0, j == levels - 1
        for key in prob:
            n1, n2 = split2(nk[key])
            if first:
                l1, l2 = n1, n2
            else:
                p1, p2 = split2(pk[key])
                l1 = p1 if last else jnp.concatenate([n1, p1], axis=0)
                l2 = p2 if last else jnp.concatenate([n2, p2], axis=0)
            res = dot3(l1, l2, bdiag4(n1), bdiag4(n2))
            if first:
                nk[key] = res
            elif last:
                pk[key] = pk[key] + res
            else:
                nk[key] = res[:Hc]
                pk[key] = pk[key] + res[Hc:]
    crow = lax.broadcasted_iota(jnp.int32, (C, P), 0)
    c_block = (crow >= Hc) & (col < Hc)
    for key, pb in prob.items():
        t_pan = pk[key]
        t1, t2 = split2(t_pan)
        c1, c2 = split2(jnp.where(c_block, pb["n"], 0.0))
        x1, x2 = split2(dot3(t1, t2, bdiag(c1), bdiag(c2)))
        zero = jnp.zeros_like(t1)
        low = dot3(x1, x2, bdiag4(jnp.where(first_half, t1, zero)), bdiag4(jnp.where(first_half, t2, zero)))
        pk[key] = jnp.concatenate([jnp.where(first_half, t_pan, 0.0), low + jnp.where(first_half, 0.0, t_pan)], axis=0)

    for key, pb in prob.items():
        t16 = pk[key].astype(bf16)
        pb["u"] = mm(t16, bdiag(pb["vb"]))
        pb["w"] = mm(t16, bdiag(pb["kbe"])).astype(bf16)

    for c in chunks:
        rs = slice(c * C, (c + 1) * C)
        for pr in pairs:
            pb = prob[c, pr]
            s_pair = s_scr[pr]
            o2 = mm(jnp.concatenate([pb["w"], pb["qg"]], axis=0), bdiag(s_pair.astype(bf16)))
            vbd = bdiag((pb["u"] - o2[:C]).astype(bf16))
            o = o2[C:] + mm(pb["qk"], vbd)
            s_scr[pr] = s_pair * pb["egl"] + mm(pb["ktt"], vbd)
            on = [o[:, j * GDN_DV:(j + 1) * GDN_DV] for j in range(2)]
            on = [t * lax.rsqrt(jnp.mean(t * t, -1, keepdims=True) + 1e-6) * ng_ref[...] for t in on]
            z_pair = proj[rs, nq + pr * P:nq + (pr + 1) * P]
            oc_scr[rs, pr * P:(pr + 1) * P] = (jnp.concatenate(on, axis=1) * _silu(z_pair)).astype(bf16)

    for pr in pairs:
        s_out_ref[0, 2 * pr] = s_scr[pr][:, :GDN_DV]
        s_out_ref[0, 2 * pr + 1] = s_scr[pr][:, GDN_DV:]

    ga = proj[:, nq + GDN_WIDTH:nq + GDN_WIDTH + CC_CH]
    gb = proj[:, nq + GDN_WIDTH + CC_CH:]
    glu = ga * jax.nn.sigmoid(gb)
    glu_ext[CC_TAIL:CC_TAIL + tb, :] = glu
    span = CC_TAIL - SUBLANES
    for r in range(1, SUBLANES):
        rot_scr[r - 1] = glu_ext[pl.ds(SUBLANES - r, tb + span), :]

    for base in range(0, tb, CC_ROWS):
        acc = None
        for s in range(CC_W):
            a, r = divmod(s, SUBLANES)
            off = span - SUBLANES * a
            if r == 0:
                xs = glu_ext[base + SUBLANES + off:base + SUBLANES + off + CC_ROWS, :]
            else:
                xs = rot_scr[r - 1, base + off:base + off + CC_ROWS, :]
            term = ccw_ref[CC_W - 1 - s:CC_W - s, :] * xs
            acc = term if acc is None else acc + term
        cc = _silu(_ln(acc + ccb_ref[...], cclg_ref[...], cclb_ref[...]))
        oc_scr[base:base + CC_ROWS, GDN_WIDTH:] = cc.astype(bf16)
    tail = glu_ext[tb:tb + CC_TAIL, :]
    glu_ext[0:CC_TAIL, :] = tail
    ccc_out_ref[0] = tail

    y = jnp.dot(oc_scr[...], wout_ref[...], preferred_element_type=f32)
    o_ref[...] = _ln(DN_ALPHA * x + y, g_ref[MIX_LN_ROW:MIX_LN_ROW + 1, :], b_ref[MIX_LN_ROW:MIX_LN_ROW + 1, :])


def _mixer0_ln(x, wmain, wba, dnw, alog, dtb, ng, ccw, ccb, cclg, cclb, wout, g, b, *, l, batch, seq, tb=512):
    T, D = x.shape
    tb = min(tb, seq)
    nb = seq // tb
    consts = [wmain, wba, dnw, alog, dtb, ng, ccw, ccb, cclg, cclb, wout]
    return pl.pallas_call(
        _mixer0_kernel,
        grid=(batch, nb),
        in_specs=[pl.BlockSpec((tb, D), lambda i, j: (i * nb + j, 0))] + [_full(c.shape) for c in consts]
        + [_layer_spec(g.shape, l), _layer_spec(b.shape, l)],
        out_specs=[
            pl.BlockSpec((tb, D), lambda i, j: (i * nb + j, 0)),
            pl.BlockSpec((1, GDN_HEADS, GDN_DK, GDN_DV), lambda i, j: (i, 0, 0, 0)),
            pl.BlockSpec((1, DN_TAIL, 3 * GDN_WIDTH), lambda i, j: (i, 0, 0)),
            pl.BlockSpec((1, CC_TAIL, CC_CH), lambda i, j: (i, 0, 0)),
        ],
        out_shape=[
            jax.ShapeDtypeStruct((T, D), f32),
            jax.ShapeDtypeStruct((batch, GDN_HEADS, GDN_DK, GDN_DV), f32),
            jax.ShapeDtypeStruct((batch, DN_TAIL, 3 * GDN_WIDTH), f32),
            jax.ShapeDtypeStruct((batch, CC_TAIL, CC_CH), f32),
        ],
        scratch_shapes=[
            pltpu.VMEM((tb + DN_TAIL, 3 * GDN_WIDTH), f32),
            pltpu.VMEM((tb + CC_TAIL, CC_CH), f32),
            pltpu.VMEM((SUBLANES - 1, tb + CC_TAIL - SUBLANES, CC_CH), f32),
            pltpu.VMEM((GDN_HEADS // 2, GDN_DK, 2 * GDN_DV), f32),
            pltpu.VMEM((tb, GDN_WIDTH + CC_CH), bf16),
        ],
        compiler_params=_cparams("parallel", "arbitrary"),
        name="mixer0_ln",
    )(x, *consts, g, b)


def _mixer0_dec_kernel(x_ref, s_ref, dnc_ref, ccc_ref, wmain_ref, wba_ref, dnw_ref, alog_ref, dtb_ref, ng_ref, ccw_ref,
                       ccb_ref, cclg_ref, cclb_ref, wout_ref, g_ref, b_ref,
                       o_ref, s_out_ref, dnc_out_ref, ccc_out_ref,
                       q_scr, k_scr, v_scr, z_scr, beta_scr, eg_scr, oc_scr, *, bs):
    i = pl.program_id(0)
    nq = 3 * GDN_WIDTH

    @pl.when(i == 0)
    def _():
        xb = x_ref[...].astype(bf16)
        proj = jnp.dot(xb, wmain_ref[...], preferred_element_type=f32)
        ba = jnp.dot(xb, wba_ref[...], preferred_element_type=f32)
        qkv_raw = proj[:, :nq]
        acc = dnw_ref[SCONV_W - 1:SCONV_W, :] * qkv_raw
        for j in range(SCONV_W - 1):
            acc = acc + dnw_ref[j:j + 1, :] * dnc_ref[j]
        for j in range(SCONV_W - 2):
            dnc_out_ref[j] = dnc_ref[j + 1]
        dnc_out_ref[SCONV_W - 2] = qkv_raw
        qkv = _silu(acc)
        for h in range(GDN_HEADS):
            hs = slice(h * GDN_DK, (h + 1) * GDN_DK)
            q_scr[:, hs] = _l2n(qkv[:, h * GDN_DK:(h + 1) * GDN_DK]) * (GDN_DK ** -0.5)
            k_scr[:, hs] = _l2n(qkv[:, GDN_WIDTH + h * GDN_DK:GDN_WIDTH + (h + 1) * GDN_DK])
        v_scr[...] = qkv[:, 2 * GDN_WIDTH:]
        z_scr[...] = _silu(proj[:, nq:nq + GDN_WIDTH])
        beta, g = _gdn_gates(ba, alog_ref[...], dtb_ref[...])
        beta_scr[...] = beta
        eg_scr[...] = jnp.exp(g)

        ga = proj[:, nq + GDN_WIDTH:nq + GDN_WIDTH + CC_CH]
        gb = proj[:, nq + GDN_WIDTH + CC_CH:]
        glu = ga * jax.nn.sigmoid(gb)
        acc = ccw_ref[CC_W - 1:CC_W, :] * glu
        for j in range(CC_W - 1):
            acc = acc + ccw_ref[j:j + 1, :] * ccc_ref[j]
        for j in range(CC_W - 2):
            ccc_out_ref[j] = ccc_ref[j + 1]
        ccc_out_ref[CC_W - 2] = glu
        cc = _silu(_ln(acc + ccb_ref[...], cclg_ref[...], cclb_ref[...]))
        oc_scr[:, GDN_WIDTH:] = cc

    rows = pl.ds(pl.multiple_of(i * bs, bs), bs)
    q_blk, k_blk, v_blk, z_blk = q_scr[rows, :], k_scr[rows, :], v_scr[rows, :], z_scr[rows, :]
    beta_blk, eg_blk = beta_scr[rows, :], eg_scr[rows, :]
    o_rows = []
    for s in range(bs):
        o_heads = []
        for h in range(GDN_HEADS):
            hs = slice(h * GDN_DK, (h + 1) * GDN_DK)
            k_col = jnp.broadcast_to(k_blk[s:s + 1, hs], (GDN_DK, GDN_DK)).T
            q_col = jnp.broadcast_to(q_blk[s:s + 1, hs], (GDN_DK, GDN_DK)).T
            b1 = beta_blk[s:s + 1, h:h + 1]
            e1 = eg_blk[s:s + 1, h:h + 1]
            s_old = s_ref[s, h]
            ks = jnp.sum(k_col * s_old, 0, keepdims=True)
            v_new = b1 * (v_blk[s:s + 1, hs] - e1 * ks)
            s_new = s_old * e1 + k_col * v_new
            s_out_ref[s, h] = s_new
            o = jnp.sum(q_col * s_new, 0, keepdims=True)
            o = o * lax.rsqrt(jnp.mean(o * o, -1, keepdims=True) + 1e-6) * ng_ref[...]
            o_heads.append(o * z_blk[s:s + 1, hs])
        o_rows.append(jnp.concatenate(o_heads, axis=-1))
    oc_scr[rows, :GDN_WIDTH] = jnp.concatenate(o_rows, axis=0)

    @pl.when(i == pl.num_programs(0) - 1)
    def _():
        y = jnp.dot(oc_scr[...].astype(bf16), wout_ref[...], preferred_element_type=f32)
        o_ref[...] = _ln(DN_ALPHA * x_ref[...] + y, g_ref[MIX_LN_ROW:MIX_LN_ROW + 1, :], b_ref[MIX_LN_ROW:MIX_LN_ROW + 1, :])


def _mixer0_dec_ln(x, s, dnc, ccc, wmain, wba, dnw, alog, dtb, ng, ccw, ccb, cclg, cclb, wout, g, b, *, l, bs=8):
    N, D = x.shape
    consts = [wmain, wba, dnw, alog, dtb, ng, ccw, ccb, cclg, cclb, wout]
    sspec = pl.BlockSpec((bs, GDN_HEADS, GDN_DK, GDN_DV), lambda i: (i, 0, 0, 0))
    return pl.pallas_call(
        functools.partial(_mixer0_dec_kernel, bs=bs),
        grid=(N // bs,),
        in_specs=[_full((N, D)), sspec, _full(dnc.shape), _full(ccc.shape)] + [_full(c.shape) for c in consts]
        + [_layer_spec(g.shape, l), _layer_spec(b.shape, l)],
        out_specs=[_full((N, D)), sspec, _full(dnc.shape), _full(ccc.shape)],
        out_shape=[jax.ShapeDtypeStruct((N, D), f32), jax.ShapeDtypeStruct(s.shape, f32),
                   jax.ShapeDtypeStruct(dnc.shape, f32), jax.ShapeDtypeStruct(ccc.shape, f32)],
        scratch_shapes=[pltpu.VMEM((N, GDN_WIDTH), f32)] * 4 + [pltpu.VMEM((N, LANES), f32)] * 2
        + [pltpu.VMEM((N, GDN_WIDTH + CC_CH), f32)],
        compiler_params=_cparams("arbitrary"),
        name="mixer0_dec_ln",
    )(x, s, dnc, ccc, *consts, g, b)


def _pad_lanes(v, n=LANES):
    return jnp.zeros((1, n), f32).at[0, :v.shape[0]].set(v.astype(f32))


def kernel(x_prompt, x_sample, mem_prompt, cache_mem_k, cache_mem_v, state_dn_S, state_dn_conv, state_cc_conv, ln_g, ln_b, ffn_w_gate, ffn_w_up, ffn_w_down, xa_wq, xa_wk, xa_wv, xa_wo, ab_w_in, dn_conv_w, dn_A_log, dn_dt_bias, dn_norm_g, cc_conv_w, cc_conv_b, cc_ln_g, cc_ln_b, ab_w_out, gm_w_in, gm_ln_g, gm_ln_b, gm_w_s, gm_b_s, gm_w_out):
    B, SEQ, D = x_prompt.shape
    N = x_sample.shape[0]
    row = lambda v: v.reshape(1, -1).astype(f32)

    wg, wu, wd = ffn_w_gate.astype(bf16), ffn_w_up.astype(bf16), ffn_w_down.astype(bf16)
    wq, wk, wv, wo = xa_wq.astype(bf16), xa_wk.astype(bf16), xa_wv.astype(bf16), xa_wo.astype(bf16)
    nq = 3 * GDN_WIDTH
    nz = nq + GDN_WIDTH
    w_main = jnp.concatenate([ab_w_in[:, :nz], ab_w_in[:, nz + 2 * GDN_HEADS:]], axis=1).astype(bf16)
    w_ba = jnp.zeros((D, 2 * LANES), f32)
    w_ba = w_ba.at[:, :GDN_HEADS].set(ab_w_in[:, nz:nz + GDN_HEADS])
    w_ba = w_ba.at[:, LANES:LANES + GDN_HEADS].set(ab_w_in[:, nz + GDN_HEADS:nz + 2 * GDN_HEADS]).astype(bf16)
    w_out = ab_w_out.astype(bf16)
    gw_in, gw_out = gm_w_in.astype(bf16), gm_w_out.astype(bf16)
    mixer_consts = (w_main, w_ba, dn_conv_w.astype(f32), _pad_lanes(dn_A_log), _pad_lanes(dn_dt_bias), row(dn_norm_g),
                    cc_conv_w.astype(f32), row(cc_conv_b), row(cc_ln_g), row(cc_ln_b), w_out)
    gm_consts = (gw_in, gm_w_s.astype(f32), gm_b_s.astype(f32)[:, :, None], row(gm_ln_g), row(gm_ln_b), gw_out)
    lng, lnb = ln_g.astype(f32), ln_b.astype(f32)

    mem_k, mem_v, mem_kh, mem_vh = _mem_kv(mem_prompt, wk, wv)

    def trunk(x, mk, mv, *, tm, prompt, state=None):
        gm_v = None
        for l in range(DEPTH):
            x = _ffn_ln(x, wg, wu, wd, lng, lnb, l=l, half=0, tm=tm)
            if l % 2 == 0:
                if prompt:
                    x, dn_s, dn_c, cc_c = _mixer0_ln(x, *mixer_consts, lng, lnb, l=l, batch=B, seq=SEQ)
                else:
                    x, dn_s, dn_c, cc_c = _mixer0_dec_ln(x, *state, *mixer_consts, lng, lnb, l=l)
            else:
                x, gm_v = _gmlp_ln(x, *gm_consts, lng, lnb, l=l, tm=tm, single=not prompt)
            if prompt:
                x = _xattn_ln(x, mk, mv, wq, wo, lng, lnb, l=l, seq=SEQ)
            else:
                x = _xattn_dec_ln(x, mk, mv, wq, wo, lng, lnb, l=l)
            x = _ffn_ln(x, wg, wu, wd, lng, lnb, l=l, half=1, tm=tm)
        return x, dn_s, dn_c, cc_c, gm_v

    yp, dn_s_p, dn_c_p, cc_c_p, _ = trunk(x_prompt.reshape(B * SEQ, D), mem_kh, mem_vh, tm=1024, prompt=True)
    ys, dn_s_s, dn_c_s, cc_c_s, gm_v_s = trunk(
        x_sample.reshape(N, D), cache_mem_k, cache_mem_v, tm=N, prompt=False,
        state=(state_dn_S, jnp.swapaxes(state_dn_conv, 0, 1), jnp.swapaxes(state_cc_conv, 0, 1)))

    return (yp.reshape(B, SEQ, D), ys.reshape(N, 1, D), mem_k, mem_v,
            dn_s_p, dn_c_p[:, DN_TAIL - (SCONV_W - 1):], cc_c_p[:, CC_TAIL - (CC_W - 1):],
            dn_s_s, jnp.swapaxes(dn_c_s, 0, 1), jnp.swapaxes(cc_c_s, 0, 1), gm_v_s.reshape(N, 1, GM_WIDTH))
```

```python
import functools

import jax
import jax.numpy as jnp
from jax import lax
from jax.experimental import pallas as pl
from jax.experimental.pallas import tpu as pltpu

D_MODEL = 1024
DEPTH = 2
DN_ALPHA = (2 * DEPTH) ** 0.25
LN_EPS = 1e-5
GDN_HEADS = 4
GDN_DK = 128
GDN_DV = 128
GDN_WIDTH = GDN_HEADS * GDN_DK
SCONV_W = 4
CC_CH = D_MODEL // 2
CC_W = 31
GM_WIDTH = D_MODEL
GM_GROUPS = 4
GM_CHUNK = 128
N_MEM = 256
XA_HEADS = 4
XA_HEAD_DIM = D_MODEL // XA_HEADS
D_FF = 2816
MIX_LN_ROW = 1
XA_LN_ROW = 2

LANES = 128
SUBLANES = 8
GDN_BLOCK = 128
DN_TAIL = SUBLANES
CC_TAIL = 32
CC_ROWS = 32
DN_ROWS = 16
FFN_ROWS = 1024
FFN_ATTEND_ROWS = 512
VMEM_LIMIT = 56 * 1024 * 1024

bf16 = jnp.bfloat16
f32 = jnp.float32


def _cparams(*sem):
    return pltpu.CompilerParams(dimension_semantics=sem, vmem_limit_bytes=VMEM_LIMIT)


def _full(shape):
    n = len(shape)
    return pl.BlockSpec(shape, lambda *_: (0,) * n)


def _dot_nt(a, b):
    return lax.dot_general(a.astype(bf16), b.astype(bf16), (((1,), (1,)), ((), ())), preferred_element_type=f32)


def _split3(a):
    a1 = a.astype(bf16)
    r = a - a1.astype(f32)
    a2 = r.astype(bf16)
    a3 = (r - a2.astype(f32)).astype(bf16)
    return a1, a2, a3


def _dot_exact_lhs(a_exact_bf16, b):
    b1, b2, b3 = _split3(b)
    d = lambda y: jnp.dot(a_exact_bf16, y, preferred_element_type=f32)
    return d(b1) + (d(b2) + d(b3))


def _ln(y, g, b):
    mu = jnp.mean(y, -1, keepdims=True)
    d = y - mu
    var = jnp.mean(d * d, -1, keepdims=True)
    return d * lax.rsqrt(var + LN_EPS) * g + b


def _silu(x):
    return x * jax.nn.sigmoid(x)


def _softplus(x):
    return jnp.maximum(x, 0.0) + jnp.log(1.0 + jnp.exp(-jnp.abs(x)))


def _decode_attend(q_rows, k_ref, v_ref):
    grp = SUBLANES // XA_HEADS
    o_rows = []
    for s in range(q_rows.shape[0]):
        q4 = jnp.concatenate([q_rows[s:s + 1, h * XA_HEAD_DIM:(h + 1) * XA_HEAD_DIM] for h in range(XA_HEADS)], axis=0)
        q8 = jnp.concatenate([q4] * grp, axis=0)
        both = lambda t: sum(t[j * XA_HEADS:(j + 1) * XA_HEADS] for j in range(grp))
        k3 = k_ref[s].reshape(N_MEM // grp, grp * XA_HEADS, XA_HEAD_DIM)
        v3 = v_ref[s].reshape(N_MEM // grp, grp * XA_HEADS, XA_HEAD_DIM)
        sc = jnp.sum(k3 * q8[None], -1, keepdims=True) * (XA_HEAD_DIM ** -0.5)
        mx = jnp.max(sc, 0)
        mx = functools.reduce(jnp.maximum, [mx[j * XA_HEADS:(j + 1) * XA_HEADS] for j in range(grp)])
        e = jnp.exp(sc - jnp.concatenate([mx] * grp, axis=0)[None])
        den = both(jnp.sum(e, 0))
        p = e / jnp.concatenate([den] * grp, axis=0)[None]
        o4 = both(jnp.sum(p * v3, 0))
        o_rows.append(jnp.concatenate([o4[h:h + 1, :] for h in range(XA_HEADS)], axis=-1))
    return jnp.concatenate(o_rows, axis=0)


def _ffn_kernel(*refs, ln_row, rows, tf, samples, attend):
    x_ref, wg_ref, wu_ref, wd_ref, g_ref, b_ref = refs[:6]
    pos = 6
    if samples:
        xs_ref = refs[pos]
        pos += 1
    if attend:
        ck_ref, cv_ref, wq_ref, wo_ref = refs[pos:pos + 4]
        pos += 4
    o_ref = refs[pos]
    pos += 1
    if samples:
        os_ref = refs[pos]
        pos += 1
    if attend:
        q_scr, a_scr = refs[pos:pos + 2]
    i = pl.program_id(0)
    r = slice(ln_row, ln_row + 1)
    xr = slice(XA_LN_ROW, XA_LN_ROW + 1)
    nf = wg_ref.shape[1] // tf

    def ffn(x):
        xb = x.astype(bf16)
        acc = None
        for j in range(nf):
            cs = slice(j * tf, (j + 1) * tf)
            hg = jnp.dot(xb, wg_ref[:, cs], preferred_element_type=f32)
            hu = jnp.dot(xb, wu_ref[:, cs], preferred_element_type=f32)
            d = jnp.dot((_silu(hg) * hu).astype(bf16), wd_ref[cs, :], preferred_element_type=f32)
            acc = d if acc is None else acc + d
        return _ln(DN_ALPHA * x + 0.5 * acc, g_ref[r, :], b_ref[r, :])

    if attend:
        @pl.when(i == 0)
        def _():
            q_scr[...] = jnp.dot(xs_ref[...].astype(bf16), wq_ref[...], preferred_element_type=f32)
            a_scr[...] = jnp.zeros_like(a_scr)

    for r0 in range(0, x_ref.shape[0], rows):
        o_ref[r0:r0 + rows, :] = ffn(x_ref[r0:r0 + rows, :])

    if attend:
        bs = ck_ref.shape[0]
        share = max(SUBLANES // bs, 1)
        nrow = bs * share
        blk = pl.ds(pl.multiple_of((i // share) * nrow, nrow), nrow)
        q_blk = q_scr[blk, :]
        sub = i % share
        q_rows = q_blk[:bs]
        for j in range(1, share):
            q_rows = jnp.where(sub == j, q_blk[j * bs:(j + 1) * bs], q_rows)
        o_rows = _decode_attend(q_rows, ck_ref, cv_ref)
        if share > 1:
            a_blk = a_scr[blk, :]
            o_rows = jnp.concatenate([jnp.where(sub == j, o_rows, a_blk[j * bs:(j + 1) * bs]) for j in range(share)], axis=0)
        a_scr[blk, :] = o_rows

    if samples:
        @pl.when(i == pl.num_programs(0) - 1)
        def _():
            xs = xs_ref[...]
            if attend:
                att = jnp.dot(a_scr[...].astype(bf16), wo_ref[...], preferred_element_type=f32)
                xs = _ln(DN_ALPHA * xs + att, g_ref[xr, :], b_ref[xr, :])
            os_ref[...] = ffn(xs)


def _ffn_ln(x, wg, wu, wd, g, b, *, l, half, tm, rows=512, tf=256, xs=None, attend=None):
    T, D = x.shape
    tm = min(tm, T)
    rows = min(rows, tm)
    F = wg.shape[-1]
    nln = g.shape[1]
    steps = T // tm
    once = pl.Buffered(1)
    xspec = pl.BlockSpec((tm, D), lambda i: (i, 0))
    operands = [x, wg, wu, wd, g, b]
    in_specs = [
        xspec,
        pl.BlockSpec((None, None, D, F), lambda i: (l, half, 0, 0), pipeline_mode=once),
        pl.BlockSpec((None, None, D, F), lambda i: (l, half, 0, 0), pipeline_mode=once),
        pl.BlockSpec((None, None, F, D), lambda i: (l, half, 0, 0), pipeline_mode=once),
        pl.BlockSpec((None, nln, D), lambda i: (l, 0, 0)),
        pl.BlockSpec((None, nln, D), lambda i: (l, 0, 0)),
    ]
    out_specs, out_shape, scratch = [xspec], [jax.ShapeDtypeStruct((T, D), f32)], []
    if xs is not None:
        N = xs.shape[0]
        operands.append(xs)
        in_specs.append(_full((N, D)))
        out_specs.append(_full((N, D)))
        out_shape.append(jax.ShapeDtypeStruct((N, D), f32))
    if attend is not None:
        ck, cv, wq, wo = attend
        assert N % steps == 0
        bs = N // steps
        assert bs % SUBLANES == 0 or SUBLANES % bs == 0
        cspec = pl.BlockSpec((None, bs, N_MEM, XA_HEADS, XA_HEAD_DIM), lambda i: (l, i, 0, 0, 0))
        wspec = pl.BlockSpec((None, D, D), lambda i: (l, 0, 0), pipeline_mode=once)
        operands += [ck, cv, wq, wo]
        in_specs += [cspec, cspec, wspec, wspec]
        scratch = [pltpu.VMEM((N, D), f32), pltpu.VMEM((N, D), f32)]
    res = pl.pallas_call(
        functools.partial(_ffn_kernel, ln_row=(nln - 1) * half, rows=rows, tf=tf, samples=xs is not None,
                          attend=attend is not None),
        grid=(steps,),
        in_specs=in_specs,
        out_specs=out_specs,
        out_shape=out_shape,
        scratch_shapes=scratch,
        compiler_params=_cparams("parallel" if xs is None else "arbitrary"),
        name="ffn_ln",
    )(*operands)
    return res[0] if xs is None else res


def _memkv_kernel(m_ref, wk_ref, wv_ref, k_ref, v_ref, kh_ref, vh_ref):
    nb = m_ref.shape[0]
    for b in range(nb):
        mb = m_ref[b].astype(bf16)
        k = jnp.dot(mb, wk_ref[...], preferred_element_type=f32)
        v = jnp.dot(mb, wv_ref[...], preferred_element_type=f32)
        for h in range(XA_HEADS):
            sl = slice(h * XA_HEAD_DIM, (h + 1) * XA_HEAD_DIM)
            k_ref[b, :, h, :] = k[:, sl]
            v_ref[b, :, h, :] = v[:, sl]
            kh_ref[b, h] = k[:, sl].astype(bf16)
            vh_ref[b, h] = v[:, sl].astype(bf16)


def _mem_kv(mem, wk, wv, *, nb=2):
    B, M, D = mem.shape
    nb = min(nb, B)
    L = wk.shape[0]
    out = jax.ShapeDtypeStruct((L, B, M, XA_HEADS, XA_HEAD_DIM), f32)
    outh = jax.ShapeDtypeStruct((L, B, XA_HEADS, M, XA_HEAD_DIM), bf16)
    ospec = pl.BlockSpec((None, nb, M, XA_HEADS, XA_HEAD_DIM), lambda l, i: (l, i, 0, 0, 0))
    hspec = pl.BlockSpec((None, nb, XA_HEADS, M, XA_HEAD_DIM), lambda l, i: (l, i, 0, 0, 0))
    return pl.pallas_call(
        _memkv_kernel,
        grid=(L, B // nb),
        in_specs=[
            pl.BlockSpec((nb, M, D), lambda l, i: (i, 0, 0)),
            pl.BlockSpec((None, D, D), lambda l, i: (l, 0, 0)),
            pl.BlockSpec((None, D, D), lambda l, i: (l, 0, 0)),
        ],
        out_specs=[ospec, ospec, hspec, hspec],
        out_shape=[out, out, outh, outh],
        compiler_params=_cparams("parallel", "parallel"),
        name="mem_kv",
    )(mem, wk, wv)


def _xattn_kernel(x_ref, k_ref, v_ref, wq_ref, wo_ref, g_ref, b_ref, o_ref, oh_ref, *, rows):
    starts = list(range(0, x_ref.shape[0], rows))
    proj = lambda r: jnp.dot(x_ref[r:r + rows, :].astype(bf16), wq_ref[...], preferred_element_type=f32).astype(bf16)
    nxt = proj(starts[0])
    for i, r0 in enumerate(starts):
        rs = slice(r0, r0 + rows)
        x = x_ref[rs, :]
        q = nxt
        if i + 1 < len(starts):
            nxt = proj(starts[i + 1])
        for h in range(XA_HEADS):
            sl = slice(h * XA_HEAD_DIM, (h + 1) * XA_HEAD_DIM)
            s = _dot_nt(q[:, sl], k_ref[h]) * (XA_HEAD_DIM ** -0.5)
            s = s - jnp.max(s, -1, keepdims=True)
            e = jnp.exp(s)
            p = e / jnp.sum(e, -1, keepdims=True)
            oh_ref[rs, sl] = jnp.dot(p.astype(bf16), v_ref[h], preferred_element_type=f32).astype(bf16)
        att = jnp.dot(oh_ref[rs, :], wo_ref[...], preferred_element_type=f32)
        o_ref[rs, :] = _ln(DN_ALPHA * x + att, g_ref[XA_LN_ROW:XA_LN_ROW + 1, :], b_ref[XA_LN_ROW:XA_LN_ROW + 1, :])


def _layer_spec(shape, l):
    n = len(shape)
    return pl.BlockSpec((None,) + tuple(shape[1:]), lambda *_: (l,) + (0,) * (n - 1))


def _xattn_ln(x, mk, mv, wq, wo, g, b, *, l, seq, tq=1024, rows=512):
    T, D = x.shape
    tq = min(tq, seq)
    rows = min(rows, tq)
    nq = seq // tq
    mspec = pl.BlockSpec((None, None, XA_HEADS, N_MEM, XA_HEAD_DIM), lambda i: (l, i // nq, 0, 0, 0))
    return pl.pallas_call(
        functools.partial(_xattn_kernel, rows=rows),
        grid=(T // tq,),
        in_specs=[
            pl.BlockSpec((tq, D), lambda i: (i, 0)),
            mspec, mspec,
            _layer_spec(wq.shape, l), _layer_spec(wo.shape, l), _layer_spec(g.shape, l), _layer_spec(b.shape, l),
        ],
        out_specs=pl.BlockSpec((tq, D), lambda i: (i, 0)),
        out_shape=jax.ShapeDtypeStruct((T, D), f32),
        scratch_shapes=[pltpu.VMEM((tq, D), bf16)],
        compiler_params=_cparams("parallel"),
        name="xattn_ln",
    )(x, mk, mv, wq, wo, g, b)


def _gmlp_kernel(x_ref, win_ref, ws_ref, bs_ref, vg_ref, vb_ref, wout_ref, g_ref, b_ref, o_ref, *rest, single, rows):
    v_ref, uf_ref = rest if single else (None, rest[0])
    gw = GM_WIDTH // GM_GROUPS
    if not single:
        row = lax.broadcasted_iota(jnp.int32, (GM_CHUNK, GM_CHUNK), 0)
        col = lax.broadcasted_iota(jnp.int32, (GM_CHUNK, GM_CHUNK), 1)
        wmask = [jnp.where(col <= row, ws_ref[g], 0.0).astype(bf16) for g in range(GM_GROUPS)]
    starts = list(range(0, x_ref.shape[0], rows))
    proj = lambda r: jnp.dot(x_ref[r:r + rows, :].astype(bf16), win_ref[...], preferred_element_type=f32)
    nxt = proj(starts[0])
    for i, r0 in enumerate(starts):
        x = x_ref[r0:r0 + rows, :]
        cur = nxt
        if i + 1 < len(starts):
            nxt = proj(starts[i + 1])
        pr = jax.nn.gelu(cur)
        u = pr[:, :GM_WIDTH]
        v = _ln(pr[:, GM_WIDTH:], vg_ref[...], vb_ref[...])
        if single:
            v_ref[r0:r0 + rows, :] = v
            for g in range(GM_GROUPS):
                sl = slice(g * gw, (g + 1) * gw)
                f = ws_ref[g][0:1, 0:1] * v[:, sl] + bs_ref[g][0:1, 0:1]
                uf_ref[r0:r0 + rows, sl] = (u[:, sl] * f).astype(bf16)
        else:
            vb16 = v.astype(bf16)
            for g in range(GM_GROUPS):
                sl = slice(g * gw, (g + 1) * gw)
                for c in range(rows // GM_CHUNK):
                    cs = slice(c * GM_CHUNK, (c + 1) * GM_CHUNK)
                    f = jnp.dot(wmask[g], vb16[cs, sl], preferred_element_type=f32) + bs_ref[g]
                    uf_ref[r0 + c * GM_CHUNK:r0 + (c + 1) * GM_CHUNK, sl] = (u[cs, sl] * f).astype(bf16)
        y = jnp.dot(uf_ref[r0:r0 + rows, :], wout_ref[...], preferred_element_type=f32)
        o_ref[r0:r0 + rows, :] = _ln(DN_ALPHA * x + y, g_ref[MIX_LN_ROW:MIX_LN_ROW + 1, :], b_ref[MIX_LN_ROW:MIX_LN_ROW + 1, :])


def _gmlp_ln(x, w_in, w_s, b_s, vg, vb, w_out, g, b, *, l, tm, single, rows=512):
    T, D = x.shape
    tm = min(tm, T)
    rows = min(rows, tm)
    xspec = pl.BlockSpec((tm, D), lambda i: (i, 0))
    out = jax.ShapeDtypeStruct((T, D), f32)
    res = pl.pallas_call(
        functools.partial(_gmlp_kernel, single=single, rows=rows),
        grid=(T // tm,),
        in_specs=[
            xspec, _full((D, 2 * GM_WIDTH)), _full(w_s.shape), _full(b_s.shape),
            _full((1, GM_WIDTH)), _full((1, GM_WIDTH)), _full((GM_WIDTH, D)), _layer_spec(g.shape, l), _layer_spec(b.shape, l),
        ],
        out_specs=[xspec, pl.BlockSpec((tm, GM_WIDTH), lambda i: (i, 0))] if single else xspec,
        out_shape=[out, jax.ShapeDtypeStruct((T, GM_WIDTH), f32)] if single else out,
        scratch_shapes=[pltpu.VMEM((tm, GM_WIDTH), bf16)],
        compiler_params=_cparams("parallel"),
        name="gmlp_ln",
    )(x, w_in, w_s, b_s, vg, vb, w_out, g, b)
    return res if single else (res, None)


def _gdn_gates(ba, alog, dtb):
    beta = jax.nn.sigmoid(ba[:, :LANES])
    g = -jnp.exp(alog) * _softplus(ba[:, LANES:] + dtb)
    return beta, g


def _l2n(x):
    return x * lax.rsqrt(jnp.sum(x * x, -1, keepdims=True) + 1e-6)


def _mixer0_kernel(x_ref, wqz_ref, wglu_ref, wba_ref, dnw_ref, alog_ref, dtb_ref, ng_ref, ccw_ref, ccb_ref, cclg_ref, cclb_ref,
                   wout_ref, g_ref, b_ref,
                   o_ref, s_out_ref, dnc_out_ref, ccc_out_ref,
                   qkv_ext, glu_ext, rot_scr, s_scr, oc_scr):
    blk = pl.program_id(1)
    tb = x_ref.shape[0]
    C = GDN_BLOCK

    @pl.when(blk == 0)
    def _():
        qkv_ext[0:DN_TAIL, :] = jnp.zeros((DN_TAIL, 3 * GDN_WIDTH), f32)
        glu_ext[0:CC_TAIL, :] = jnp.zeros((CC_TAIL, CC_CH), f32)
        s_scr[...] = jnp.zeros_like(s_scr)

    x = x_ref[...]
    xb = x.astype(bf16)
    proj = jnp.concatenate([jnp.dot(xb, wqz_ref[...], preferred_element_type=f32),
                            jnp.dot(xb, wglu_ref[...], preferred_element_type=f32)], axis=1)
    ba = jnp.dot(xb, wba_ref[...], preferred_element_type=f32)
    nq = 3 * GDN_WIDTH

    qkv_ext[DN_TAIL:DN_TAIL + tb, :] = proj[:, :nq]
    blocks = []
    for i in range(tb // DN_ROWS):
        acc = None
        for s in range(SCONV_W):
            term = dnw_ref[SCONV_W - 1 - s:SCONV_W - s, :] * qkv_ext[pl.ds(DN_TAIL - s + i * DN_ROWS, DN_ROWS), :]
            acc = term if acc is None else acc + term
        blocks.append(_silu(acc))
    qkv = jnp.concatenate(blocks, axis=0)
    tail = qkv_ext[tb:tb + DN_TAIL, :]
    qkv_ext[0:DN_TAIL, :] = tail
    dnc_out_ref[0] = tail

    beta, g = _gdn_gates(ba, alog_ref[...], dtb_ref[...])

    P = 2 * GDN_DK
    pairs = range(GDN_HEADS // 2)
    chunks = range(tb // C)
    row = lax.broadcasted_iota(jnp.int32, (C, P), 0)
    col = lax.broadcasted_iota(jnp.int32, (C, P), 1)
    col = jnp.where(col >= C, col - C, col)
    causal = col <= row
    strict = col < row
    ltri = jnp.where(causal[:, :C], 1.0, 0.0).astype(bf16)
    qn = [_l2n(qkv[:, h * GDN_DK:(h + 1) * GDN_DK]) * (GDN_DK ** -0.5) for h in range(GDN_HEADS)]
    kn = [_l2n(qkv[:, GDN_WIDTH + h * GDN_DK:GDN_WIDTH + (h + 1) * GDN_DK]) for h in range(GDN_HEADS)]

    def pair_cols(m, h0):
        return jnp.concatenate([jnp.broadcast_to(m[:, h0:h0 + 1], (C, GDN_DK)),
                                jnp.broadcast_to(m[:, h0 + 1:h0 + 2], (C, GDN_DK))], axis=1)

    def bdiag(m):
        z = jnp.zeros((C, C), m.dtype)
        return jnp.concatenate([jnp.concatenate([m[:, :C], z], axis=1), jnp.concatenate([z, m[:, C:]], axis=1)], axis=0)

    def split2(a):
        a1 = a.astype(bf16)
        return a1, (a - a1.astype(f32)).astype(bf16)

    mm = lambda a, b_: jnp.dot(a, b_, preferred_element_type=f32)

    prob = {}
    for c in chunks:
        rs = slice(c * C, (c + 1) * C)
        gc = _dot_exact_lhs(ltri, g[rs])
        gct = gc.T
        eg = jnp.exp(gc)
        g_last = gc[C - 1:C, :]
        ekt = jnp.exp(g_last - gc)
        egl = jnp.exp(g_last)
        for pr in pairs:
            h0 = 2 * pr
            k_pair = jnp.concatenate([kn[h0][rs], kn[h0 + 1][rs]], axis=1)
            q_pair = jnp.concatenate([qn[h0][rs], qn[h0 + 1][rs]], axis=1)
            v_pair = qkv[rs, 2 * GDN_WIDTH + pr * P:2 * GDN_WIDTH + (pr + 1) * P]
            beta_pair = pair_cols(beta[rs], h0)
            grow = jnp.concatenate([gct[h0:h0 + 1, :], gct[h0 + 1:h0 + 2, :]], axis=1)
            decay = jnp.where(causal, jnp.exp(jnp.where(causal, pair_cols(gc, h0) - grow, 0.0)), 0.0)
            kb = k_pair * beta_pair
            prod = _dot_nt(jnp.concatenate([kb, q_pair], axis=0), bdiag(k_pair.astype(bf16)))
            eg_pair = pair_cols(eg, h0)
            kt = k_pair * pair_cols(ekt, h0)
            prob[c, pr] = dict(
                n=-jnp.where(strict, prod[:C] * decay, 0.0),
                qk=(prod[C:] * decay).astype(bf16),
                vb=(v_pair * beta_pair).astype(bf16),
                kbe=(kb * eg_pair).astype(bf16),
                qg=(q_pair * eg_pair).astype(bf16),
                ktt=jnp.concatenate([kt[:, :C].T, kt[:, C:].T], axis=1).astype(bf16),
                egl=jnp.concatenate([jnp.broadcast_to(egl[:, h0:h0 + 1], (1, GDN_DK)),
                                     jnp.broadcast_to(egl[:, h0 + 1:h0 + 2], (1, GDN_DK))], axis=1))

    Hc = C // 2
    lane = lax.broadcasted_iota(jnp.int32, (Hc, P), 1)
    prow = lax.broadcasted_iota(jnp.int32, (Hc, P), 0)
    first_half = jnp.where(lane >= C, lane - C, lane) < Hc
    eye4 = jnp.where((lane & (Hc - 1)) == prow, 1.0, 0.0)
    quarter = [(lane >= q * Hc) & (lane < (q + 1) * Hc) for q in range(P // Hc)]

    def bdiag4(m):
        return jnp.concatenate([jnp.where(qm, m, jnp.zeros_like(m)) for qm in quarter], axis=0)

    def dot3(l1, l2, w1, w2):
        r = l1.shape[0]
        o = mm(jnp.concatenate([l1, l2], axis=0), w1)
        return o[:r] + (o[r:] + mm(l1, w2))

    nk = {key: jnp.where(first_half, pb["n"][:Hc], pb["n"][Hc:]) for key, pb in prob.items()}
    pk = {key: eye4 + nk[key] for key in prob}
    levels = Hc.bit_length() - 1
    for j in range(levels):
        first, last = j == 0, j == levels - 1
        for key in prob:
            n1, n2 = split2(nk[key])
            if first:
                l1, l2 = n1, n2
            else:
                p1, p2 = split2(pk[key])
                l1 = p1 if last else jnp.concatenate([n1, p1], axis=0)
                l2 = p2 if last else jnp.concatenate([n2, p2], axis=0)
            res = dot3(l1, l2, bdiag4(n1), bdiag4(n2))
            if first:
                nk[key] = res
            elif last:
                pk[key] = pk[key] + res
            else:
                nk[key] = res[:Hc]
                pk[key] = pk[key] + res[Hc:]
    crow = lax.broadcasted_iota(jnp.int32, (C, P), 0)
    c_block = (crow >= Hc) & (col < Hc)
    for key, pb in prob.items():
        t_pan = pk[key]
        t1, t2 = split2(t_pan)
        c1, c2 = split2(jnp.where(c_block, pb["n"], 0.0))
        x1, x2 = split2(dot3(t1, t2, bdiag(c1), bdiag(c2)))
        zero = jnp.zeros_like(t1)
        low = dot3(x1, x2, bdiag4(jnp.where(first_half, t1, zero)), bdiag4(jnp.where(first_half, t2, zero)))
        pk[key] = jnp.concatenate([jnp.where(first_half, t_pan, 0.0), low + jnp.where(first_half, 0.0, t_pan)], axis=0)

    for key, pb in prob.items():
        t16 = pk[key].astype(bf16)
        pb["u"] = mm(t16, bdiag(pb["vb"]))
        pb["w"] = mm(t16, bdiag(pb["kbe"])).astype(bf16)

    for c in chunks:
        rs = slice(c * C, (c + 1) * C)
        for pr in pairs:
            pb = prob[c, pr]
            s_pair = s_scr[pr]
            o2 = mm(jnp.concatenate([pb["w"], pb["qg"]], axis=0), bdiag(s_pair.astype(bf16)))
            vbd = bdiag((pb["u"] - o2[:C]).astype(bf16))
            o = o2[C:] + mm(pb["qk"], vbd)
            s_scr[pr] = s_pair * pb["egl"] + mm(pb["ktt"], vbd)
            on = [o[:, j * GDN_DV:(j + 1) * GDN_DV] for j in range(2)]
            on = [t * lax.rsqrt(jnp.mean(t * t, -1, keepdims=True) + 1e-6) * ng_ref[...] for t in on]
            z_pair = proj[rs, nq + pr * P:nq + (pr + 1) * P]
            oc_scr[rs, pr * P:(pr + 1) * P] = (jnp.concatenate(on, axis=1) * _silu(z_pair)).astype(bf16)

    for pr in pairs:
        s_out_ref[0, 2 * pr] = s_scr[pr][:, :GDN_DV]
        s_out_ref[0, 2 * pr + 1] = s_scr[pr][:, GDN_DV:]

    ga = proj[:, nq + GDN_WIDTH:nq + GDN_WIDTH + CC_CH]
    gb = proj[:, nq + GDN_WIDTH + CC_CH:]
    glu = ga * jax.nn.sigmoid(gb)
    glu_ext[CC_TAIL:CC_TAIL + tb, :] = glu
    span = CC_TAIL - SUBLANES
    for r in range(1, SUBLANES):
        rot_scr[r - 1] = glu_ext[pl.ds(SUBLANES - r, tb + span), :]

    for base in range(0, tb, CC_ROWS):
        acc = None
        for s in range(CC_W):
            a, r = divmod(s, SUBLANES)
            off = span - SUBLANES * a
            if r == 0:
                xs = glu_ext[base + SUBLANES + off:base + SUBLANES + off + CC_ROWS, :]
            else:
                xs = rot_scr[r - 1, base + off:base + off + CC_ROWS, :]
            term = ccw_ref[CC_W - 1 - s:CC_W - s, :] * xs
            acc = term if acc is None else acc + term
        cc = _silu(_ln(acc + ccb_ref[...], cclg_ref[...], cclb_ref[...]))
        oc_scr[base:base + CC_ROWS, GDN_WIDTH:] = cc.astype(bf16)
    tail = glu_ext[tb:tb + CC_TAIL, :]
    glu_ext[0:CC_TAIL, :] = tail
    ccc_out_ref[0] = tail

    y = jnp.dot(oc_scr[...], wout_ref[...], preferred_element_type=f32)
    o_ref[...] = _ln(DN_ALPHA * x + y, g_ref[MIX_LN_ROW:MIX_LN_ROW + 1, :], b_ref[MIX_LN_ROW:MIX_LN_ROW + 1, :])


def _mixer0_ln(x, wqz, wglu, wba, dnw, alog, dtb, ng, ccw, ccb, cclg, cclb, wout, g, b, *, l, batch, seq, tb=512):
    T, D = x.shape
    tb = min(tb, seq)
    nb = seq // tb
    consts = [wqz, wglu, wba, dnw, alog, dtb, ng, ccw, ccb, cclg, cclb, wout]
    return pl.pallas_call(
        _mixer0_kernel,
        grid=(batch, nb),
        in_specs=[pl.BlockSpec((tb, D), lambda i, j: (i * nb + j, 0))] + [_full(c.shape) for c in consts]
        + [_layer_spec(g.shape, l), _layer_spec(b.shape, l)],
        out_specs=[
            pl.BlockSpec((tb, D), lambda i, j: (i * nb + j, 0)),
            pl.BlockSpec((1, GDN_HEADS, GDN_DK, GDN_DV), lambda i, j: (i, 0, 0, 0)),
            pl.BlockSpec((1, DN_TAIL, 3 * GDN_WIDTH), lambda i, j: (i, 0, 0)),
            pl.BlockSpec((1, CC_TAIL, CC_CH), lambda i, j: (i, 0, 0)),
        ],
        out_shape=[
            jax.ShapeDtypeStruct((T, D), f32),
            jax.ShapeDtypeStruct((batch, GDN_HEADS, GDN_DK, GDN_DV), f32),
            jax.ShapeDtypeStruct((batch, DN_TAIL, 3 * GDN_WIDTH), f32),
            jax.ShapeDtypeStruct((batch, CC_TAIL, CC_CH), f32),
        ],
        scratch_shapes=[
            pltpu.VMEM((tb + DN_TAIL, 3 * GDN_WIDTH), f32),
            pltpu.VMEM((tb + CC_TAIL, CC_CH), f32),
            pltpu.VMEM((SUBLANES - 1, tb + CC_TAIL - SUBLANES, CC_CH), f32),
            pltpu.VMEM((GDN_HEADS // 2, GDN_DK, 2 * GDN_DV), f32),
            pltpu.VMEM((tb, GDN_WIDTH + CC_CH), bf16),
        ],
        compiler_params=_cparams("parallel", "arbitrary"),
        name="mixer0_ln",
    )(x, *consts, g, b)


def _mixer0_dec_kernel(x_ref, s_ref, dnc_ref, ccc_ref, wqz_ref, wglu_ref, wba_ref, dnw_ref, alog_ref, dtb_ref, ng_ref, ccw_ref,
                       ccb_ref, cclg_ref, cclb_ref, wout_ref, g_ref, b_ref,
                       o_ref, s_out_ref, dnc_out_ref, ccc_out_ref,
                       q_scr, k_scr, v_scr, z_scr, beta_scr, eg_scr, oc_scr, *, bs):
    i = pl.program_id(0)
    nq = 3 * GDN_WIDTH

    @pl.when(i == 0)
    def _():
        xb = x_ref[...].astype(bf16)
        proj = jnp.concatenate([jnp.dot(xb, wqz_ref[...], preferred_element_type=f32),
                                jnp.dot(xb, wglu_ref[...], preferred_element_type=f32)], axis=1)
        ba = jnp.dot(xb, wba_ref[...], preferred_element_type=f32)
        qkv_raw = proj[:, :nq]
        acc = dnw_ref[SCONV_W - 1:SCONV_W, :] * qkv_raw
        for j in range(SCONV_W - 1):
            acc = acc + dnw_ref[j:j + 1, :] * dnc_ref[j]
        for j in range(SCONV_W - 2):
            dnc_out_ref[j] = dnc_ref[j + 1]
        dnc_out_ref[SCONV_W - 2] = qkv_raw
        qkv = _silu(acc)
        for h in range(GDN_HEADS):
            hs = slice(h * GDN_DK, (h + 1) * GDN_DK)
            q_scr[:, hs] = _l2n(qkv[:, h * GDN_DK:(h + 1) * GDN_DK]) * (GDN_DK ** -0.5)
            k_scr[:, hs] = _l2n(qkv[:, GDN_WIDTH + h * GDN_DK:GDN_WIDTH + (h + 1) * GDN_DK])
        v_scr[...] = qkv[:, 2 * GDN_WIDTH:]
        z_scr[...] = _silu(proj[:, nq:nq + GDN_WIDTH])
        beta, g = _gdn_gates(ba, alog_ref[...], dtb_ref[...])
        beta_scr[...] = beta
        eg_scr[...] = jnp.exp(g)

        ga = proj[:, nq + GDN_WIDTH:nq + GDN_WIDTH + CC_CH]
        gb = proj[:, nq + GDN_WIDTH + CC_CH:]
        glu = ga * jax.nn.sigmoid(gb)
        acc = ccw_ref[CC_W - 1:CC_W, :] * glu
        for j in range(CC_W - 1):
            acc = acc + ccw_ref[j:j + 1, :] * ccc_ref[j]
        for j in range(CC_W - 2):
            ccc_out_ref[j] = ccc_ref[j + 1]
        ccc_out_ref[CC_W - 2] = glu
        cc = _silu(_ln(acc + ccb_ref[...], cclg_ref[...], cclb_ref[...]))
        oc_scr[:, GDN_WIDTH:] = cc

    rows = pl.ds(pl.multiple_of(i * bs, bs), bs)
    q_blk, k_blk, v_blk, z_blk = q_scr[rows, :], k_scr[rows, :], v_scr[rows, :], z_scr[rows, :]
    beta_blk, eg_blk = beta_scr[rows, :], eg_scr[rows, :]
    o_rows = []
    for s in range(bs):
        o_heads = []
        for h in range(GDN_HEADS):
            hs = slice(h * GDN_DK, (h + 1) * GDN_DK)
            k_col = jnp.broadcast_to(k_blk[s:s + 1, hs], (GDN_DK, GDN_DK)).T
            q_col = jnp.broadcast_to(q_blk[s:s + 1, hs], (GDN_DK, GDN_DK)).T
            b1 = beta_blk[s:s + 1, h:h + 1]
            e1 = eg_blk[s:s + 1, h:h + 1]
            s_old = s_ref[s, h]
            ks = jnp.sum(k_col * s_old, 0, keepdims=True)
            v_new = b1 * (v_blk[s:s + 1, hs] - e1 * ks)
            s_new = s_old * e1 + k_col * v_new
            s_out_ref[s, h] = s_new
            o = jnp.sum(q_col * s_new, 0, keepdims=True)
            o = o * lax.rsqrt(jnp.mean(o * o, -1, keepdims=True) + 1e-6) * ng_ref[...]
            o_heads.append(o * z_blk[s:s + 1, hs])
        o_rows.append(jnp.concatenate(o_heads, axis=-1))
    oc_scr[rows, :GDN_WIDTH] = jnp.concatenate(o_rows, axis=0)

    @pl.when(i == pl.num_programs(0) - 1)
    def _():
        y = jnp.dot(oc_scr[...].astype(bf16), wout_ref[...], preferred_element_type=f32)
        o_ref[...] = _ln(DN_ALPHA * x_ref[...] + y, g_ref[MIX_LN_ROW:MIX_LN_ROW + 1, :], b_ref[MIX_LN_ROW:MIX_LN_ROW + 1, :])


def _mixer0_dec_ln(x, s, dnc, ccc, wqz, wglu, wba, dnw, alog, dtb, ng, ccw, ccb, cclg, cclb, wout, g, b, *, l, bs=8):
    N, D = x.shape
    consts = [wqz, wglu, wba, dnw, alog, dtb, ng, ccw, ccb, cclg, cclb, wout]
    sspec = pl.BlockSpec((bs, GDN_HEADS, GDN_DK, GDN_DV), lambda i: (i, 0, 0, 0))
    return pl.pallas_call(
        functools.partial(_mixer0_dec_kernel, bs=bs),
        grid=(N // bs,),
        in_specs=[_full((N, D)), sspec, _full(dnc.shape), _full(ccc.shape)] + [_full(c.shape) for c in consts]
        + [_layer_spec(g.shape, l), _layer_spec(b.shape, l)],
        out_specs=[_full((N, D)), sspec, _full(dnc.shape), _full(ccc.shape)],
        out_shape=[jax.ShapeDtypeStruct((N, D), f32), jax.ShapeDtypeStruct(s.shape, f32),
                   jax.ShapeDtypeStruct(dnc.shape, f32), jax.ShapeDtypeStruct(ccc.shape, f32)],
        scratch_shapes=[pltpu.VMEM((N, GDN_WIDTH), f32)] * 4 + [pltpu.VMEM((N, LANES), f32)] * 2
        + [pltpu.VMEM((N, GDN_WIDTH + CC_CH), f32)],
        compiler_params=_cparams("arbitrary"),
        name="mixer0_dec_ln",
    )(x, s, dnc, ccc, *consts, g, b)


def _pad_lanes(v, n=LANES):
    return jnp.zeros((1, n), f32).at[0, :v.shape[0]].set(v.astype(f32))


def kernel(x_prompt, x_sample, mem_prompt, cache_mem_k, cache_mem_v, state_dn_S, state_dn_conv, state_cc_conv, ln_g, ln_b, ffn_w_gate, ffn_w_up, ffn_w_down, xa_wq, xa_wk, xa_wv, xa_wo, ab_w_in, dn_conv_w, dn_A_log, dn_dt_bias, dn_norm_g, cc_conv_w, cc_conv_b, cc_ln_g, cc_ln_b, ab_w_out, gm_w_in, gm_ln_g, gm_ln_b, gm_w_s, gm_b_s, gm_w_out):
    B, SEQ, D = x_prompt.shape
    N = x_sample.shape[0]
    row = lambda v: v.reshape(1, -1).astype(f32)

    wg, wu, wd = ffn_w_gate.astype(bf16), ffn_w_up.astype(bf16), ffn_w_down.astype(bf16)
    wq, wk, wv, wo = xa_wq.astype(bf16), xa_wk.astype(bf16), xa_wv.astype(bf16), xa_wo.astype(bf16)
    nq = 3 * GDN_WIDTH
    nz = nq + GDN_WIDTH
    w_qz, w_glu = ab_w_in[:, :nz].astype(bf16), ab_w_in[:, nz + 2 * GDN_HEADS:].astype(bf16)
    w_ba = jnp.zeros((D, 2 * LANES), f32)
    w_ba = w_ba.at[:, :GDN_HEADS].set(ab_w_in[:, nz:nz + GDN_HEADS])
    w_ba = w_ba.at[:, LANES:LANES + GDN_HEADS].set(ab_w_in[:, nz + GDN_HEADS:nz + 2 * GDN_HEADS]).astype(bf16)
    w_out = ab_w_out.astype(bf16)
    gw_in, gw_out = gm_w_in.astype(bf16), gm_w_out.astype(bf16)
    mixer_consts = (w_qz, w_glu, w_ba, dn_conv_w.astype(f32), _pad_lanes(dn_A_log), _pad_lanes(dn_dt_bias), row(dn_norm_g),
                    cc_conv_w.astype(f32), row(cc_conv_b), row(cc_ln_g), row(cc_ln_b), w_out)
    gm_consts = (gw_in, gm_w_s.astype(f32), gm_b_s.astype(f32)[:, :, None], row(gm_ln_g), row(gm_ln_b), gw_out)
    lng, lnb = ln_g.astype(f32), ln_b.astype(f32)

    mem_k, mem_v, mem_kh, mem_vh = _mem_kv(mem_prompt, wk, wv)

    yp, ys = x_prompt.reshape(B * SEQ, D), x_sample.reshape(N, D)
    state = (state_dn_S, jnp.swapaxes(state_dn_conv, 0, 1), jnp.swapaxes(state_cc_conv, 0, 1))
    for l in range(DEPTH):
        yp, ys = _ffn_ln(yp, wg, wu, wd, lng, lnb, l=l, half=0, tm=FFN_ROWS, xs=ys)
        if l % 2 == 0:
            yp, dn_s_p, dn_c_p, cc_c_p = _mixer0_ln(yp, *mixer_consts, lng, lnb, l=l, batch=B, seq=SEQ)
            ys, dn_s_s, dn_c_s, cc_c_s = _mixer0_dec_ln(ys, *state, *mixer_consts, lng, lnb, l=l)
        else:
            yp, _ = _gmlp_ln(yp, *gm_consts, lng, lnb, l=l, tm=FFN_ROWS, single=False)
            ys, gm_v_s = _gmlp_ln(ys, *gm_consts, lng, lnb, l=l, tm=N, single=True)
        yp = _xattn_ln(yp, mem_kh, mem_vh, wq, wo, lng, lnb, l=l, seq=SEQ)
        yp, ys = _ffn_ln(yp, wg, wu, wd, lng, lnb, l=l, half=1, tm=FFN_ATTEND_ROWS, xs=ys,
                         attend=(cache_mem_k, cache_mem_v, wq, wo))

    return (yp.reshape(B, SEQ, D), ys.reshape(N, 1, D), mem_k, mem_v,
            dn_s_p, dn_c_p[:, DN_TAIL - (SCONV_W - 1):], cc_c_p[:, CC_TAIL - (CC_W - 1):],
            dn_s_s, jnp.swapaxes(dn_c_s, 0, 1), jnp.swapaxes(cc_c_s, 0, 1), gm_v_s.reshape(N, 1, GM_WIDTH))
```

```python
import functools

import jax
import jax.numpy as jnp
from jax import lax
from jax.experimental import pallas as pl
from jax.experimental.pallas import tpu as pltpu

D_MODEL = 1024
DEPTH = 2
DN_ALPHA = (2 * DEPTH) ** 0.25
LN_EPS = 1e-5
GDN_HEADS = 4
GDN_DK = 128
GDN_DV = 128
GDN_WIDTH = GDN_HEADS * GDN_DK
SCONV_W = 4
CC_CH = D_MODEL // 2
CC_W = 31
GM_WIDTH = D_MODEL
GM_GROUPS = 4
GM_CHUNK = 128
N_MEM = 256
XA_HEADS = 4
XA_HEAD_DIM = D_MODEL // XA_HEADS
D_FF = 2816
MIX_LN_ROW = 1
XA_LN_ROW = 2

LANES = 128
SUBLANES = 8
GDN_BLOCK = 128
DN_TAIL = SUBLANES
CC_TAIL = 32
CC_ROWS = 32
DN_ROWS = 16
FFN_ROWS = 1024
GMLP_ROWS = 1024
FFN_ATTEND_ROWS = 512
VMEM_LIMIT = 56 * 1024 * 1024

bf16 = jnp.bfloat16
f32 = jnp.float32


def _cparams(*sem):
    return pltpu.CompilerParams(dimension_semantics=sem, vmem_limit_bytes=VMEM_LIMIT)


def _full(shape):
    n = len(shape)
    return pl.BlockSpec(shape, lambda *_: (0,) * n)


def _dot_nt(a, b):
    return lax.dot_general(a.astype(bf16), b.astype(bf16), (((1,), (1,)), ((), ())), preferred_element_type=f32)


def _split3(a):
    a1 = a.astype(bf16)
    r = a - a1.astype(f32)
    a2 = r.astype(bf16)
    a3 = (r - a2.astype(f32)).astype(bf16)
    return a1, a2, a3


def _dot_exact_lhs(a_exact_bf16, b):
    b1, b2, b3 = _split3(b)
    d = lambda y: jnp.dot(a_exact_bf16, y, preferred_element_type=f32)
    return d(b1) + (d(b2) + d(b3))


def _ln(y, g, b):
    mu = jnp.mean(y, -1, keepdims=True)
    d = y - mu
    var = jnp.mean(d * d, -1, keepdims=True)
    return d * lax.rsqrt(var + LN_EPS) * g + b


def _silu(x):
    return x * jax.nn.sigmoid(x)


def _softplus(x):
    return jnp.maximum(x, 0.0) + jnp.log(1.0 + jnp.exp(-jnp.abs(x)))


def _decode_attend(q_rows, k_ref, v_ref):
    grp = SUBLANES // XA_HEADS
    o_rows = []
    for s in range(q_rows.shape[0]):
        q4 = jnp.concatenate([q_rows[s:s + 1, h * XA_HEAD_DIM:(h + 1) * XA_HEAD_DIM] for h in range(XA_HEADS)], axis=0)
        q8 = jnp.concatenate([q4] * grp, axis=0)
        both = lambda t: sum(t[j * XA_HEADS:(j + 1) * XA_HEADS] for j in range(grp))
        k3 = k_ref[s].reshape(N_MEM // grp, grp * XA_HEADS, XA_HEAD_DIM)
        v3 = v_ref[s].reshape(N_MEM // grp, grp * XA_HEADS, XA_HEAD_DIM)
        sc = jnp.sum(k3 * q8[None], -1, keepdims=True) * (XA_HEAD_DIM ** -0.5)
        mx = jnp.max(sc, 0)
        mx = functools.reduce(jnp.maximum, [mx[j * XA_HEADS:(j + 1) * XA_HEADS] for j in range(grp)])
        e = jnp.exp(sc - jnp.concatenate([mx] * grp, axis=0)[None])
        den = both(jnp.sum(e, 0))
        p = e / jnp.concatenate([den] * grp, axis=0)[None]
        o4 = both(jnp.sum(p * v3, 0))
        o_rows.append(jnp.concatenate([o4[h:h + 1, :] for h in range(XA_HEADS)], axis=-1))
    return jnp.concatenate(o_rows, axis=0)


def _ffn_kernel(*refs, ln_row, rows, tf, samples, attend):
    x_ref, wg_ref, wu_ref, wd_ref, g_ref, b_ref = refs[:6]
    pos = 6
    if samples:
        xs_ref = refs[pos]
        pos += 1
    if attend:
        ck_ref, cv_ref, wq_ref, wo_ref = refs[pos:pos + 4]
        pos += 4
    o_ref = refs[pos]
    pos += 1
    if samples:
        os_ref = refs[pos]
        pos += 1
    if attend:
        q_scr, a_scr = refs[pos:pos + 2]
    i = pl.program_id(0)
    r = slice(ln_row, ln_row + 1)
    xr = slice(XA_LN_ROW, XA_LN_ROW + 1)
    nf = wg_ref.shape[1] // tf
    norm = lambda y: _ln(y, g_ref[r, :], b_ref[r, :])

    def pre_norm(x):
        xb = x.astype(bf16)
        acc = None
        for j in range(nf):
            cs = slice(j * tf, (j + 1) * tf)
            hg = jnp.dot(xb, wg_ref[:, cs], preferred_element_type=f32)
            hu = jnp.dot(xb, wu_ref[:, cs], preferred_element_type=f32)
            d = jnp.dot((_silu(hg) * hu).astype(bf16), wd_ref[cs, :], preferred_element_type=f32)
            acc = d if acc is None else acc + d
        return DN_ALPHA * x + 0.5 * acc

    def attend_step():
        bs = ck_ref.shape[0]
        share = max(SUBLANES // bs, 1)
        nrow = bs * share
        blk = pl.ds(pl.multiple_of((i // share) * nrow, nrow), nrow)
        q_blk = q_scr[blk, :]
        sub = i % share
        q_rows = q_blk[:bs]
        for j in range(1, share):
            q_rows = jnp.where(sub == j, q_blk[j * bs:(j + 1) * bs], q_rows)
        o_rows = _decode_attend(q_rows, ck_ref, cv_ref)
        if share > 1:
            a_blk = a_scr[blk, :]
            o_rows = jnp.concatenate([jnp.where(sub == j, o_rows, a_blk[j * bs:(j + 1) * bs]) for j in range(share)], axis=0)
        a_scr[blk, :] = o_rows

    def sample_rows():
        xs = xs_ref[...]
        if attend:
            att = jnp.dot(a_scr[...].astype(bf16), wo_ref[...], preferred_element_type=f32)
            xs = _ln(DN_ALPHA * xs + att, g_ref[xr, :], b_ref[xr, :])
        os_ref[...] = norm(pre_norm(xs))

    if attend:
        @pl.when(i == 0)
        def _():
            q_scr[...] = jnp.dot(xs_ref[...].astype(bf16), wq_ref[...], preferred_element_type=f32)
            a_scr[...] = jnp.zeros_like(a_scr)

    for r0 in range(0, x_ref.shape[0], rows):
        o_ref[r0:r0 + rows, :] = norm(pre_norm(x_ref[r0:r0 + rows, :]))
    if attend:
        attend_step()
    if samples:
        pl.when(i == pl.num_programs(0) - 1)(sample_rows)


def _ffn_ln(x, wg, wu, wd, g, b, *, l, half, tm, rows=512, tf=256, xs=None, attend=None):
    T, D = x.shape
    tm = min(tm, T)
    rows = min(rows, tm)
    F = wg.shape[-1]
    nln = g.shape[1]
    steps = T // tm
    once = pl.Buffered(1)
    xspec = pl.BlockSpec((tm, D), lambda i: (i, 0))
    operands = [x, wg, wu, wd, g, b]
    in_specs = [
        xspec,
        pl.BlockSpec((None, None, D, F), lambda i: (l, half, 0, 0), pipeline_mode=once),
        pl.BlockSpec((None, None, D, F), lambda i: (l, half, 0, 0), pipeline_mode=once),
        pl.BlockSpec((None, None, F, D), lambda i: (l, half, 0, 0), pipeline_mode=once),
        pl.BlockSpec((None, nln, D), lambda i: (l, 0, 0)),
        pl.BlockSpec((None, nln, D), lambda i: (l, 0, 0)),
    ]
    out_specs, out_shape, scratch = [xspec], [jax.ShapeDtypeStruct((T, D), f32)], []
    if xs is not None:
        N = xs.shape[0]
        operands.append(xs)
        in_specs.append(_full((N, D)))
        out_specs.append(_full((N, D)))
        out_shape.append(jax.ShapeDtypeStruct((N, D), f32))
    if attend is not None:
        ck, cv, wq, wo = attend
        assert N % steps == 0
        bs = N // steps
        assert bs % SUBLANES == 0 or SUBLANES % bs == 0
        cspec = pl.BlockSpec((None, bs, N_MEM, XA_HEADS, XA_HEAD_DIM), lambda i: (l, i, 0, 0, 0))
        wspec = pl.BlockSpec((None, D, D), lambda i: (l, 0, 0), pipeline_mode=once)
        operands += [ck, cv, wq, wo]
        in_specs += [cspec, cspec, wspec, wspec]
        scratch = [pltpu.VMEM((N, D), f32), pltpu.VMEM((N, D), f32)]
    res = pl.pallas_call(
        functools.partial(_ffn_kernel, ln_row=(nln - 1) * half, rows=rows, tf=tf, samples=xs is not None,
                          attend=attend is not None),
        grid=(steps,),
        in_specs=in_specs,
        out_specs=out_specs,
        out_shape=out_shape,
        scratch_shapes=scratch,
        compiler_params=_cparams("parallel" if xs is None else "arbitrary"),
        name="ffn_ln",
    )(*operands)
    return res[0] if xs is None else res


def _memkv_kernel(m_ref, wk_ref, wv_ref, k_ref, v_ref, kh_ref, vh_ref):
    nb = m_ref.shape[0]
    for b in range(nb):
        mb = m_ref[b].astype(bf16)
        k = jnp.dot(mb, wk_ref[...], preferred_element_type=f32)
        v = jnp.dot(mb, wv_ref[...], preferred_element_type=f32)
        for h in range(XA_HEADS):
            sl = slice(h * XA_HEAD_DIM, (h + 1) * XA_HEAD_DIM)
            k_ref[b, :, h, :] = k[:, sl]
            v_ref[b, :, h, :] = v[:, sl]
            kh_ref[b, h] = k[:, sl].astype(bf16)
            vh_ref[b, h] = v[:, sl].astype(bf16)


def _mem_kv(mem, wk, wv, *, nb=2):
    B, M, D = mem.shape
    nb = min(nb, B)
    L = wk.shape[0]
    out = jax.ShapeDtypeStruct((L, B, M, XA_HEADS, XA_HEAD_DIM), f32)
    outh = jax.ShapeDtypeStruct((L, B, XA_HEADS, M, XA_HEAD_DIM), bf16)
    ospec = pl.BlockSpec((None, nb, M, XA_HEADS, XA_HEAD_DIM), lambda l, i: (l, i, 0, 0, 0))
    hspec = pl.BlockSpec((None, nb, XA_HEADS, M, XA_HEAD_DIM), lambda l, i: (l, i, 0, 0, 0))
    return pl.pallas_call(
        _memkv_kernel,
        grid=(L, B // nb),
        in_specs=[
            pl.BlockSpec((nb, M, D), lambda l, i: (i, 0, 0)),
            pl.BlockSpec((None, D, D), lambda l, i: (l, 0, 0)),
            pl.BlockSpec((None, D, D), lambda l, i: (l, 0, 0)),
        ],
        out_specs=[ospec, ospec, hspec, hspec],
        out_shape=[out, out, outh, outh],
        compiler_params=_cparams("parallel", "parallel"),
        name="mem_kv",
    )(mem, wk, wv)


def _xattn_kernel(x_ref, k_ref, v_ref, wq_ref, wo_ref, g_ref, b_ref, o_ref, oh_ref, *, rows):
    starts = list(range(0, x_ref.shape[0], rows))
    proj = lambda r: jnp.dot(x_ref[r:r + rows, :].astype(bf16), wq_ref[...], preferred_element_type=f32).astype(bf16)
    nxt = proj(starts[0])
    for i, r0 in enumerate(starts):
        rs = slice(r0, r0 + rows)
        x = x_ref[rs, :]
        q = nxt
        if i + 1 < len(starts):
            nxt = proj(starts[i + 1])
        for h in range(XA_HEADS):
            sl = slice(h * XA_HEAD_DIM, (h + 1) * XA_HEAD_DIM)
            s = _dot_nt(q[:, sl], k_ref[h]) * (XA_HEAD_DIM ** -0.5)
            s = s - jnp.max(s, -1, keepdims=True)
            e = jnp.exp(s)
            p = e / jnp.sum(e, -1, keepdims=True)
            oh_ref[rs, sl] = jnp.dot(p.astype(bf16), v_ref[h], preferred_element_type=f32).astype(bf16)
        att = jnp.dot(oh_ref[rs, :], wo_ref[...], preferred_element_type=f32)
        o_ref[rs, :] = _ln(DN_ALPHA * x + att, g_ref[XA_LN_ROW:XA_LN_ROW + 1, :], b_ref[XA_LN_ROW:XA_LN_ROW + 1, :])


def _layer_spec(shape, l):
    n = len(shape)
    return pl.BlockSpec((None,) + tuple(shape[1:]), lambda *_: (l,) + (0,) * (n - 1))


def _xattn_ln(x, mk, mv, wq, wo, g, b, *, l, seq, tq=2048, rows=512):
    T, D = x.shape
    tq = min(tq, seq)
    rows = min(rows, tq)
    nq = seq // tq
    mspec = pl.BlockSpec((None, None, XA_HEADS, N_MEM, XA_HEAD_DIM), lambda i: (l, i // nq, 0, 0, 0))
    return pl.pallas_call(
        functools.partial(_xattn_kernel, rows=rows),
        grid=(T // tq,),
        in_specs=[
            pl.BlockSpec((tq, D), lambda i: (i, 0)),
            mspec, mspec,
            _layer_spec(wq.shape, l), _layer_spec(wo.shape, l), _layer_spec(g.shape, l), _layer_spec(b.shape, l),
        ],
        out_specs=pl.BlockSpec((tq, D), lambda i: (i, 0)),
        out_shape=jax.ShapeDtypeStruct((T, D), f32),
        scratch_shapes=[pltpu.VMEM((tq, D), bf16)],
        compiler_params=_cparams("parallel"),
        name="xattn_ln",
    )(x, mk, mv, wq, wo, g, b)


def _gmlp_kernel(x_ref, win_ref, ws_ref, bs_ref, vg_ref, vb_ref, wout_ref, g_ref, b_ref, o_ref, *rest, single, rows):
    v_ref, uf_ref = rest if single else (None, rest[0])
    gw = GM_WIDTH // GM_GROUPS
    if not single:
        row = lax.broadcasted_iota(jnp.int32, (GM_CHUNK, GM_CHUNK), 0)
        col = lax.broadcasted_iota(jnp.int32, (GM_CHUNK, GM_CHUNK), 1)
        wmask = [jnp.where(col <= row, ws_ref[g], 0.0).astype(bf16) for g in range(GM_GROUPS)]
    starts = list(range(0, x_ref.shape[0], rows))
    proj = lambda r: jnp.dot(x_ref[r:r + rows, :].astype(bf16), win_ref[...], preferred_element_type=f32)
    nxt = proj(starts[0])
    for i, r0 in enumerate(starts):
        x = x_ref[r0:r0 + rows, :]
        cur = nxt
        if i + 1 < len(starts):
            nxt = proj(starts[i + 1])
        pr = jax.nn.gelu(cur)
        u = pr[:, :GM_WIDTH]
        v = _ln(pr[:, GM_WIDTH:], vg_ref[...], vb_ref[...])
        if single:
            v_ref[r0:r0 + rows, :] = v
            for g in range(GM_GROUPS):
                sl = slice(g * gw, (g + 1) * gw)
                f = ws_ref[g][0:1, 0:1] * v[:, sl] + bs_ref[g][0:1, 0:1]
                uf_ref[r0:r0 + rows, sl] = (u[:, sl] * f).astype(bf16)
        else:
            vb16 = v.astype(bf16)
            for g in range(GM_GROUPS):
                sl = slice(g * gw, (g + 1) * gw)
                for c in range(rows // GM_CHUNK):
                    cs = slice(c * GM_CHUNK, (c + 1) * GM_CHUNK)
                    f = jnp.dot(wmask[g], vb16[cs, sl], preferred_element_type=f32) + bs_ref[g]
                    uf_ref[r0 + c * GM_CHUNK:r0 + (c + 1) * GM_CHUNK, sl] = (u[cs, sl] * f).astype(bf16)
        y = jnp.dot(uf_ref[r0:r0 + rows, :], wout_ref[...], preferred_element_type=f32)
        o_ref[r0:r0 + rows, :] = _ln(DN_ALPHA * x + y, g_ref[MIX_LN_ROW:MIX_LN_ROW + 1, :], b_ref[MIX_LN_ROW:MIX_LN_ROW + 1, :])


def _gmlp_ln(x, w_in, w_s, b_s, vg, vb, w_out, g, b, *, l, tm, single, rows=512):
    T, D = x.shape
    tm = min(tm, T)
    rows = min(rows, tm)
    xspec = pl.BlockSpec((tm, D), lambda i: (i, 0))
    out = jax.ShapeDtypeStruct((T, D), f32)
    res = pl.pallas_call(
        functools.partial(_gmlp_kernel, single=single, rows=rows),
        grid=(T // tm,),
        in_specs=[
            xspec, _full((D, 2 * GM_WIDTH)), _full(w_s.shape), _full(b_s.shape),
            _full((1, GM_WIDTH)), _full((1, GM_WIDTH)), _full((GM_WIDTH, D)), _layer_spec(g.shape, l), _layer_spec(b.shape, l),
        ],
        out_specs=[xspec, pl.BlockSpec((tm, GM_WIDTH), lambda i: (i, 0))] if single else xspec,
        out_shape=[out, jax.ShapeDtypeStruct((T, GM_WIDTH), f32)] if single else out,
        scratch_shapes=[pltpu.VMEM((tm, GM_WIDTH), bf16)],
        compiler_params=_cparams("parallel"),
        name="gmlp_ln",
    )(x, w_in, w_s, b_s, vg, vb, w_out, g, b)
    return res if single else (res, None)


def _gdn_gates(ba, alog, dtb):
    beta = jax.nn.sigmoid(ba[:, :LANES])
    g = -jnp.exp(alog) * _softplus(ba[:, LANES:] + dtb)
    return beta, g


def _l2n(x):
    return x * lax.rsqrt(jnp.sum(x * x, -1, keepdims=True) + 1e-6)


def _mixer0_kernel(x_ref, wqz_ref, wglu_ref, wba_ref, dnw_ref, alog_ref, dtb_ref, ng_ref, ccw_ref, ccb_ref, cclg_ref,
                   cclb_ref, wout_ref, g_ref, b_ref,
                   o_ref, s_out_ref, dnc_out_ref, ccc_out_ref,
                   qkv_ext, glu_ext, rot_scr, s_scr, oc_scr):
    blk = pl.program_id(1)
    tb = x_ref.shape[0]
    C = GDN_BLOCK

    @pl.when(blk == 0)
    def _():
        qkv_ext[0:DN_TAIL, :] = jnp.zeros((DN_TAIL, 3 * GDN_WIDTH), f32)
        glu_ext[0:CC_TAIL, :] = jnp.zeros((CC_TAIL, CC_CH), f32)
        s_scr[...] = jnp.zeros_like(s_scr)

    x = x_ref[...]
    nq = 3 * GDN_WIDTH
    nproj = wqz_ref.shape[1] + wglu_ref.shape[1]

    def project(xv):
        xb = xv.astype(bf16)
        return jnp.concatenate([jnp.dot(xb, w[...], preferred_element_type=f32) for w in (wqz_ref, wglu_ref, wba_ref)], axis=1)

    proj = project(x)
    ba = proj[:, nproj:]

    qkv_ext[DN_TAIL:DN_TAIL + tb, :] = proj[:, :nq]
    blocks = []
    for i in range(tb // DN_ROWS):
        acc = None
        for s in range(SCONV_W):
            term = dnw_ref[SCONV_W - 1 - s:SCONV_W - s, :] * qkv_ext[pl.ds(DN_TAIL - s + i * DN_ROWS, DN_ROWS), :]
            acc = term if acc is None else acc + term
        blocks.append(_silu(acc))
    qkv = jnp.concatenate(blocks, axis=0)
    tail = qkv_ext[tb:tb + DN_TAIL, :]
    qkv_ext[0:DN_TAIL, :] = tail
    dnc_out_ref[0] = tail

    beta, g = _gdn_gates(ba, alog_ref[...], dtb_ref[...])

    P = 2 * GDN_DK
    pairs = range(GDN_HEADS // 2)
    chunks = range(tb // C)
    row = lax.broadcasted_iota(jnp.int32, (C, P), 0)
    col = lax.broadcasted_iota(jnp.int32, (C, P), 1)
    col = jnp.where(col >= C, col - C, col)
    causal = col <= row
    strict = col < row
    ltri = jnp.where(causal[:, :C], 1.0, 0.0).astype(bf16)
    qn = [_l2n(qkv[:, h * GDN_DK:(h + 1) * GDN_DK]) * (GDN_DK ** -0.5) for h in range(GDN_HEADS)]
    kn = [_l2n(qkv[:, GDN_WIDTH + h * GDN_DK:GDN_WIDTH + (h + 1) * GDN_DK]) for h in range(GDN_HEADS)]

    def pair_cols(m, h0):
        return jnp.concatenate([jnp.broadcast_to(m[:, h0:h0 + 1], (C, GDN_DK)),
                                jnp.broadcast_to(m[:, h0 + 1:h0 + 2], (C, GDN_DK))], axis=1)

    def bdiag(m):
        z = jnp.zeros((C, C), m.dtype)
        return jnp.concatenate([jnp.concatenate([m[:, :C], z], axis=1), jnp.concatenate([z, m[:, C:]], axis=1)], axis=0)

    def split2(a):
        a1 = a.astype(bf16)
        return a1, (a - a1.astype(f32)).astype(bf16)

    mm = lambda a, b_: jnp.dot(a, b_, preferred_element_type=f32)

    prob = {}
    for c in chunks:
        rs = slice(c * C, (c + 1) * C)
        gc = _dot_exact_lhs(ltri, g[rs])
        gct = gc.T
        eg = jnp.exp(gc)
        g_last = gc[C - 1:C, :]
        ekt = jnp.exp(g_last - gc)
        egl = jnp.exp(g_last)
        for pr in pairs:
            h0 = 2 * pr
            k_pair = jnp.concatenate([kn[h0][rs], kn[h0 + 1][rs]], axis=1)
            q_pair = jnp.concatenate([qn[h0][rs], qn[h0 + 1][rs]], axis=1)
            v_pair = qkv[rs, 2 * GDN_WIDTH + pr * P:2 * GDN_WIDTH + (pr + 1) * P]
            beta_pair = pair_cols(beta[rs], h0)
            grow = jnp.concatenate([gct[h0:h0 + 1, :], gct[h0 + 1:h0 + 2, :]], axis=1)
            decay = jnp.where(causal, jnp.exp(jnp.where(causal, pair_cols(gc, h0) - grow, 0.0)), 0.0)
            kb = k_pair * beta_pair
            prod = _dot_nt(jnp.concatenate([kb, q_pair], axis=0), bdiag(k_pair.astype(bf16)))
            eg_pair = pair_cols(eg, h0)
            kt = k_pair * pair_cols(ekt, h0)
            prob[c, pr] = dict(
                n=-jnp.where(strict, prod[:C] * decay, 0.0),
                qk=(prod[C:] * decay).astype(bf16),
                vb=(v_pair * beta_pair).astype(bf16),
                kbe=(kb * eg_pair).astype(bf16),
                qg=(q_pair * eg_pair).astype(bf16),
                ktt=jnp.concatenate([kt[:, :C].T, kt[:, C:].T], axis=1).astype(bf16),
                egl=jnp.concatenate([jnp.broadcast_to(egl[:, h0:h0 + 1], (1, GDN_DK)),
                                     jnp.broadcast_to(egl[:, h0 + 1:h0 + 2], (1, GDN_DK))], axis=1))

    Hc = C // 2
    lane = lax.broadcasted_iota(jnp.int32, (Hc, P), 1)
    prow = lax.broadcasted_iota(jnp.int32, (Hc, P), 0)
    first_half = jnp.where(lane >= C, lane - C, lane) < Hc
    eye4 = jnp.where((lane & (Hc - 1)) == prow, 1.0, 0.0)
    quarter = [(lane >= q * Hc) & (lane < (q + 1) * Hc) for q in range(P // Hc)]

    def bdiag4(m):
        return jnp.concatenate([jnp.where(qm, m, jnp.zeros_like(m)) for qm in quarter], axis=0)

    def dot3(l1, l2, w1, w2):
        r = l1.shape[0]
        o = mm(jnp.concatenate([l1, l2], axis=0), w1)
        return o[:r] + (o[r:] + mm(l1, w2))

    nk = {key: jnp.where(first_half, pb["n"][:Hc], pb["n"][Hc:]) for key, pb in prob.items()}
    pk = {key: eye4 + nk[key] for key in prob}
    levels = Hc.bit_length() - 1
    for j in range(levels):
        first, last = j == 0, j == levels - 1
        for key in prob:
            n1, n2 = split2(nk[key])
            if first:
                l1, l2 = n1, n2
            else:
                p1, p2 = split2(pk[key])
                l1 = p1 if last else jnp.concatenate([n1, p1], axis=0)
                l2 = p2 if last else jnp.concatenate([n2, p2], axis=0)
            res = dot3(l1, l2, bdiag4(n1), bdiag4(n2))
            if first:
                nk[key] = res
            elif last:
                pk[key] = pk[key] + res
            else:
                nk[key] = res[:Hc]
                pk[key] = pk[key] + res[Hc:]
    crow = lax.broadcasted_iota(jnp.int32, (C, P), 0)
    c_block = (crow >= Hc) & (col < Hc)
    for key, pb in prob.items():
        t_pan = pk[key]
        t1, t2 = split2(t_pan)
        c1, c2 = split2(jnp.where(c_block, pb["n"], 0.0))
        x1, x2 = split2(dot3(t1, t2, bdiag(c1), bdiag(c2)))
        zero = jnp.zeros_like(t1)
        low = dot3(x1, x2, bdiag4(jnp.where(first_half, t1, zero)), bdiag4(jnp.where(first_half, t2, zero)))
        pk[key] = jnp.concatenate([jnp.where(first_half, t_pan, 0.0), low + jnp.where(first_half, 0.0, t_pan)], axis=0)

    for key, pb in prob.items():
        t16 = pk[key].astype(bf16)
        pb["u"] = mm(t16, bdiag(pb["vb"]))
        pb["w"] = mm(t16, bdiag(pb["kbe"])).astype(bf16)

    for c in chunks:
        rs = slice(c * C, (c + 1) * C)
        for pr in pairs:
            pb = prob[c, pr]
            s_pair = s_scr[pr]
            o2 = mm(jnp.concatenate([pb["w"], pb["qg"]], axis=0), bdiag(s_pair.astype(bf16)))
            vbd = bdiag((pb["u"] - o2[:C]).astype(bf16))
            o = o2[C:] + mm(pb["qk"], vbd)
            s_scr[pr] = s_pair * pb["egl"] + mm(pb["ktt"], vbd)
            on = [o[:, j * GDN_DV:(j + 1) * GDN_DV] for j in range(2)]
            on = [t * lax.rsqrt(jnp.mean(t * t, -1, keepdims=True) + 1e-6) * ng_ref[...] for t in on]
            z_pair = proj[rs, nq + pr * P:nq + (pr + 1) * P]
            oc_scr[rs, pr * P:(pr + 1) * P] = (jnp.concatenate(on, axis=1) * _silu(z_pair)).astype(bf16)

    for pr in pairs:
        s_out_ref[0, 2 * pr] = s_scr[pr][:, :GDN_DV]
        s_out_ref[0, 2 * pr + 1] = s_scr[pr][:, GDN_DV:]

    ga = proj[:, nq + GDN_WIDTH:nq + GDN_WIDTH + CC_CH]
    gb = proj[:, nq + GDN_WIDTH + CC_CH:nproj]
    glu = ga * jax.nn.sigmoid(gb)
    glu_ext[CC_TAIL:CC_TAIL + tb, :] = glu
    span = CC_TAIL - SUBLANES
    for r in range(1, SUBLANES):
        rot_scr[r - 1] = glu_ext[pl.ds(SUBLANES - r, tb + span), :]

    for base in range(0, tb, CC_ROWS):
        acc = None
        for s in range(CC_W):
            a, r = divmod(s, SUBLANES)
            off = span - SUBLANES * a
            if r == 0:
                xs = glu_ext[base + SUBLANES + off:base + SUBLANES + off + CC_ROWS, :]
            else:
                xs = rot_scr[r - 1, base + off:base + off + CC_ROWS, :]
            term = ccw_ref[CC_W - 1 - s:CC_W - s, :] * xs
            acc = term if acc is None else acc + term
        cc = _silu(_ln(acc + ccb_ref[...], cclg_ref[...], cclb_ref[...]))
        oc_scr[base:base + CC_ROWS, GDN_WIDTH:] = cc.astype(bf16)
    tail = glu_ext[tb:tb + CC_TAIL, :]
    glu_ext[0:CC_TAIL, :] = tail
    ccc_out_ref[0] = tail

    y = jnp.dot(oc_scr[...], wout_ref[...], preferred_element_type=f32)
    o_ref[...] = _ln(DN_ALPHA * x + y, g_ref[MIX_LN_ROW:MIX_LN_ROW + 1, :], b_ref[MIX_LN_ROW:MIX_LN_ROW + 1, :])


def _mixer0_ln(x, wqz, wglu, wba, dnw, alog, dtb, ng, ccw, ccb, cclg, cclb, wout, g, b, *, l, batch, seq, tb=512):
    T, D = x.shape
    tb = min(tb, seq)
    nb = seq // tb
    consts = [wqz, wglu, wba, dnw, alog, dtb, ng, ccw, ccb, cclg, cclb, wout]
    return pl.pallas_call(
        _mixer0_kernel,
        grid=(batch, nb),
        in_specs=[pl.BlockSpec((tb, D), lambda i, j: (i * nb + j, 0))] + [_full(c.shape) for c in consts]
        + [_layer_spec(g.shape, l), _layer_spec(b.shape, l)],
        out_specs=[
            pl.BlockSpec((tb, D), lambda i, j: (i * nb + j, 0)),
            pl.BlockSpec((1, GDN_HEADS, GDN_DK, GDN_DV), lambda i, j: (i, 0, 0, 0)),
            pl.BlockSpec((1, DN_TAIL, 3 * GDN_WIDTH), lambda i, j: (i, 0, 0)),
            pl.BlockSpec((1, CC_TAIL, CC_CH), lambda i, j: (i, 0, 0)),
        ],
        out_shape=[
            jax.ShapeDtypeStruct((T, D), f32),
            jax.ShapeDtypeStruct((batch, GDN_HEADS, GDN_DK, GDN_DV), f32),
            jax.ShapeDtypeStruct((batch, DN_TAIL, 3 * GDN_WIDTH), f32),
            jax.ShapeDtypeStruct((batch, CC_TAIL, CC_CH), f32),
        ],
        scratch_shapes=[
            pltpu.VMEM((tb + DN_TAIL, 3 * GDN_WIDTH), f32),
            pltpu.VMEM((tb + CC_TAIL, CC_CH), f32),
            pltpu.VMEM((SUBLANES - 1, tb + CC_TAIL - SUBLANES, CC_CH), f32),
            pltpu.VMEM((GDN_HEADS // 2, GDN_DK, 2 * GDN_DV), f32),
            pltpu.VMEM((tb, GDN_WIDTH + CC_CH), bf16),
        ],
        compiler_params=_cparams("parallel", "arbitrary"),
        name="mixer0_ln",
    )(x, *consts, g, b)


def _mixer0_dec_kernel(x_ref, s_ref, dnc_ref, ccc_ref, wqz_ref, wglu_ref, wba_ref, dnw_ref, alog_ref, dtb_ref, ng_ref, ccw_ref,
                       ccb_ref, cclg_ref, cclb_ref, wout_ref, g_ref, b_ref,
                       o_ref, s_out_ref, dnc_out_ref, ccc_out_ref,
                       q_scr, k_scr, v_scr, z_scr, beta_scr, eg_scr, oc_scr, *, bs):
    i = pl.program_id(0)
    nq = 3 * GDN_WIDTH

    @pl.when(i == 0)
    def _():
        xb = x_ref[...].astype(bf16)
        proj = jnp.concatenate([jnp.dot(xb, wqz_ref[...], preferred_element_type=f32),
                                jnp.dot(xb, wglu_ref[...], preferred_element_type=f32)], axis=1)
        ba = jnp.dot(xb, wba_ref[...], preferred_element_type=f32)
        qkv_raw = proj[:, :nq]
        acc = dnw_ref[SCONV_W - 1:SCONV_W, :] * qkv_raw
        for j in range(SCONV_W - 1):
            acc = acc + dnw_ref[j:j + 1, :] * dnc_ref[j]
        for j in range(SCONV_W - 2):
            dnc_out_ref[j] = dnc_ref[j + 1]
        dnc_out_ref[SCONV_W - 2] = qkv_raw
        qkv = _silu(acc)
        for h in range(GDN_HEADS):
            hs = slice(h * GDN_DK, (h + 1) * GDN_DK)
            q_scr[:, hs] = _l2n(qkv[:, h * GDN_DK:(h + 1) * GDN_DK]) * (GDN_DK ** -0.5)
            k_scr[:, hs] = _l2n(qkv[:, GDN_WIDTH + h * GDN_DK:GDN_WIDTH + (h + 1) * GDN_DK])
        v_scr[...] = qkv[:, 2 * GDN_WIDTH:]
        z_scr[...] = _silu(proj[:, nq:nq + GDN_WIDTH])
        beta, g = _gdn_gates(ba, alog_ref[...], dtb_ref[...])
        beta_scr[...] = beta
        eg_scr[...] = jnp.exp(g)

        ga = proj[:, nq + GDN_WIDTH:nq + GDN_WIDTH + CC_CH]
        gb = proj[:, nq + GDN_WIDTH + CC_CH:]
        glu = ga * jax.nn.sigmoid(gb)
        acc = ccw_ref[CC_W - 1:CC_W, :] * glu
        for j in range(CC_W - 1):
            acc = acc + ccw_ref[j:j + 1, :] * ccc_ref[j]
        for j in range(CC_W - 2):
            ccc_out_ref[j] = ccc_ref[j + 1]
        ccc_out_ref[CC_W - 2] = glu
        cc = _silu(_ln(acc + ccb_ref[...], cclg_ref[...], cclb_ref[...]))
        oc_scr[:, GDN_WIDTH:] = cc

    rows = pl.ds(pl.multiple_of(i * bs, bs), bs)
    q_blk, k_blk, v_blk, z_blk = q_scr[rows, :], k_scr[rows, :], v_scr[rows, :], z_scr[rows, :]
    beta_blk, eg_blk = beta_scr[rows, :], eg_scr[rows, :]
    o_rows = []
    for s in range(bs):
        o_heads = []
        for h in range(GDN_HEADS):
            hs = slice(h * GDN_DK, (h + 1) * GDN_DK)
            k_col = jnp.broadcast_to(k_blk[s:s + 1, hs], (GDN_DK, GDN_DK)).T
            q_col = jnp.broadcast_to(q_blk[s:s + 1, hs], (GDN_DK, GDN_DK)).T
            b1 = beta_blk[s:s + 1, h:h + 1]
            e1 = eg_blk[s:s + 1, h:h + 1]
            s_old = s_ref[s, h]
            ks = jnp.sum(k_col * s_old, 0, keepdims=True)
            v_new = b1 * (v_blk[s:s + 1, hs] - e1 * ks)
            s_new = s_old * e1 + k_col * v_new
            s_out_ref[s, h] = s_new
            o = jnp.sum(q_col * s_new, 0, keepdims=True)
            o = o * lax.rsqrt(jnp.mean(o * o, -1, keepdims=True) + 1e-6) * ng_ref[...]
            o_heads.append(o * z_blk[s:s + 1, hs])
        o_rows.append(jnp.concatenate(o_heads, axis=-1))
    oc_scr[rows, :GDN_WIDTH] = jnp.concatenate(o_rows, axis=0)

    @pl.when(i == pl.num_programs(0) - 1)
    def _():
        y = jnp.dot(oc_scr[...].astype(bf16), wout_ref[...], preferred_element_type=f32)
        o_ref[...] = _ln(DN_ALPHA * x_ref[...] + y, g_ref[MIX_LN_ROW:MIX_LN_ROW + 1, :], b_ref[MIX_LN_ROW:MIX_LN_ROW + 1, :])


def _mixer0_dec_ln(x, s, dnc, ccc, wqz, wglu, wba, dnw, alog, dtb, ng, ccw, ccb, cclg, cclb, wout, g, b, *, l, bs=8):
    N, D = x.shape
    consts = [wqz, wglu, wba, dnw, alog, dtb, ng, ccw, ccb, cclg, cclb, wout]
    sspec = pl.BlockSpec((bs, GDN_HEADS, GDN_DK, GDN_DV), lambda i: (i, 0, 0, 0))
    return pl.pallas_call(
        functools.partial(_mixer0_dec_kernel, bs=bs),
        grid=(N // bs,),
        in_specs=[_full((N, D)), sspec, _full(dnc.shape), _full(ccc.shape)] + [_full(c.shape) for c in consts]
        + [_layer_spec(g.shape, l), _layer_spec(b.shape, l)],
        out_specs=[_full((N, D)), sspec, _full(dnc.shape), _full(ccc.shape)],
        out_shape=[jax.ShapeDtypeStruct((N, D), f32), jax.ShapeDtypeStruct(s.shape, f32),
                   jax.ShapeDtypeStruct(dnc.shape, f32), jax.ShapeDtypeStruct(ccc.shape, f32)],
        scratch_shapes=[pltpu.VMEM((N, GDN_WIDTH), f32)] * 4 + [pltpu.VMEM((N, LANES), f32)] * 2
        + [pltpu.VMEM((N, GDN_WIDTH + CC_CH), f32)],
        compiler_params=_cparams("arbitrary"),
        name="mixer0_dec_ln",
    )(x, s, dnc, ccc, *consts, g, b)


def _pad_lanes(v, n=LANES):
    return jnp.zeros((1, n), f32).at[0, :v.shape[0]].set(v.astype(f32))


def kernel(x_prompt, x_sample, mem_prompt, cache_mem_k, cache_mem_v, state_dn_S, state_dn_conv, state_cc_conv, ln_g, ln_b, ffn_w_gate, ffn_w_up, ffn_w_down, xa_wq, xa_wk, xa_wv, xa_wo, ab_w_in, dn_conv_w, dn_A_log, dn_dt_bias, dn_norm_g, cc_conv_w, cc_conv_b, cc_ln_g, cc_ln_b, ab_w_out, gm_w_in, gm_ln_g, gm_ln_b, gm_w_s, gm_b_s, gm_w_out):
    B, SEQ, D = x_prompt.shape
    N = x_sample.shape[0]
    row = lambda v: v.reshape(1, -1).astype(f32)

    wg, wu, wd = ffn_w_gate.astype(bf16), ffn_w_up.astype(bf16), ffn_w_down.astype(bf16)
    wq, wk, wv, wo = xa_wq.astype(bf16), xa_wk.astype(bf16), xa_wv.astype(bf16), xa_wo.astype(bf16)
    nq = 3 * GDN_WIDTH
    nz = nq + GDN_WIDTH
    w_qz, w_glu = ab_w_in[:, :nz].astype(bf16), ab_w_in[:, nz + 2 * GDN_HEADS:].astype(bf16)
    w_ba = jnp.zeros((D, 2 * LANES), f32)
    w_ba = w_ba.at[:, :GDN_HEADS].set(ab_w_in[:, nz:nz + GDN_HEADS])
    w_ba = w_ba.at[:, LANES:LANES + GDN_HEADS].set(ab_w_in[:, nz + GDN_HEADS:nz + 2 * GDN_HEADS]).astype(bf16)
    w_out = ab_w_out.astype(bf16)
    gw_in, gw_out = gm_w_in.astype(bf16), gm_w_out.astype(bf16)
    mixer_consts = (w_qz, w_glu, w_ba, dn_conv_w.astype(f32), _pad_lanes(dn_A_log), _pad_lanes(dn_dt_bias), row(dn_norm_g),
                    cc_conv_w.astype(f32), row(cc_conv_b), row(cc_ln_g), row(cc_ln_b), w_out)
    gm_consts = (gw_in, gm_w_s.astype(f32), gm_b_s.astype(f32)[:, :, None], row(gm_ln_g), row(gm_ln_b), gw_out)
    lng, lnb = ln_g.astype(f32), ln_b.astype(f32)

    mem_k, mem_v, mem_kh, mem_vh = _mem_kv(mem_prompt, wk, wv)

    yp, ys = x_prompt.reshape(B * SEQ, D), x_sample.reshape(N, D)
    state = (state_dn_S, jnp.swapaxes(state_dn_conv, 0, 1), jnp.swapaxes(state_cc_conv, 0, 1))
    for l in range(DEPTH):
        yp, ys = _ffn_ln(yp, wg, wu, wd, lng, lnb, l=l, half=0, tm=FFN_ROWS, xs=ys)
        if l % 2 == 0:
            yp, dn_s_p, dn_c_p, cc_c_p = _mixer0_ln(yp, *mixer_consts, lng, lnb, l=l, batch=B, seq=SEQ)
            ys, dn_s_s, dn_c_s, cc_c_s = _mixer0_dec_ln(ys, *state, *mixer_consts, lng, lnb, l=l)
        else:
            yp, _ = _gmlp_ln(yp, *gm_consts, lng, lnb, l=l, tm=GMLP_ROWS, single=False)
            ys, gm_v_s = _gmlp_ln(ys, *gm_consts, lng, lnb, l=l, tm=N, single=True)
        yp = _xattn_ln(yp, mem_kh, mem_vh, wq, wo, lng, lnb, l=l, seq=SEQ)
        yp, ys = _ffn_ln(yp, wg, wu, wd, lng, lnb, l=l, half=1, tm=FFN_ATTEND_ROWS, xs=ys,
                         attend=(cache_mem_k, cache_mem_v, wq, wo))

    return (yp.reshape(B, SEQ, D), ys.reshape(N, 1, D), mem_k, mem_v,
            dn_s_p, dn_c_p[:, DN_TAIL - (SCONV_W - 1):], cc_c_p[:, CC_TAIL - (CC_W - 1):],
            dn_s_s, jnp.swapaxes(dn_c_s, 0, 1), jnp.swapaxes(cc_c_s, 0, 1), gm_v_s.reshape(N, 1, GM_WIDTH))
```

```python
import functools

import jax
import jax.numpy as jnp
from jax import lax
from jax.experimental import pallas as pl
from jax.experimental.pallas import tpu as pltpu

D_MODEL = 1024
DEPTH = 2
DN_ALPHA = (2 * DEPTH) ** 0.25
LN_EPS = 1e-5
GDN_HEADS = 4
GDN_DK = 128
GDN_DV = 128
GDN_WIDTH = GDN_HEADS * GDN_DK
SCONV_W = 4
CC_CH = D_MODEL // 2
CC_W = 31
GM_WIDTH = D_MODEL
GM_GROUPS = 4
GM_CHUNK = 128
N_MEM = 256
XA_HEADS = 4
XA_HEAD_DIM = D_MODEL // XA_HEADS
D_FF = 2816
MIX_LN_ROW = 1
XA_LN_ROW = 2

LANES = 128
SUBLANES = 8
GDN_BLOCK = 128
DN_TAIL = SUBLANES
CC_TAIL = 32
CC_ROWS = 32
DN_ROWS = 16
FFN_ROWS = 1024
GMLP_ROWS = 1024
FFN_ATTEND_ROWS = 512
VMEM_LIMIT = 56 * 1024 * 1024

bf16 = jnp.bfloat16
f32 = jnp.float32


def _cparams(*sem):
    return pltpu.CompilerParams(dimension_semantics=sem, vmem_limit_bytes=VMEM_LIMIT)


def _full(shape):
    n = len(shape)
    return pl.BlockSpec(shape, lambda *_: (0,) * n)


def _dot_nt(a, b):
    return lax.dot_general(a.astype(bf16), b.astype(bf16), (((1,), (1,)), ((), ())), preferred_element_type=f32)


def _split3(a):
    a1 = a.astype(bf16)
    r = a - a1.astype(f32)
    a2 = r.astype(bf16)
    a3 = (r - a2.astype(f32)).astype(bf16)
    return a1, a2, a3


def _dot_exact_lhs(a_exact_bf16, b):
    b1, b2, b3 = _split3(b)
    d = lambda y: jnp.dot(a_exact_bf16, y, preferred_element_type=f32)
    return d(b1) + (d(b2) + d(b3))


def _ln(y, g, b):
    mu = jnp.mean(y, -1, keepdims=True)
    d = y - mu
    var = jnp.mean(d * d, -1, keepdims=True)
    return d * lax.rsqrt(var + LN_EPS) * g + b


def _silu(x):
    return x * jax.nn.sigmoid(x)


def _softplus(x):
    return jnp.maximum(x, 0.0) + jnp.log(1.0 + jnp.exp(-jnp.abs(x)))


def _decode_attend(q_rows, k_ref, v_ref):
    grp = SUBLANES // XA_HEADS
    o_rows = []
    for s in range(q_rows.shape[0]):
        q4 = jnp.concatenate([q_rows[s:s + 1, h * XA_HEAD_DIM:(h + 1) * XA_HEAD_DIM] for h in range(XA_HEADS)], axis=0)
        q8 = jnp.concatenate([q4] * grp, axis=0)
        both = lambda t: sum(t[j * XA_HEADS:(j + 1) * XA_HEADS] for j in range(grp))
        k3 = k_ref[s].reshape(N_MEM // grp, grp * XA_HEADS, XA_HEAD_DIM)
        v3 = v_ref[s].reshape(N_MEM // grp, grp * XA_HEADS, XA_HEAD_DIM)
        sc = jnp.sum(k3 * q8[None], -1, keepdims=True) * (XA_HEAD_DIM ** -0.5)
        mx = jnp.max(sc, 0)
        mx = functools.reduce(jnp.maximum, [mx[j * XA_HEADS:(j + 1) * XA_HEADS] for j in range(grp)])
        e = jnp.exp(sc - jnp.concatenate([mx] * grp, axis=0)[None])
        den = both(jnp.sum(e, 0))
        p = e / jnp.concatenate([den] * grp, axis=0)[None]
        o4 = both(jnp.sum(p * v3, 0))
        o_rows.append(jnp.concatenate([o4[h:h + 1, :] for h in range(XA_HEADS)], axis=-1))
    return jnp.concatenate(o_rows, axis=0)


def _ffn_kernel(*refs, ln_row, rows, tf, samples, attend):
    x_ref, wg_ref, wu_ref, wd_ref, g_ref, b_ref = refs[:6]
    pos = 6
    if samples:
        xs_ref = refs[pos]
        pos += 1
    if attend:
        ck_ref, cv_ref, wq_ref, wo_ref = refs[pos:pos + 4]
        pos += 4
    o_ref = refs[pos]
    pos += 1
    if samples:
        os_ref = refs[pos]
        pos += 1
    if attend:
        q_scr, a_scr = refs[pos:pos + 2]
    i = pl.program_id(0)
    r = slice(ln_row, ln_row + 1)
    xr = slice(XA_LN_ROW, XA_LN_ROW + 1)
    nf = wg_ref.shape[1] // tf
    norm = lambda y: _ln(y, g_ref[r, :], b_ref[r, :])

    def pre_norm(x):
        xb = x.astype(bf16)
        acc = None
        for j in range(nf):
            cs = slice(j * tf, (j + 1) * tf)
            hg = jnp.dot(xb, wg_ref[:, cs], preferred_element_type=f32)
            hu = jnp.dot(xb, wu_ref[:, cs], preferred_element_type=f32)
            d = jnp.dot((_silu(hg) * hu).astype(bf16), wd_ref[cs, :], preferred_element_type=f32)
            acc = d if acc is None else acc + d
        return DN_ALPHA * x + 0.5 * acc

    def attend_step():
        bs = ck_ref.shape[0]
        share = max(SUBLANES // bs, 1)
        nrow = bs * share
        blk = pl.ds(pl.multiple_of((i // share) * nrow, nrow), nrow)
        q_blk = q_scr[blk, :]
        sub = i % share
        q_rows = q_blk[:bs]
        for j in range(1, share):
            q_rows = jnp.where(sub == j, q_blk[j * bs:(j + 1) * bs], q_rows)
        o_rows = _decode_attend(q_rows, ck_ref, cv_ref)
        if share > 1:
            a_blk = a_scr[blk, :]
            o_rows = jnp.concatenate([jnp.where(sub == j, o_rows, a_blk[j * bs:(j + 1) * bs]) for j in range(share)], axis=0)
        a_scr[blk, :] = o_rows

    def sample_rows():
        xs = xs_ref[...]
        if attend:
            att = jnp.dot(a_scr[...].astype(bf16), wo_ref[...], preferred_element_type=f32)
            xs = _ln(DN_ALPHA * xs + att, g_ref[xr, :], b_ref[xr, :])
        os_ref[...] = norm(pre_norm(xs))

    if attend:
        @pl.when(i == 0)
        def _():
            q_scr[...] = jnp.dot(xs_ref[...].astype(bf16), wq_ref[...], preferred_element_type=f32)
            a_scr[...] = jnp.zeros_like(a_scr)

    for r0 in range(0, x_ref.shape[0], rows):
        o_ref[r0:r0 + rows, :] = norm(pre_norm(x_ref[r0:r0 + rows, :]))
    if attend:
        attend_step()
    if samples:
        pl.when(i == pl.num_programs(0) - 1)(sample_rows)


def _ffn_ln(x, wg, wu, wd, g, b, *, l, half, tm, rows=512, tf=256, xs=None, attend=None):
    T, D = x.shape
    tm = min(tm, T)
    rows = min(rows, tm)
    F = wg.shape[-1]
    nln = g.shape[1]
    steps = T // tm
    once = pl.Buffered(1)
    xspec = pl.BlockSpec((tm, D), lambda i: (i, 0))
    operands = [x, wg, wu, wd, g, b]
    in_specs = [
        xspec,
        pl.BlockSpec((None, None, D, F), lambda i: (l, half, 0, 0), pipeline_mode=once),
        pl.BlockSpec((None, None, D, F), lambda i: (l, half, 0, 0), pipeline_mode=once),
        pl.BlockSpec((None, None, F, D), lambda i: (l, half, 0, 0), pipeline_mode=once),
        pl.BlockSpec((None, nln, D), lambda i: (l, 0, 0)),
        pl.BlockSpec((None, nln, D), lambda i: (l, 0, 0)),
    ]
    out_specs, out_shape, scratch = [xspec], [jax.ShapeDtypeStruct((T, D), f32)], []
    if xs is not None:
        N = xs.shape[0]
        operands.append(xs)
        in_specs.append(_full((N, D)))
        out_specs.append(_full((N, D)))
        out_shape.append(jax.ShapeDtypeStruct((N, D), f32))
    if attend is not None:
        ck, cv, wq, wo = attend
        assert N % steps == 0
        bs = N // steps
        assert bs % SUBLANES == 0 or SUBLANES % bs == 0
        cspec = pl.BlockSpec((None, bs, N_MEM, XA_HEADS, XA_HEAD_DIM), lambda i: (l, i, 0, 0, 0))
        wspec = pl.BlockSpec((None, D, D), lambda i: (l, 0, 0), pipeline_mode=once)
        operands += [ck, cv, wq, wo]
        in_specs += [cspec, cspec, wspec, wspec]
        scratch = [pltpu.VMEM((N, D), f32), pltpu.VMEM((N, D), f32)]
    res = pl.pallas_call(
        functools.partial(_ffn_kernel, ln_row=(nln - 1) * half, rows=rows, tf=tf, samples=xs is not None,
                          attend=attend is not None),
        grid=(steps,),
        in_specs=in_specs,
        out_specs=out_specs,
        out_shape=out_shape,
        scratch_shapes=scratch,
        compiler_params=_cparams("parallel" if xs is None else "arbitrary"),
        name="ffn_ln",
    )(*operands)
    return res[0] if xs is None else res


def _memkv_kernel(m_ref, wk_ref, wv_ref, k_ref, v_ref, kh_ref, vh_ref):
    nb = m_ref.shape[0]
    for b in range(nb):
        mb = m_ref[b].astype(bf16)
        k = jnp.dot(mb, wk_ref[...], preferred_element_type=f32)
        v = jnp.dot(mb, wv_ref[...], preferred_element_type=f32)
        for h in range(XA_HEADS):
            sl = slice(h * XA_HEAD_DIM, (h + 1) * XA_HEAD_DIM)
            k_ref[b, :, h, :] = k[:, sl]
            v_ref[b, :, h, :] = v[:, sl]
            kh_ref[b, h] = k[:, sl].astype(bf16)
            vh_ref[b, h] = v[:, sl].astype(bf16)


def _mem_kv(mem, wk, wv, *, nb=2):
    B, M, D = mem.shape
    nb = min(nb, B)
    L = wk.shape[0]
    out = jax.ShapeDtypeStruct((L, B, M, XA_HEADS, XA_HEAD_DIM), f32)
    outh = jax.ShapeDtypeStruct((L, B, XA_HEADS, M, XA_HEAD_DIM), bf16)
    ospec = pl.BlockSpec((None, nb, M, XA_HEADS, XA_HEAD_DIM), lambda l, i: (l, i, 0, 0, 0))
    hspec = pl.BlockSpec((None, nb, XA_HEADS, M, XA_HEAD_DIM), lambda l, i: (l, i, 0, 0, 0))
    return pl.pallas_call(
        _memkv_kernel,
        grid=(L, B // nb),
        in_specs=[
            pl.BlockSpec((nb, M, D), lambda l, i: (i, 0, 0)),
            pl.BlockSpec((None, D, D), lambda l, i: (l, 0, 0)),
            pl.BlockSpec((None, D, D), lambda l, i: (l, 0, 0)),
        ],
        out_specs=[ospec, ospec, hspec, hspec],
        out_shape=[out, out, outh, outh],
        compiler_params=_cparams("parallel", "parallel"),
        name="mem_kv",
    )(mem, wk, wv)


def _xattn_kernel(x_ref, k_ref, v_ref, wq_ref, wo_ref, g_ref, b_ref, o_ref, oh_ref, *, rows):
    starts = list(range(0, x_ref.shape[0], rows))
    proj = lambda r: jnp.dot(x_ref[r:r + rows, :].astype(bf16), wq_ref[...], preferred_element_type=f32).astype(bf16)
    nxt = proj(starts[0])
    for i, r0 in enumerate(starts):
        rs = slice(r0, r0 + rows)
        x = x_ref[rs, :]
        q = nxt
        if i + 1 < len(starts):
            nxt = proj(starts[i + 1])
        for h in range(XA_HEADS):
            sl = slice(h * XA_HEAD_DIM, (h + 1) * XA_HEAD_DIM)
            s = _dot_nt(q[:, sl], k_ref[h]) * (XA_HEAD_DIM ** -0.5)
            s = s - jnp.max(s, -1, keepdims=True)
            e = jnp.exp(s)
            p = e / jnp.sum(e, -1, keepdims=True)
            oh_ref[rs, sl] = jnp.dot(p.astype(bf16), v_ref[h], preferred_element_type=f32).astype(bf16)
        att = jnp.dot(oh_ref[rs, :], wo_ref[...], preferred_element_type=f32)
        o_ref[rs, :] = _ln(DN_ALPHA * x + att, g_ref[XA_LN_ROW:XA_LN_ROW + 1, :], b_ref[XA_LN_ROW:XA_LN_ROW + 1, :])


def _layer_spec(shape, l):
    n = len(shape)
    return pl.BlockSpec((None,) + tuple(shape[1:]), lambda *_: (l,) + (0,) * (n - 1))


def _xattn_ln(x, mk, mv, wq, wo, g, b, *, l, seq, tq=2048, rows=512):
    T, D = x.shape
    tq = min(tq, seq)
    rows = min(rows, tq)
    nq = seq // tq
    mspec = pl.BlockSpec((None, None, XA_HEADS, N_MEM, XA_HEAD_DIM), lambda i: (l, i // nq, 0, 0, 0))
    return pl.pallas_call(
        functools.partial(_xattn_kernel, rows=rows),
        grid=(T // tq,),
        in_specs=[
            pl.BlockSpec((tq, D), lambda i: (i, 0)),
            mspec, mspec,
            _layer_spec(wq.shape, l), _layer_spec(wo.shape, l), _layer_spec(g.shape, l), _layer_spec(b.shape, l),
        ],
        out_specs=pl.BlockSpec((tq, D), lambda i: (i, 0)),
        out_shape=jax.ShapeDtypeStruct((T, D), f32),
        scratch_shapes=[pltpu.VMEM((tq, D), bf16)],
        compiler_params=_cparams("parallel"),
        name="xattn_ln",
    )(x, mk, mv, wq, wo, g, b)


def _gmlp_kernel(x_ref, win_ref, ws_ref, bs_ref, vg_ref, vb_ref, wout_ref, g_ref, b_ref, o_ref, *rest, single, rows):
    v_ref, uf_ref = rest if single else (None, rest[0])
    gw = GM_WIDTH // GM_GROUPS
    if not single:
        row = lax.broadcasted_iota(jnp.int32, (GM_CHUNK, GM_CHUNK), 0)
        col = lax.broadcasted_iota(jnp.int32, (GM_CHUNK, GM_CHUNK), 1)
        wmask = [jnp.where(col <= row, ws_ref[g], 0.0).astype(bf16) for g in range(GM_GROUPS)]
    starts = list(range(0, x_ref.shape[0], rows))
    proj = lambda r: jnp.dot(x_ref[r:r + rows, :].astype(bf16), win_ref[...], preferred_element_type=f32)
    nxt = proj(starts[0])
    for i, r0 in enumerate(starts):
        x = x_ref[r0:r0 + rows, :]
        cur = nxt
        if i + 1 < len(starts):
            nxt = proj(starts[i + 1])
        pr = jax.nn.gelu(cur)
        u = pr[:, :GM_WIDTH]
        v = _ln(pr[:, GM_WIDTH:], vg_ref[...], vb_ref[...])
        if single:
            v_ref[r0:r0 + rows, :] = v
            for g in range(GM_GROUPS):
                sl = slice(g * gw, (g + 1) * gw)
                f = ws_ref[g][0:1, 0:1] * v[:, sl] + bs_ref[g][0:1, 0:1]
                uf_ref[r0:r0 + rows, sl] = (u[:, sl] * f).astype(bf16)
        else:
            vb16 = v.astype(bf16)
            for g in range(GM_GROUPS):
                sl = slice(g * gw, (g + 1) * gw)
                for c in range(rows // GM_CHUNK):
                    cs = slice(c * GM_CHUNK, (c + 1) * GM_CHUNK)
                    f = jnp.dot(wmask[g], vb16[cs, sl], preferred_element_type=f32) + bs_ref[g]
                    uf_ref[r0 + c * GM_CHUNK:r0 + (c + 1) * GM_CHUNK, sl] = (u[cs, sl] * f).astype(bf16)
        y = jnp.dot(uf_ref[r0:r0 + rows, :], wout_ref[...], preferred_element_type=f32)
        o_ref[r0:r0 + rows, :] = _ln(DN_ALPHA * x + y, g_ref[MIX_LN_ROW:MIX_LN_ROW + 1, :], b_ref[MIX_LN_ROW:MIX_LN_ROW + 1, :])


def _gmlp_ln(x, w_in, w_s, b_s, vg, vb, w_out, g, b, *, l, tm, single, rows=512):
    T, D = x.shape
    tm = min(tm, T)
    rows = min(rows, tm)
    xspec = pl.BlockSpec((tm, D), lambda i: (i, 0))
    out = jax.ShapeDtypeStruct((T, D), f32)
    res = pl.pallas_call(
        functools.partial(_gmlp_kernel, single=single, rows=rows),
        grid=(T // tm,),
        in_specs=[
            xspec, _full((D, 2 * GM_WIDTH)), _full(w_s.shape), _full(b_s.shape),
            _full((1, GM_WIDTH)), _full((1, GM_WIDTH)), _full((GM_WIDTH, D)), _layer_spec(g.shape, l), _layer_spec(b.shape, l),
        ],
        out_specs=[xspec, pl.BlockSpec((tm, GM_WIDTH), lambda i: (i, 0))] if single else xspec,
        out_shape=[out, jax.ShapeDtypeStruct((T, GM_WIDTH), f32)] if single else out,
        scratch_shapes=[pltpu.VMEM((tm, GM_WIDTH), bf16)],
        compiler_params=_cparams("parallel"),
        name="gmlp_ln",
    )(x, w_in, w_s, b_s, vg, vb, w_out, g, b)
    return res if single else (res, None)


def _gdn_gates(ba, alog, dtb):
    beta = jax.nn.sigmoid(ba[:, :LANES])
    g = -jnp.exp(alog) * _softplus(ba[:, LANES:] + dtb)
    return beta, g


def _l2n(x):
    return x * lax.rsqrt(jnp.sum(x * x, -1, keepdims=True) + 1e-6)


def _mixer0_kernel(x_ref, wqz_ref, wglu_ref, wba_ref, dnw_ref, alog_ref, dtb_ref, ng_ref, ccw_ref, ccb_ref, cclg_ref,
                   cclb_ref, wout_ref, g_ref, b_ref,
                   o_ref, s_out_ref, dnc_out_ref, ccc_out_ref,
                   qkv_ext, glu_ext, rot_scr, s_scr, oc_scr):
    blk = pl.program_id(1)
    tb = x_ref.shape[0]
    C = GDN_BLOCK

    @pl.when(blk == 0)
    def _():
        qkv_ext[0:DN_TAIL, :] = jnp.zeros((DN_TAIL, 3 * GDN_WIDTH), f32)
        glu_ext[0:CC_TAIL, :] = jnp.zeros((CC_TAIL, CC_CH), f32)
        s_scr[...] = jnp.zeros_like(s_scr)

    x = x_ref[...]
    nq = 3 * GDN_WIDTH
    xb = x.astype(bf16)
    ba = jnp.dot(xb, wba_ref[...], preferred_element_type=f32)
    proj = jnp.dot(xb, wqz_ref[...], preferred_element_type=f32)

    qkv_ext[DN_TAIL:DN_TAIL + tb, :] = proj[:, :nq]
    blocks = []
    for i in range(tb // DN_ROWS):
        acc = None
        for s in range(SCONV_W):
            term = dnw_ref[SCONV_W - 1 - s:SCONV_W - s, :] * qkv_ext[pl.ds(DN_TAIL - s + i * DN_ROWS, DN_ROWS), :]
            acc = term if acc is None else acc + term
        blocks.append(_silu(acc))
    qkv = jnp.concatenate(blocks, axis=0)
    tail = qkv_ext[tb:tb + DN_TAIL, :]
    qkv_ext[0:DN_TAIL, :] = tail
    dnc_out_ref[0] = tail

    beta, g = _gdn_gates(ba, alog_ref[...], dtb_ref[...])

    P = 2 * GDN_DK
    pairs = range(GDN_HEADS // 2)
    chunks = range(tb // C)
    row = lax.broadcasted_iota(jnp.int32, (C, P), 0)
    col = lax.broadcasted_iota(jnp.int32, (C, P), 1)
    col = jnp.where(col >= C, col - C, col)
    causal = col <= row
    strict = col < row
    ltri = jnp.where(causal[:, :C], 1.0, 0.0).astype(bf16)
    qn = [_l2n(qkv[:, h * GDN_DK:(h + 1) * GDN_DK]) * (GDN_DK ** -0.5) for h in range(GDN_HEADS)]
    kn = [_l2n(qkv[:, GDN_WIDTH + h * GDN_DK:GDN_WIDTH + (h + 1) * GDN_DK]) for h in range(GDN_HEADS)]

    def pair_cols(m, h0):
        return jnp.concatenate([jnp.broadcast_to(m[:, h0:h0 + 1], (C, GDN_DK)),
                                jnp.broadcast_to(m[:, h0 + 1:h0 + 2], (C, GDN_DK))], axis=1)

    def bdiag(m):
        z = jnp.zeros((C, C), m.dtype)
        return jnp.concatenate([jnp.concatenate([m[:, :C], z], axis=1), jnp.concatenate([z, m[:, C:]], axis=1)], axis=0)

    def split2(a):
        a1 = a.astype(bf16)
        return a1, (a - a1.astype(f32)).astype(bf16)

    mm = lambda a, b_: jnp.dot(a, b_, preferred_element_type=f32)

    prob = {}
    for c in chunks:
        rs = slice(c * C, (c + 1) * C)
        gc = _dot_exact_lhs(ltri, g[rs])
        gct = gc.T
        eg = jnp.exp(gc)
        g_last = gc[C - 1:C, :]
        ekt = jnp.exp(g_last - gc)
        egl = jnp.exp(g_last)
        for pr in pairs:
            h0 = 2 * pr
            k_pair = jnp.concatenate([kn[h0][rs], kn[h0 + 1][rs]], axis=1)
            q_pair = jnp.concatenate([qn[h0][rs], qn[h0 + 1][rs]], axis=1)
            v_pair = qkv[rs, 2 * GDN_WIDTH + pr * P:2 * GDN_WIDTH + (pr + 1) * P]
            beta_pair = pair_cols(beta[rs], h0)
            grow = jnp.concatenate([gct[h0:h0 + 1, :], gct[h0 + 1:h0 + 2, :]], axis=1)
            decay = jnp.where(causal, jnp.exp(jnp.where(causal, pair_cols(gc, h0) - grow, 0.0)), 0.0)
            kb = k_pair * beta_pair
            prod = _dot_nt(jnp.concatenate([kb, q_pair], axis=0), bdiag(k_pair.astype(bf16)))
            eg_pair = pair_cols(eg, h0)
            kt = k_pair * pair_cols(ekt, h0)
            prob[c, pr] = dict(
                n=-jnp.where(strict, prod[:C] * decay, 0.0),
                qk=(prod[C:] * decay).astype(bf16),
                vb=(v_pair * beta_pair).astype(bf16),
                kbe=(kb * eg_pair).astype(bf16),
                qg=(q_pair * eg_pair).astype(bf16),
                ktt=jnp.concatenate([kt[:, :C].T, kt[:, C:].T], axis=1).astype(bf16),
                egl=jnp.concatenate([jnp.broadcast_to(egl[:, h0:h0 + 1], (1, GDN_DK)),
                                     jnp.broadcast_to(egl[:, h0 + 1:h0 + 2], (1, GDN_DK))], axis=1))

    glu_in = jnp.dot(xb, wglu_ref[...], preferred_element_type=f32)
    glu_ext[CC_TAIL:CC_TAIL + tb, :] = glu_in[:, :CC_CH] * jax.nn.sigmoid(glu_in[:, CC_CH:])
    span = CC_TAIL - SUBLANES
    for r in range(1, SUBLANES):
        rot_scr[r - 1] = glu_ext[pl.ds(SUBLANES - r, tb + span), :]

    def conv_rows(base):
        acc = None
        for s in range(CC_W):
            a, r = divmod(s, SUBLANES)
            off = span - SUBLANES * a
            if r == 0:
                xs = glu_ext[base + SUBLANES + off:base + SUBLANES + off + CC_ROWS, :]
            else:
                xs = rot_scr[r - 1, base + off:base + off + CC_ROWS, :]
            term = ccw_ref[CC_W - 1 - s:CC_W - s, :] * xs
            acc = term if acc is None else acc + term
        cc = _silu(_ln(acc + ccb_ref[...], cclg_ref[...], cclb_ref[...]))
        oc_scr[base:base + CC_ROWS, GDN_WIDTH:] = cc.astype(bf16)


    Hc = C // 2
    lane = lax.broadcasted_iota(jnp.int32, (Hc, P), 1)
    prow = lax.broadcasted_iota(jnp.int32, (Hc, P), 0)
    first_half = jnp.where(lane >= C, lane - C, lane) < Hc
    eye4 = jnp.where((lane & (Hc - 1)) == prow, 1.0, 0.0)
    quarter = [(lane >= q * Hc) & (lane < (q + 1) * Hc) for q in range(P // Hc)]

    def bdiag4(m):
        return jnp.concatenate([jnp.where(qm, m, jnp.zeros_like(m)) for qm in quarter], axis=0)

    def dot3(l1, l2, w1, w2):
        r = l1.shape[0]
        o = mm(jnp.concatenate([l1, l2], axis=0), w1)
        return o[:r] + (o[r:] + mm(l1, w2))

    nk = {key: jnp.where(first_half, pb["n"][:Hc], pb["n"][Hc:]) for key, pb in prob.items()}
    pk = {key: eye4 + nk[key] for key in prob}
    levels = Hc.bit_length() - 1
    for j in range(levels):
        first, last = j == 0, j == levels - 1
        for key in prob:
            n1, n2 = split2(nk[key])
            if first:
                l1, l2 = n1, n2
            else:
                p1, p2 = split2(pk[key])
                l1 = p1 if last else jnp.concatenate([n1, p1], axis=0)
                l2 = p2 if last else jnp.concatenate([n2, p2], axis=0)
            res = dot3(l1, l2, bdiag4(n1), bdiag4(n2))
            if first:
                nk[key] = res
            elif last:
                pk[key] = pk[key] + res
            else:
                nk[key] = res[:Hc]
                pk[key] = pk[key] + res[Hc:]
    crow = lax.broadcasted_iota(jnp.int32, (C, P), 0)
    c_block = (crow >= Hc) & (col < Hc)
    for key, pb in prob.items():
        t_pan = pk[key]
        t1, t2 = split2(t_pan)
        c1, c2 = split2(jnp.where(c_block, pb["n"], 0.0))
        x1, x2 = split2(dot3(t1, t2, bdiag(c1), bdiag(c2)))
        zero = jnp.zeros_like(t1)
        low = dot3(x1, x2, bdiag4(jnp.where(first_half, t1, zero)), bdiag4(jnp.where(first_half, t2, zero)))
        pk[key] = jnp.concatenate([jnp.where(first_half, t_pan, 0.0), low + jnp.where(first_half, 0.0, t_pan)], axis=0)

    for key, pb in prob.items():
        t16 = pk[key].astype(bf16)
        pb["u"] = mm(t16, bdiag(pb["vb"]))
        pb["w"] = mm(t16, bdiag(pb["kbe"])).astype(bf16)

    for c in chunks:
        rs = slice(c * C, (c + 1) * C)
        for pr in pairs:
            pb = prob[c, pr]
            s_pair = s_scr[pr]
            o2 = mm(jnp.concatenate([pb["w"], pb["qg"]], axis=0), bdiag(s_pair.astype(bf16)))
            vbd = bdiag((pb["u"] - o2[:C]).astype(bf16))
            o = o2[C:] + mm(pb["qk"], vbd)
            s_scr[pr] = s_pair * pb["egl"] + mm(pb["ktt"], vbd)
            on = [o[:, j * GDN_DV:(j + 1) * GDN_DV] for j in range(2)]
            on = [t * lax.rsqrt(jnp.mean(t * t, -1, keepdims=True) + 1e-6) * ng_ref[...] for t in on]
            z_pair = proj[rs, nq + pr * P:nq + (pr + 1) * P]
            oc_scr[rs, pr * P:(pr + 1) * P] = (jnp.concatenate(on, axis=1) * _silu(z_pair)).astype(bf16)

    for pr in pairs:
        s_out_ref[0, 2 * pr] = s_scr[pr][:, :GDN_DV]
        s_out_ref[0, 2 * pr + 1] = s_scr[pr][:, GDN_DV:]

    for base in range(0, tb, CC_ROWS):
        conv_rows(base)
    tail = glu_ext[tb:tb + CC_TAIL, :]
    glu_ext[0:CC_TAIL, :] = tail
    ccc_out_ref[0] = tail

    y = jnp.dot(oc_scr[...], wout_ref[...], preferred_element_type=f32)
    o_ref[...] = _ln(DN_ALPHA * x + y, g_ref[MIX_LN_ROW:MIX_LN_ROW + 1, :], b_ref[MIX_LN_ROW:MIX_LN_ROW + 1, :])


def _mixer0_ln(x, wqz, wglu, wba, dnw, alog, dtb, ng, ccw, ccb, cclg, cclb, wout, g, b, *, l, batch, seq, tb=512):
    T, D = x.shape
    tb = min(tb, seq)
    nb = seq // tb
    consts = [wqz, wglu, wba, dnw, alog, dtb, ng, ccw, ccb, cclg, cclb, wout]
    return pl.pallas_call(
        _mixer0_kernel,
        grid=(batch, nb),
        in_specs=[pl.BlockSpec((tb, D), lambda i, j: (i * nb + j, 0))] + [_full(c.shape) for c in consts]
        + [_layer_spec(g.shape, l), _layer_spec(b.shape, l)],
        out_specs=[
            pl.BlockSpec((tb, D), lambda i, j: (i * nb + j, 0)),
            pl.BlockSpec((1, GDN_HEADS, GDN_DK, GDN_DV), lambda i, j: (i, 0, 0, 0)),
            pl.BlockSpec((1, DN_TAIL, 3 * GDN_WIDTH), lambda i, j: (i, 0, 0)),
            pl.BlockSpec((1, CC_TAIL, CC_CH), lambda i, j: (i, 0, 0)),
        ],
        out_shape=[
            jax.ShapeDtypeStruct((T, D), f32),
            jax.ShapeDtypeStruct((batch, GDN_HEADS, GDN_DK, GDN_DV), f32),
            jax.ShapeDtypeStruct((batch, DN_TAIL, 3 * GDN_WIDTH), f32),
            jax.ShapeDtypeStruct((batch, CC_TAIL, CC_CH), f32),
        ],
        scratch_shapes=[
            pltpu.VMEM((tb + DN_TAIL, 3 * GDN_WIDTH), f32),
            pltpu.VMEM((tb + CC_TAIL, CC_CH), f32),
            pltpu.VMEM((SUBLANES - 1, tb + CC_TAIL - SUBLANES, CC_CH), f32),
            pltpu.VMEM((GDN_HEADS // 2, GDN_DK, 2 * GDN_DV), f32),
            pltpu.VMEM((tb, GDN_WIDTH + CC_CH), bf16),
        ],
        compiler_params=_cparams("parallel", "arbitrary"),
        name="mixer0_ln",
    )(x, *consts, g, b)


def _mixer0_dec_kernel(x_ref, s_ref, dnc_ref, ccc_ref, wqz_ref, wglu_ref, wba_ref, dnw_ref, alog_ref, dtb_ref, ng_ref, ccw_ref,
                       ccb_ref, cclg_ref, cclb_ref, wout_ref, g_ref, b_ref,
                       o_ref, s_out_ref, dnc_out_ref, ccc_out_ref,
                       q_scr, k_scr, v_scr, z_scr, beta_scr, eg_scr, oc_scr, *, bs):
    i = pl.program_id(0)
    nq = 3 * GDN_WIDTH

    @pl.when(i == 0)
    def _():
        xb = x_ref[...].astype(bf16)
        proj = jnp.concatenate([jnp.dot(xb, wqz_ref[...], preferred_element_type=f32),
                                jnp.dot(xb, wglu_ref[...], preferred_element_type=f32)], axis=1)
        ba = jnp.dot(xb, wba_ref[...], preferred_element_type=f32)
        qkv_raw = proj[:, :nq]
        acc = dnw_ref[SCONV_W - 1:SCONV_W, :] * qkv_raw
        for j in range(SCONV_W - 1):
            acc = acc + dnw_ref[j:j + 1, :] * dnc_ref[j]
        for j in range(SCONV_W - 2):
            dnc_out_ref[j] = dnc_ref[j + 1]
        dnc_out_ref[SCONV_W - 2] = qkv_raw
        qkv = _silu(acc)
        for h in range(GDN_HEADS):
            hs = slice(h * GDN_DK, (h + 1) * GDN_DK)
            q_scr[:, hs] = _l2n(qkv[:, h * GDN_DK:(h + 1) * GDN_DK]) * (GDN_DK ** -0.5)
            k_scr[:, hs] = _l2n(qkv[:, GDN_WIDTH + h * GDN_DK:GDN_WIDTH + (h + 1) * GDN_DK])
        v_scr[...] = qkv[:, 2 * GDN_WIDTH:]
        z_scr[...] = _silu(proj[:, nq:nq + GDN_WIDTH])
        beta, g = _gdn_gates(ba, alog_ref[...], dtb_ref[...])
        beta_scr[...] = beta
        eg_scr[...] = jnp.exp(g)

        ga = proj[:, nq + GDN_WIDTH:nq + GDN_WIDTH + CC_CH]
        gb = proj[:, nq + GDN_WIDTH + CC_CH:]
        glu = ga * jax.nn.sigmoid(gb)
        acc = ccw_ref[CC_W - 1:CC_W, :] * glu
        for j in range(CC_W - 1):
            acc = acc + ccw_ref[j:j + 1, :] * ccc_ref[j]
        for j in range(CC_W - 2):
            ccc_out_ref[j] = ccc_ref[j + 1]
        ccc_out_ref[CC_W - 2] = glu
        cc = _silu(_ln(acc + ccb_ref[...], cclg_ref[...], cclb_ref[...]))
        oc_scr[:, GDN_WIDTH:] = cc

    rows = pl.ds(pl.multiple_of(i * bs, bs), bs)
    q_blk, k_blk, v_blk, z_blk = q_scr[rows, :], k_scr[rows, :], v_scr[rows, :], z_scr[rows, :]
    beta_blk, eg_blk = beta_scr[rows, :], eg_scr[rows, :]
    o_rows = []
    for s in range(bs):
        o_heads = []
        for h in range(GDN_HEADS):
            hs = slice(h * GDN_DK, (h + 1) * GDN_DK)
            k_col = jnp.broadcast_to(k_blk[s:s + 1, hs], (GDN_DK, GDN_DK)).T
            q_col = jnp.broadcast_to(q_blk[s:s + 1, hs], (GDN_DK, GDN_DK)).T
            b1 = beta_blk[s:s + 1, h:h + 1]
            e1 = eg_blk[s:s + 1, h:h + 1]
            s_old = s_ref[s, h]
            ks = jnp.sum(k_col * s_old, 0, keepdims=True)
            v_new = b1 * (v_blk[s:s + 1, hs] - e1 * ks)
            s_new = s_old * e1 + k_col * v_new
            s_out_ref[s, h] = s_new
            o = jnp.sum(q_col * s_new, 0, keepdims=True)
            o = o * lax.rsqrt(jnp.mean(o * o, -1, keepdims=True) + 1e-6) * ng_ref[...]
            o_heads.append(o * z_blk[s:s + 1, hs])
        o_rows.append(jnp.concatenate(o_heads, axis=-1))
    oc_scr[rows, :GDN_WIDTH] = jnp.concatenate(o_rows, axis=0)

    @pl.when(i == pl.num_programs(0) - 1)
    def _():
        y = jnp.dot(oc_scr[...].astype(bf16), wout_ref[...], preferred_element_type=f32)
        o_ref[...] = _ln(DN_ALPHA * x_ref[...] + y, g_ref[MIX_LN_ROW:MIX_LN_ROW + 1, :], b_ref[MIX_LN_ROW:MIX_LN_ROW + 1, :])


def _mixer0_dec_ln(x, s, dnc, ccc, wqz, wglu, wba, dnw, alog, dtb, ng, ccw, ccb, cclg, cclb, wout, g, b, *, l, bs=8):
    N, D = x.shape
    consts = [wqz, wglu, wba, dnw, alog, dtb, ng, ccw, ccb, cclg, cclb, wout]
    sspec = pl.BlockSpec((bs, GDN_HEADS, GDN_DK, GDN_DV), lambda i: (i, 0, 0, 0))
    return pl.pallas_call(
        functools.partial(_mixer0_dec_kernel, bs=bs),
        grid=(N // bs,),
        in_specs=[_full((N, D)), sspec, _full(dnc.shape), _full(ccc.shape)] + [_full(c.shape) for c in consts]
        + [_layer_spec(g.shape, l), _layer_spec(b.shape, l)],
        out_specs=[_full((N, D)), sspec, _full(dnc.shape), _full(ccc.shape)],
        out_shape=[jax.ShapeDtypeStruct((N, D), f32), jax.ShapeDtypeStruct(s.shape, f32),
                   jax.ShapeDtypeStruct(dnc.shape, f32), jax.ShapeDtypeStruct(ccc.shape, f32)],
        scratch_shapes=[pltpu.VMEM((N, GDN_WIDTH), f32)] * 4 + [pltpu.VMEM((N, LANES), f32)] * 2
        + [pltpu.VMEM((N, GDN_WIDTH + CC_CH), f32)],
        compiler_params=_cparams("arbitrary"),
        name="mixer0_dec_ln",
    )(x, s, dnc, ccc, *consts, g, b)


def _pad_lanes(v, n=LANES):
    return jnp.zeros((1, n), f32).at[0, :v.shape[0]].set(v.astype(f32))


def kernel(x_prompt, x_sample, mem_prompt, cache_mem_k, cache_mem_v, state_dn_S, state_dn_conv, state_cc_conv, ln_g, ln_b, ffn_w_gate, ffn_w_up, ffn_w_down, xa_wq, xa_wk, xa_wv, xa_wo, ab_w_in, dn_conv_w, dn_A_log, dn_dt_bias, dn_norm_g, cc_conv_w, cc_conv_b, cc_ln_g, cc_ln_b, ab_w_out, gm_w_in, gm_ln_g, gm_ln_b, gm_w_s, gm_b_s, gm_w_out):
    B, SEQ, D = x_prompt.shape
    N = x_sample.shape[0]
    row = lambda v: v.reshape(1, -1).astype(f32)

    wg, wu, wd = ffn_w_gate.astype(bf16), ffn_w_up.astype(bf16), ffn_w_down.astype(bf16)
    wq, wk, wv, wo = xa_wq.astype(bf16), xa_wk.astype(bf16), xa_wv.astype(bf16), xa_wo.astype(bf16)
    nq = 3 * GDN_WIDTH
    nz = nq + GDN_WIDTH
    w_qz, w_glu = ab_w_in[:, :nz].astype(bf16), ab_w_in[:, nz + 2 * GDN_HEADS:].astype(bf16)
    w_ba = jnp.zeros((D, 2 * LANES), f32)
    w_ba = w_ba.at[:, :GDN_HEADS].set(ab_w_in[:, nz:nz + GDN_HEADS])
    w_ba = w_ba.at[:, LANES:LANES + GDN_HEADS].set(ab_w_in[:, nz + GDN_HEADS:nz + 2 * GDN_HEADS]).astype(bf16)
    w_out = ab_w_out.astype(bf16)
    gw_in, gw_out = gm_w_in.astype(bf16), gm_w_out.astype(bf16)
    mixer_consts = (w_qz, w_glu, w_ba, dn_conv_w.astype(f32), _pad_lanes(dn_A_log), _pad_lanes(dn_dt_bias), row(dn_norm_g),
                    cc_conv_w.astype(f32), row(cc_conv_b), row(cc_ln_g), row(cc_ln_b), w_out)
    gm_consts = (gw_in, gm_w_s.astype(f32), gm_b_s.astype(f32)[:, :, None], row(gm_ln_g), row(gm_ln_b), gw_out)
    lng, lnb = ln_g.astype(f32), ln_b.astype(f32)

    mem_k, mem_v, mem_kh, mem_vh = _mem_kv(mem_prompt, wk, wv)

    yp, ys = x_prompt.reshape(B * SEQ, D), x_sample.reshape(N, D)
    state = (state_dn_S, jnp.swapaxes(state_dn_conv, 0, 1), jnp.swapaxes(state_cc_conv, 0, 1))
    for l in range(DEPTH):
        yp, ys = _ffn_ln(yp, wg, wu, wd, lng, lnb, l=l, half=0, tm=FFN_ROWS, xs=ys)
        if l % 2 == 0:
            yp, dn_s_p, dn_c_p, cc_c_p = _mixer0_ln(yp, *mixer_consts, lng, lnb, l=l, batch=B, seq=SEQ)
            ys, dn_s_s, dn_c_s, cc_c_s = _mixer0_dec_ln(ys, *state, *mixer_consts, lng, lnb, l=l)
        else:
            yp, _ = _gmlp_ln(yp, *gm_consts, lng, lnb, l=l, tm=GMLP_ROWS, single=False)
            ys, gm_v_s = _gmlp_ln(ys, *gm_consts, lng, lnb, l=l, tm=N, single=True)
        yp = _xattn_ln(yp, mem_kh, mem_vh, wq, wo, lng, lnb, l=l, seq=SEQ)
        yp, ys = _ffn_ln(yp, wg, wu, wd, lng, lnb, l=l, half=1, tm=FFN_ATTEND_ROWS, xs=ys,
                         attend=(cache_mem_k, cache_mem_v, wq, wo))

    return (yp.reshape(B, SEQ, D), ys.reshape(N, 1, D), mem_k, mem_v,
            dn_s_p, dn_c_p[:, DN_TAIL - (SCONV_W - 1):], cc_c_p[:, CC_TAIL - (CC_W - 1):],
            dn_s_s, jnp.swapaxes(dn_c_s, 0, 1), jnp.swapaxes(cc_c_s, 0, 1), gm_v_s.reshape(N, 1, GM_WIDTH))
```

```python
import functools

import jax
import jax.numpy as jnp
from jax import lax
from jax.experimental import pallas as pl
from jax.experimental.pallas import tpu as pltpu

D_MODEL = 1024
DEPTH = 2
DN_ALPHA = (2 * DEPTH) ** 0.25
LN_EPS = 1e-5
GDN_HEADS = 4
GDN_DK = 128
GDN_DV = 128
GDN_WIDTH = GDN_HEADS * GDN_DK
SCONV_W = 4
CC_CH = D_MODEL // 2
CC_W = 31
GM_WIDTH = D_MODEL
GM_GROUPS = 4
GM_CHUNK = 128
N_MEM = 256
XA_HEADS = 4
XA_HEAD_DIM = D_MODEL // XA_HEADS
D_FF = 2816
MIX_LN_ROW = 1
XA_LN_ROW = 2

LANES = 128
SUBLANES = 8
GDN_BLOCK = 128
DN_TAIL = SUBLANES
CC_TAIL = 32
CC_ROWS = 64
DN_ROWS = 32
FFN_ROWS = 1024
GMLP_ROWS = 1024
FFN_ATTEND_ROWS = 512
VMEM_LIMIT = 56 * 1024 * 1024

bf16 = jnp.bfloat16
f32 = jnp.float32


def _cparams(*sem):
    return pltpu.CompilerParams(dimension_semantics=sem, vmem_limit_bytes=VMEM_LIMIT)


def _full(shape):
    n = len(shape)
    return pl.BlockSpec(shape, lambda *_: (0,) * n)


def _dot_nt(a, b):
    return lax.dot_general(a.astype(bf16), b.astype(bf16), (((1,), (1,)), ((), ())), preferred_element_type=f32)


def _split3(a):
    a1 = a.astype(bf16)
    r = a - a1.astype(f32)
    a2 = r.astype(bf16)
    a3 = (r - a2.astype(f32)).astype(bf16)
    return a1, a2, a3


def _dot_exact_lhs(a_exact_bf16, b):
    b1, b2, b3 = _split3(b)
    d = lambda y: jnp.dot(a_exact_bf16, y, preferred_element_type=f32)
    return d(b1) + (d(b2) + d(b3))


def _ln(y, g, b):
    mu = jnp.mean(y, -1, keepdims=True)
    d = y - mu
    var = jnp.mean(d * d, -1, keepdims=True)
    return d * lax.rsqrt(var + LN_EPS) * g + b


def _silu(x):
    return x * jax.nn.sigmoid(x)


def _softplus(x):
    return jnp.maximum(x, 0.0) + jnp.log(1.0 + jnp.exp(-jnp.abs(x)))


def _decode_attend(q_rows, k_ref, v_ref):
    grp = SUBLANES // XA_HEADS
    o_rows = []
    for s in range(q_rows.shape[0]):
        q4 = jnp.concatenate([q_rows[s:s + 1, h * XA_HEAD_DIM:(h + 1) * XA_HEAD_DIM] for h in range(XA_HEADS)], axis=0)
        q8 = jnp.concatenate([q4] * grp, axis=0)
        both = lambda t: sum(t[j * XA_HEADS:(j + 1) * XA_HEADS] for j in range(grp))
        k3 = k_ref[s].reshape(N_MEM // grp, grp * XA_HEADS, XA_HEAD_DIM)
        v3 = v_ref[s].reshape(N_MEM // grp, grp * XA_HEADS, XA_HEAD_DIM)
        sc = jnp.sum(k3 * q8[None], -1, keepdims=True) * (XA_HEAD_DIM ** -0.5)
        mx = jnp.max(sc, 0)
        mx = functools.reduce(jnp.maximum, [mx[j * XA_HEADS:(j + 1) * XA_HEADS] for j in range(grp)])
        e = jnp.exp(sc - jnp.concatenate([mx] * grp, axis=0)[None])
        den = both(jnp.sum(e, 0))
        p = e / jnp.concatenate([den] * grp, axis=0)[None]
        o4 = both(jnp.sum(p * v3, 0))
        o_rows.append(jnp.concatenate([o4[h:h + 1, :] for h in range(XA_HEADS)], axis=-1))
    return jnp.concatenate(o_rows, axis=0)


def _ffn_kernel(*refs, ln_row, rows, tf, samples, attend):
    x_ref, wg_ref, wu_ref, wd_ref, g_ref, b_ref = refs[:6]
    pos = 6
    if samples:
        xs_ref = refs[pos]
        pos += 1
    if attend:
        ck_ref, cv_ref, wq_ref, wo_ref = refs[pos:pos + 4]
        pos += 4
    o_ref = refs[pos]
    pos += 1
    if samples:
        os_ref = refs[pos]
        pos += 1
    if attend:
        q_scr, a_scr = refs[pos:pos + 2]
    i = pl.program_id(0)
    r = slice(ln_row, ln_row + 1)
    xr = slice(XA_LN_ROW, XA_LN_ROW + 1)
    nf = wg_ref.shape[1] // tf
    norm = lambda y: _ln(y, g_ref[r, :], b_ref[r, :])

    def pre_norm(x):
        xb = x.astype(bf16)
        acc = None
        for j in range(nf):
            cs = slice(j * tf, (j + 1) * tf)
            hg = jnp.dot(xb, wg_ref[:, cs], preferred_element_type=f32)
            hu = jnp.dot(xb, wu_ref[:, cs], preferred_element_type=f32)
            d = jnp.dot((_silu(hg) * hu).astype(bf16), wd_ref[cs, :], preferred_element_type=f32)
            acc = d if acc is None else acc + d
        return DN_ALPHA * x + 0.5 * acc

    def attend_step():
        bs = ck_ref.shape[0]
        share = max(SUBLANES // bs, 1)
        nrow = bs * share
        blk = pl.ds(pl.multiple_of((i // share) * nrow, nrow), nrow)
        q_blk = q_scr[blk, :]
        sub = i % share
        q_rows = q_blk[:bs]
        for j in range(1, share):
            q_rows = jnp.where(sub == j, q_blk[j * bs:(j + 1) * bs], q_rows)
        o_rows = _decode_attend(q_rows, ck_ref, cv_ref)
        if share > 1:
            a_blk = a_scr[blk, :]
            o_rows = jnp.concatenate([jnp.where(sub == j, o_rows, a_blk[j * bs:(j + 1) * bs]) for j in range(share)], axis=0)
        a_scr[blk, :] = o_rows

    def sample_rows():
        xs = xs_ref[...]
        if attend:
            att = jnp.dot(a_scr[...].astype(bf16), wo_ref[...], preferred_element_type=f32)
            xs = _ln(DN_ALPHA * xs + att, g_ref[xr, :], b_ref[xr, :])
        os_ref[...] = norm(pre_norm(xs))

    if attend:
        @pl.when(i == 0)
        def _():
            q_scr[...] = jnp.dot(xs_ref[...].astype(bf16), wq_ref[...], preferred_element_type=f32)
            a_scr[...] = jnp.zeros_like(a_scr)

    for r0 in range(0, x_ref.shape[0], rows):
        o_ref[r0:r0 + rows, :] = norm(pre_norm(x_ref[r0:r0 + rows, :]))
    if attend:
        attend_step()
    if samples:
        pl.when(i == pl.num_programs(0) - 1)(sample_rows)


def _ffn_ln(x, wg, wu, wd, g, b, *, l, half, tm, rows=512, tf=256, xs=None, attend=None):
    T, D = x.shape
    tm = min(tm, T)
    rows = min(rows, tm)
    F = wg.shape[-1]
    nln = g.shape[1]
    steps = T // tm
    once = pl.Buffered(1)
    xspec = pl.BlockSpec((tm, D), lambda i: (i, 0))
    operands = [x, wg, wu, wd, g, b]
    in_specs = [
        xspec,
        pl.BlockSpec((None, None, D, F), lambda i: (l, half, 0, 0), pipeline_mode=once),
        pl.BlockSpec((None, None, D, F), lambda i: (l, half, 0, 0), pipeline_mode=once),
        pl.BlockSpec((None, None, F, D), lambda i: (l, half, 0, 0), pipeline_mode=once),
        pl.BlockSpec((None, nln, D), lambda i: (l, 0, 0)),
        pl.BlockSpec((None, nln, D), lambda i: (l, 0, 0)),
    ]
    out_specs, out_shape, scratch = [xspec], [jax.ShapeDtypeStruct((T, D), f32)], []
    if xs is not None:
        N = xs.shape[0]
        operands.append(xs)
        in_specs.append(_full((N, D)))
        out_specs.append(_full((N, D)))
        out_shape.append(jax.ShapeDtypeStruct((N, D), f32))
    if attend is not None:
        ck, cv, wq, wo = attend
        assert N % steps == 0
        bs = N // steps
        assert bs % SUBLANES == 0 or SUBLANES % bs == 0
        cspec = pl.BlockSpec((None, bs, N_MEM, XA_HEADS, XA_HEAD_DIM), lambda i: (l, i, 0, 0, 0))
        wspec = pl.BlockSpec((None, D, D), lambda i: (l, 0, 0), pipeline_mode=once)
        operands += [ck, cv, wq, wo]
        in_specs += [cspec, cspec, wspec, wspec]
        scratch = [pltpu.VMEM((N, D), f32), pltpu.VMEM((N, D), f32)]
    res = pl.pallas_call(
        functools.partial(_ffn_kernel, ln_row=(nln - 1) * half, rows=rows, tf=tf, samples=xs is not None,
                          attend=attend is not None),
        grid=(steps,),
        in_specs=in_specs,
        out_specs=out_specs,
        out_shape=out_shape,
        scratch_shapes=scratch,
        compiler_params=_cparams("parallel" if xs is None else "arbitrary"),
        name="ffn_ln",
    )(*operands)
    return res[0] if xs is None else res


def _memkv_kernel(m_ref, wk_ref, wv_ref, k_ref, v_ref, kh_ref, vh_ref):
    nb = m_ref.shape[0]
    for b in range(nb):
        mb = m_ref[b].astype(bf16)
        k = jnp.dot(mb, wk_ref[...], preferred_element_type=f32)
        v = jnp.dot(mb, wv_ref[...], preferred_element_type=f32)
        for h in range(XA_HEADS):
            sl = slice(h * XA_HEAD_DIM, (h + 1) * XA_HEAD_DIM)
            k_ref[b, :, h, :] = k[:, sl]
            v_ref[b, :, h, :] = v[:, sl]
            kh_ref[b, h] = k[:, sl].astype(bf16)
            vh_ref[b, h] = v[:, sl].astype(bf16)


def _mem_kv(mem, wk, wv, *, nb=2):
    B, M, D = mem.shape
    nb = min(nb, B)
    L = wk.shape[0]
    out = jax.ShapeDtypeStruct((L, B, M, XA_HEADS, XA_HEAD_DIM), f32)
    outh = jax.ShapeDtypeStruct((L, B, XA_HEADS, M, XA_HEAD_DIM), bf16)
    ospec = pl.BlockSpec((None, nb, M, XA_HEADS, XA_HEAD_DIM), lambda l, i: (l, i, 0, 0, 0))
    hspec = pl.BlockSpec((None, nb, XA_HEADS, M, XA_HEAD_DIM), lambda l, i: (l, i, 0, 0, 0))
    return pl.pallas_call(
        _memkv_kernel,
        grid=(L, B // nb),
        in_specs=[
            pl.BlockSpec((nb, M, D), lambda l, i: (i, 0, 0)),
            pl.BlockSpec((None, D, D), lambda l, i: (l, 0, 0)),
            pl.BlockSpec((None, D, D), lambda l, i: (l, 0, 0)),
        ],
        out_specs=[ospec, ospec, hspec, hspec],
        out_shape=[out, out, outh, outh],
        compiler_params=_cparams("parallel", "parallel"),
        name="mem_kv",
    )(mem, wk, wv)


def _xattn_kernel(x_ref, k_ref, v_ref, wq_ref, wo_ref, g_ref, b_ref, o_ref, oh_ref, *, rows):
    starts = list(range(0, x_ref.shape[0], rows))
    proj = lambda r: jnp.dot(x_ref[r:r + rows, :].astype(bf16), wq_ref[...], preferred_element_type=f32).astype(bf16)
    nxt = proj(starts[0])
    for i, r0 in enumerate(starts):
        rs = slice(r0, r0 + rows)
        x = x_ref[rs, :]
        q = nxt
        if i + 1 < len(starts):
            nxt = proj(starts[i + 1])
        for h in range(XA_HEADS):
            sl = slice(h * XA_HEAD_DIM, (h + 1) * XA_HEAD_DIM)
            s = _dot_nt(q[:, sl], k_ref[h]) * (XA_HEAD_DIM ** -0.5)
            s = s - jnp.max(s, -1, keepdims=True)
            e = jnp.exp(s)
            p = e / jnp.sum(e, -1, keepdims=True)
            oh_ref[rs, sl] = jnp.dot(p.astype(bf16), v_ref[h], preferred_element_type=f32).astype(bf16)
        att = jnp.dot(oh_ref[rs, :], wo_ref[...], preferred_element_type=f32)
        o_ref[rs, :] = _ln(DN_ALPHA * x + att, g_ref[XA_LN_ROW:XA_LN_ROW + 1, :], b_ref[XA_LN_ROW:XA_LN_ROW + 1, :])


def _layer_spec(shape, l):
    n = len(shape)
    return pl.BlockSpec((None,) + tuple(shape[1:]), lambda *_: (l,) + (0,) * (n - 1))


def _xattn_ln(x, mk, mv, wq, wo, g, b, *, l, seq, tq=2048, rows=512):
    T, D = x.shape
    tq = min(tq, seq)
    rows = min(rows, tq)
    nq = seq // tq
    mspec = pl.BlockSpec((None, None, XA_HEADS, N_MEM, XA_HEAD_DIM), lambda i: (l, i // nq, 0, 0, 0))
    return pl.pallas_call(
        functools.partial(_xattn_kernel, rows=rows),
        grid=(T // tq,),
        in_specs=[
            pl.BlockSpec((tq, D), lambda i: (i, 0)),
            mspec, mspec,
            _layer_spec(wq.shape, l), _layer_spec(wo.shape, l), _layer_spec(g.shape, l), _layer_spec(b.shape, l),
        ],
        out_specs=pl.BlockSpec((tq, D), lambda i: (i, 0)),
        out_shape=jax.ShapeDtypeStruct((T, D), f32),
        scratch_shapes=[pltpu.VMEM((tq, D), bf16)],
        compiler_params=_cparams("parallel"),
        name="xattn_ln",
    )(x, mk, mv, wq, wo, g, b)


def _gmlp_kernel(x_ref, win_ref, ws_ref, bs_ref, vg_ref, vb_ref, wout_ref, g_ref, b_ref, o_ref, *rest, single, rows):
    v_ref, uf_ref = rest if single else (None, rest[0])
    gw = GM_WIDTH // GM_GROUPS
    if not single:
        row = lax.broadcasted_iota(jnp.int32, (GM_CHUNK, GM_CHUNK), 0)
        col = lax.broadcasted_iota(jnp.int32, (GM_CHUNK, GM_CHUNK), 1)
        wmask = [jnp.where(col <= row, ws_ref[g], 0.0).astype(bf16) for g in range(GM_GROUPS)]
    starts = list(range(0, x_ref.shape[0], rows))
    proj = lambda r: jnp.dot(x_ref[r:r + rows, :].astype(bf16), win_ref[...], preferred_element_type=f32)
    nxt = proj(starts[0])
    for i, r0 in enumerate(starts):
        x = x_ref[r0:r0 + rows, :]
        cur = nxt
        if i + 1 < len(starts):
            nxt = proj(starts[i + 1])
        pr = jax.nn.gelu(cur)
        u = pr[:, :GM_WIDTH]
        v = _ln(pr[:, GM_WIDTH:], vg_ref[...], vb_ref[...])
        if single:
            v_ref[r0:r0 + rows, :] = v
            for g in range(GM_GROUPS):
                sl = slice(g * gw, (g + 1) * gw)
                f = ws_ref[g][0:1, 0:1] * v[:, sl] + bs_ref[g][0:1, 0:1]
                uf_ref[r0:r0 + rows, sl] = (u[:, sl] * f).astype(bf16)
        else:
            vb16 = v.astype(bf16)
            for g in range(GM_GROUPS):
                sl = slice(g * gw, (g + 1) * gw)
                for c in range(rows // GM_CHUNK):
                    cs = slice(c * GM_CHUNK, (c + 1) * GM_CHUNK)
                    f = jnp.dot(wmask[g], vb16[cs, sl], preferred_element_type=f32) + bs_ref[g]
                    uf_ref[r0 + c * GM_CHUNK:r0 + (c + 1) * GM_CHUNK, sl] = (u[cs, sl] * f).astype(bf16)
        y = jnp.dot(uf_ref[r0:r0 + rows, :], wout_ref[...], preferred_element_type=f32)
        o_ref[r0:r0 + rows, :] = _ln(DN_ALPHA * x + y, g_ref[MIX_LN_ROW:MIX_LN_ROW + 1, :], b_ref[MIX_LN_ROW:MIX_LN_ROW + 1, :])


def _gmlp_ln(x, w_in, w_s, b_s, vg, vb, w_out, g, b, *, l, tm, single, rows=512):
    T, D = x.shape
    tm = min(tm, T)
    rows = min(rows, tm)
    xspec = pl.BlockSpec((tm, D), lambda i: (i, 0))
    out = jax.ShapeDtypeStruct((T, D), f32)
    res = pl.pallas_call(
        functools.partial(_gmlp_kernel, single=single, rows=rows),
        grid=(T // tm,),
        in_specs=[
            xspec, _full((D, 2 * GM_WIDTH)), _full(w_s.shape), _full(b_s.shape),
            _full((1, GM_WIDTH)), _full((1, GM_WIDTH)), _full((GM_WIDTH, D)), _layer_spec(g.shape, l), _layer_spec(b.shape, l),
        ],
        out_specs=[xspec, pl.BlockSpec((tm, GM_WIDTH), lambda i: (i, 0))] if single else xspec,
        out_shape=[out, jax.ShapeDtypeStruct((T, GM_WIDTH), f32)] if single else out,
        scratch_shapes=[pltpu.VMEM((tm, GM_WIDTH), bf16)],
        compiler_params=_cparams("parallel"),
        name="gmlp_ln",
    )(x, w_in, w_s, b_s, vg, vb, w_out, g, b)
    return res if single else (res, None)


def _gdn_gates(ba, alog, dtb):
    beta = jax.nn.sigmoid(ba[:, :LANES])
    g = -jnp.exp(alog) * _softplus(ba[:, LANES:] + dtb)
    return beta, g


def _l2n(x):
    return x * lax.rsqrt(jnp.sum(x * x, -1, keepdims=True) + 1e-6)


def _mixer0_kernel(x_ref, wqz_ref, wglu_ref, wba_ref, dnw_ref, alog_ref, dtb_ref, ng_ref, ccw_ref, ccb_ref, cclg_ref,
                   cclb_ref, wout_ref, g_ref, b_ref,
                   o_ref, s_out_ref, dnc_out_ref, ccc_out_ref,
                   qkv_ext, glu_ext, rot_scr, s_scr, oc_scr):
    blk = pl.program_id(1)
    tb = x_ref.shape[0]
    C = GDN_BLOCK

    @pl.when(blk == 0)
    def _():
        qkv_ext[0:DN_TAIL, :] = jnp.zeros((DN_TAIL, 3 * GDN_WIDTH), f32)
        glu_ext[0:CC_TAIL, :] = jnp.zeros((CC_TAIL, CC_CH), f32)
        s_scr[...] = jnp.zeros_like(s_scr)

    x = x_ref[...]
    nq = 3 * GDN_WIDTH
    xb = x.astype(bf16)
    ba = jnp.dot(xb, wba_ref[...], preferred_element_type=f32)
    proj = jnp.dot(xb, wqz_ref[...], preferred_element_type=f32)

    qkv_ext[DN_TAIL:DN_TAIL + tb, :] = proj[:, :nq]
    blocks = []
    for i in range(tb // DN_ROWS):
        acc = None
        for s in range(SCONV_W):
            term = dnw_ref[SCONV_W - 1 - s:SCONV_W - s, :] * qkv_ext[pl.ds(DN_TAIL - s + i * DN_ROWS, DN_ROWS), :]
            acc = term if acc is None else acc + term
        blocks.append(_silu(acc))
    qkv = jnp.concatenate(blocks, axis=0)
    tail = qkv_ext[tb:tb + DN_TAIL, :]
    qkv_ext[0:DN_TAIL, :] = tail
    dnc_out_ref[0] = tail

    beta, g = _gdn_gates(ba, alog_ref[...], dtb_ref[...])

    P = 2 * GDN_DK
    pairs = range(GDN_HEADS // 2)
    chunks = range(tb // C)
    row = lax.broadcasted_iota(jnp.int32, (C, P), 0)
    col = lax.broadcasted_iota(jnp.int32, (C, P), 1)
    col = jnp.where(col >= C, col - C, col)
    causal = col <= row
    strict = col < row
    ltri = jnp.where(causal[:, :C], 1.0, 0.0).astype(bf16)
    qn = [_l2n(qkv[:, h * GDN_DK:(h + 1) * GDN_DK]) * (GDN_DK ** -0.5) for h in range(GDN_HEADS)]
    kn = [_l2n(qkv[:, GDN_WIDTH + h * GDN_DK:GDN_WIDTH + (h + 1) * GDN_DK]) for h in range(GDN_HEADS)]

    def pair_cols(m, h0):
        return jnp.concatenate([jnp.broadcast_to(m[:, h0:h0 + 1], (C, GDN_DK)),
                                jnp.broadcast_to(m[:, h0 + 1:h0 + 2], (C, GDN_DK))], axis=1)

    def bdiag(m):
        z = jnp.zeros((C, C), m.dtype)
        return jnp.concatenate([jnp.concatenate([m[:, :C], z], axis=1), jnp.concatenate([z, m[:, C:]], axis=1)], axis=0)

    def split2(a):
        a1 = a.astype(bf16)
        return a1, (a - a1.astype(f32)).astype(bf16)

    mm = lambda a, b_: jnp.dot(a, b_, preferred_element_type=f32)

    prob = {}
    for c in chunks:
        rs = slice(c * C, (c + 1) * C)
        gc = _dot_exact_lhs(ltri, g[rs])
        gct = gc.T
        eg = jnp.exp(gc)
        g_last = gc[C - 1:C, :]
        ekt = jnp.exp(g_last - gc)
        egl = jnp.exp(g_last)
        for pr in pairs:
            h0 = 2 * pr
            k_pair = jnp.concatenate([kn[h0][rs], kn[h0 + 1][rs]], axis=1)
            q_pair = jnp.concatenate([qn[h0][rs], qn[h0 + 1][rs]], axis=1)
            v_pair = qkv[rs, 2 * GDN_WIDTH + pr * P:2 * GDN_WIDTH + (pr + 1) * P]
            beta_pair = pair_cols(beta[rs], h0)
            grow = jnp.concatenate([gct[h0:h0 + 1, :], gct[h0 + 1:h0 + 2, :]], axis=1)
            decay = jnp.where(causal, jnp.exp(jnp.where(causal, pair_cols(gc, h0) - grow, 0.0)), 0.0)
            kb = k_pair * beta_pair
            prod = _dot_nt(jnp.concatenate([kb, q_pair], axis=0), bdiag(k_pair.astype(bf16)))
            eg_pair = pair_cols(eg, h0)
            kt = k_pair * pair_cols(ekt, h0)
            prob[c, pr] = dict(
                n=-jnp.where(strict, prod[:C] * decay, 0.0),
                qk=(prod[C:] * decay).astype(bf16),
                vb=(v_pair * beta_pair).astype(bf16),
                kbe=(kb * eg_pair).astype(bf16),
                qg=(q_pair * eg_pair).astype(bf16),
                ktt=jnp.concatenate([kt[:, :C].T, kt[:, C:].T], axis=1).astype(bf16),
                egl=jnp.concatenate([jnp.broadcast_to(egl[:, h0:h0 + 1], (1, GDN_DK)),
                                     jnp.broadcast_to(egl[:, h0 + 1:h0 + 2], (1, GDN_DK))], axis=1))

    glu_in = jnp.dot(xb, wglu_ref[...], preferred_element_type=f32)
    glu_ext[CC_TAIL:CC_TAIL + tb, :] = glu_in[:, :CC_CH] * jax.nn.sigmoid(glu_in[:, CC_CH:])
    span = CC_TAIL - SUBLANES
    for r in range(1, SUBLANES):
        rot_scr[r - 1] = glu_ext[pl.ds(SUBLANES - r, tb + span), :]

    def conv_rows(base):
        acc = None
        for s in range(CC_W):
            a, r = divmod(s, SUBLANES)
            off = span - SUBLANES * a
            if r == 0:
                xs = glu_ext[base + SUBLANES + off:base + SUBLANES + off + CC_ROWS, :]
            else:
                xs = rot_scr[r - 1, base + off:base + off + CC_ROWS, :]
            term = ccw_ref[CC_W - 1 - s:CC_W - s, :] * xs
            acc = term if acc is None else acc + term
        cc = _silu(_ln(acc + ccb_ref[...], cclg_ref[...], cclb_ref[...]))
        oc_scr[base:base + CC_ROWS, GDN_WIDTH:] = cc.astype(bf16)


    Hc = C // 2
    lane = lax.broadcasted_iota(jnp.int32, (Hc, P), 1)
    prow = lax.broadcasted_iota(jnp.int32, (Hc, P), 0)
    first_half = jnp.where(lane >= C, lane - C, lane) < Hc
    eye4 = jnp.where((lane & (Hc - 1)) == prow, 1.0, 0.0)
    quarter = [(lane >= q * Hc) & (lane < (q + 1) * Hc) for q in range(P // Hc)]

    def bdiag4(m):
        return jnp.concatenate([jnp.where(qm, m, jnp.zeros_like(m)) for qm in quarter], axis=0)

    def dot3(l1, l2, w1, w2):
        r = l1.shape[0]
        o = mm(jnp.concatenate([l1, l2], axis=0), w1)
        return o[:r] + (o[r:] + mm(l1, w2))

    nk = {key: jnp.where(first_half, pb["n"][:Hc], pb["n"][Hc:]) for key, pb in prob.items()}
    pk = {key: eye4 + nk[key] for key in prob}
    levels = Hc.bit_length() - 1
    for j in range(levels):
        first, last = j == 0, j == levels - 1
        for key in prob:
            n1, n2 = split2(nk[key])
            if first:
                l1, l2 = n1, n2
            else:
                p1, p2 = split2(pk[key])
                l1 = p1 if last else jnp.concatenate([n1, p1], axis=0)
                l2 = p2 if last else jnp.concatenate([n2, p2], axis=0)
            res = dot3(l1, l2, bdiag4(n1), bdiag4(n2))
            if first:
                nk[key] = res
            elif last:
                pk[key] = pk[key] + res
            else:
                nk[key] = res[:Hc]
                pk[key] = pk[key] + res[Hc:]
    crow = lax.broadcasted_iota(jnp.int32, (C, P), 0)
    c_block = (crow >= Hc) & (col < Hc)
    for key, pb in prob.items():
        t_pan = pk[key]
        t1, t2 = split2(t_pan)
        c1, c2 = split2(jnp.where(c_block, pb["n"], 0.0))
        x1, x2 = split2(dot3(t1, t2, bdiag(c1), bdiag(c2)))
        zero = jnp.zeros_like(t1)
        low = dot3(x1, x2, bdiag4(jnp.where(first_half, t1, zero)), bdiag4(jnp.where(first_half, t2, zero)))
        pk[key] = jnp.concatenate([jnp.where(first_half, t_pan, 0.0), low + jnp.where(first_half, 0.0, t_pan)], axis=0)

    for key, pb in prob.items():
        t16 = pk[key].astype(bf16)
        pb["u"] = mm(t16, bdiag(pb["vb"]))
        pb["w"] = mm(t16, bdiag(pb["kbe"])).astype(bf16)

    for c in chunks:
        rs = slice(c * C, (c + 1) * C)
        for pr in pairs:
            pb = prob[c, pr]
            s_pair = s_scr[pr]
            o2 = mm(jnp.concatenate([pb["w"], pb["qg"]], axis=0), bdiag(s_pair.astype(bf16)))
            vbd = bdiag((pb["u"] - o2[:C]).astype(bf16))
            o = o2[C:] + mm(pb["qk"], vbd)
            s_scr[pr] = s_pair * pb["egl"] + mm(pb["ktt"], vbd)
            on = [o[:, j * GDN_DV:(j + 1) * GDN_DV] for j in range(2)]
            on = [t * lax.rsqrt(jnp.mean(t * t, -1, keepdims=True) + 1e-6) * ng_ref[...] for t in on]
            z_pair = proj[rs, nq + pr * P:nq + (pr + 1) * P]
            oc_scr[rs, pr * P:(pr + 1) * P] = (jnp.concatenate(on, axis=1) * _silu(z_pair)).astype(bf16)

    for pr in pairs:
        s_out_ref[0, 2 * pr] = s_scr[pr][:, :GDN_DV]
        s_out_ref[0, 2 * pr + 1] = s_scr[pr][:, GDN_DV:]

    for base in range(0, tb, CC_ROWS):
        conv_rows(base)
    tail = glu_ext[tb:tb + CC_TAIL, :]
    glu_ext[0:CC_TAIL, :] = tail
    ccc_out_ref[0] = tail

    y = jnp.dot(oc_scr[...], wout_ref[...], preferred_element_type=f32)
    o_ref[...] = _ln(DN_ALPHA * x + y, g_ref[MIX_LN_ROW:MIX_LN_ROW + 1, :], b_ref[MIX_LN_ROW:MIX_LN_ROW + 1, :])


def _mixer0_ln(x, wqz, wglu, wba, dnw, alog, dtb, ng, ccw, ccb, cclg, cclb, wout, g, b, *, l, batch, seq, tb=512):
    T, D = x.shape
    tb = min(tb, seq)
    nb = seq // tb
    consts = [wqz, wglu, wba, dnw, alog, dtb, ng, ccw, ccb, cclg, cclb, wout]
    return pl.pallas_call(
        _mixer0_kernel,
        grid=(batch, nb),
        in_specs=[pl.BlockSpec((tb, D), lambda i, j: (i * nb + j, 0))] + [_full(c.shape) for c in consts]
        + [_layer_spec(g.shape, l), _layer_spec(b.shape, l)],
        out_specs=[
            pl.BlockSpec((tb, D), lambda i, j: (i * nb + j, 0)),
            pl.BlockSpec((1, GDN_HEADS, GDN_DK, GDN_DV), lambda i, j: (i, 0, 0, 0)),
            pl.BlockSpec((1, DN_TAIL, 3 * GDN_WIDTH), lambda i, j: (i, 0, 0)),
            pl.BlockSpec((1, CC_TAIL, CC_CH), lambda i, j: (i, 0, 0)),
        ],
        out_shape=[
            jax.ShapeDtypeStruct((T, D), f32),
            jax.ShapeDtypeStruct((batch, GDN_HEADS, GDN_DK, GDN_DV), f32),
            jax.ShapeDtypeStruct((batch, DN_TAIL, 3 * GDN_WIDTH), f32),
            jax.ShapeDtypeStruct((batch, CC_TAIL, CC_CH), f32),
        ],
        scratch_shapes=[
            pltpu.VMEM((tb + DN_TAIL, 3 * GDN_WIDTH), f32),
            pltpu.VMEM((tb + CC_TAIL, CC_CH), f32),
            pltpu.VMEM((SUBLANES - 1, tb + CC_TAIL - SUBLANES, CC_CH), f32),
            pltpu.VMEM((GDN_HEADS // 2, GDN_DK, 2 * GDN_DV), f32),
            pltpu.VMEM((tb, GDN_WIDTH + CC_CH), bf16),
        ],
        compiler_params=_cparams("parallel", "arbitrary"),
        name="mixer0_ln",
    )(x, *consts, g, b)


def _mixer0_dec_kernel(x_ref, s_ref, dnc_ref, ccc_ref, wqz_ref, wglu_ref, wba_ref, dnw_ref, alog_ref, dtb_ref, ng_ref, ccw_ref,
                       ccb_ref, cclg_ref, cclb_ref, wout_ref, g_ref, b_ref,
                       o_ref, s_out_ref, dnc_out_ref, ccc_out_ref,
                       q_scr, k_scr, v_scr, z_scr, beta_scr, eg_scr, oc_scr, *, bs):
    i = pl.program_id(0)
    nq = 3 * GDN_WIDTH

    @pl.when(i == 0)
    def _():
        xb = x_ref[...].astype(bf16)
        proj = jnp.concatenate([jnp.dot(xb, wqz_ref[...], preferred_element_type=f32),
                                jnp.dot(xb, wglu_ref[...], preferred_element_type=f32)], axis=1)
        ba = jnp.dot(xb, wba_ref[...], preferred_element_type=f32)
        qkv_raw = proj[:, :nq]
        acc = dnw_ref[SCONV_W - 1:SCONV_W, :] * qkv_raw
        for j in range(SCONV_W - 1):
            acc = acc + dnw_ref[j:j + 1, :] * dnc_ref[j]
        for j in range(SCONV_W - 2):
            dnc_out_ref[j] = dnc_ref[j + 1]
        dnc_out_ref[SCONV_W - 2] = qkv_raw
        qkv = _silu(acc)
        for h in range(GDN_HEADS):
            hs = slice(h * GDN_DK, (h + 1) * GDN_DK)
            q_scr[:, hs] = _l2n(qkv[:, h * GDN_DK:(h + 1) * GDN_DK]) * (GDN_DK ** -0.5)
            k_scr[:, hs] = _l2n(qkv[:, GDN_WIDTH + h * GDN_DK:GDN_WIDTH + (h + 1) * GDN_DK])
        v_scr[...] = qkv[:, 2 * GDN_WIDTH:]
        z_scr[...] = _silu(proj[:, nq:nq + GDN_WIDTH])
        beta, g = _gdn_gates(ba, alog_ref[...], dtb_ref[...])
        beta_scr[...] = beta
        eg_scr[...] = jnp.exp(g)

        ga = proj[:, nq + GDN_WIDTH:nq + GDN_WIDTH + CC_CH]
        gb = proj[:, nq + GDN_WIDTH + CC_CH:]
        glu = ga * jax.nn.sigmoid(gb)
        acc = ccw_ref[CC_W - 1:CC_W, :] * glu
        for j in range(CC_W - 1):
            acc = acc + ccw_ref[j:j + 1, :] * ccc_ref[j]
        for j in range(CC_W - 2):
            ccc_out_ref[j] = ccc_ref[j + 1]
        ccc_out_ref[CC_W - 2] = glu
        cc = _silu(_ln(acc + ccb_ref[...], cclg_ref[...], cclb_ref[...]))
        oc_scr[:, GDN_WIDTH:] = cc

    rows = pl.ds(pl.multiple_of(i * bs, bs), bs)
    q_blk, k_blk, v_blk, z_blk = q_scr[rows, :], k_scr[rows, :], v_scr[rows, :], z_scr[rows, :]
    beta_blk, eg_blk = beta_scr[rows, :], eg_scr[rows, :]
    o_rows = []
    for s in range(bs):
        o_heads = []
        for h in range(GDN_HEADS):
            hs = slice(h * GDN_DK, (h + 1) * GDN_DK)
            k_col = jnp.broadcast_to(k_blk[s:s + 1, hs], (GDN_DK, GDN_DK)).T
            q_col = jnp.broadcast_to(q_blk[s:s + 1, hs], (GDN_DK, GDN_DK)).T
            b1 = beta_blk[s:s + 1, h:h + 1]
            e1 = eg_blk[s:s + 1, h:h + 1]
            s_old = s_ref[s, h]
            ks = jnp.sum(k_col * s_old, 0, keepdims=True)
            v_new = b1 * (v_blk[s:s + 1, hs] - e1 * ks)
            s_new = s_old * e1 + k_col * v_new
            s_out_ref[s, h] = s_new
            o = jnp.sum(q_col * s_new, 0, keepdims=True)
            o = o * lax.rsqrt(jnp.mean(o * o, -1, keepdims=True) + 1e-6) * ng_ref[...]
            o_heads.append(o * z_blk[s:s + 1, hs])
        o_rows.append(jnp.concatenate(o_heads, axis=-1))
    oc_scr[rows, :GDN_WIDTH] = jnp.concatenate(o_rows, axis=0)

    @pl.when(i == pl.num_programs(0) - 1)
    def _():
        y = jnp.dot(oc_scr[...].astype(bf16), wout_ref[...], preferred_element_type=f32)
        o_ref[...] = _ln(DN_ALPHA * x_ref[...] + y, g_ref[MIX_LN_ROW:MIX_LN_ROW + 1, :], b_ref[MIX_LN_ROW:MIX_LN_ROW + 1, :])


def _mixer0_dec_ln(x, s, dnc, ccc, wqz, wglu, wba, dnw, alog, dtb, ng, ccw, ccb, cclg, cclb, wout, g, b, *, l, bs=8):
    N, D = x.shape
    consts = [wqz, wglu, wba, dnw, alog, dtb, ng, ccw, ccb, cclg, cclb, wout]
    sspec = pl.BlockSpec((bs, GDN_HEADS, GDN_DK, GDN_DV), lambda i: (i, 0, 0, 0))
    return pl.pallas_call(
        functools.partial(_mixer0_dec_kernel, bs=bs),
        grid=(N // bs,),
        in_specs=[_full((N, D)), sspec, _full(dnc.shape), _full(ccc.shape)] + [_full(c.shape) for c in consts]
        + [_layer_spec(g.shape, l), _layer_spec(b.shape, l)],
        out_specs=[_full((N, D)), sspec, _full(dnc.shape), _full(ccc.shape)],
        out_shape=[jax.ShapeDtypeStruct((N, D), f32), jax.ShapeDtypeStruct(s.shape, f32),
                   jax.ShapeDtypeStruct(dnc.shape, f32), jax.ShapeDtypeStruct(ccc.shape, f32)],
        scratch_shapes=[pltpu.VMEM((N, GDN_WIDTH), f32)] * 4 + [pltpu.VMEM((N, LANES), f32)] * 2
        + [pltpu.VMEM((N, GDN_WIDTH + CC_CH), f32)],
        compiler_params=_cparams("arbitrary"),
        name="mixer0_dec_ln",
    )(x, s, dnc, ccc, *consts, g, b)


def _pad_lanes(v, n=LANES):
    return jnp.zeros((1, n), f32).at[0, :v.shape[0]].set(v.astype(f32))


def kernel(x_prompt, x_sample, mem_prompt, cache_mem_k, cache_mem_v, state_dn_S, state_dn_conv, state_cc_conv, ln_g, ln_b, ffn_w_gate, ffn_w_up, ffn_w_down, xa_wq, xa_wk, xa_wv, xa_wo, ab_w_in, dn_conv_w, dn_A_log, dn_dt_bias, dn_norm_g, cc_conv_w, cc_conv_b, cc_ln_g, cc_ln_b, ab_w_out, gm_w_in, gm_ln_g, gm_ln_b, gm_w_s, gm_b_s, gm_w_out):
    B, SEQ, D = x_prompt.shape
    N = x_sample.shape[0]
    row = lambda v: v.reshape(1, -1).astype(f32)

    wg, wu, wd = ffn_w_gate.astype(bf16), ffn_w_up.astype(bf16), ffn_w_down.astype(bf16)
    wq, wk, wv, wo = xa_wq.astype(bf16), xa_wk.astype(bf16), xa_wv.astype(bf16), xa_wo.astype(bf16)
    nq = 3 * GDN_WIDTH
    nz = nq + GDN_WIDTH
    w_qz, w_glu = ab_w_in[:, :nz].astype(bf16), ab_w_in[:, nz + 2 * GDN_HEADS:].astype(bf16)
    w_ba = jnp.zeros((D, 2 * LANES), f32)
    w_ba = w_ba.at[:, :GDN_HEADS].set(ab_w_in[:, nz:nz + GDN_HEADS])
    w_ba = w_ba.at[:, LANES:LANES + GDN_HEADS].set(ab_w_in[:, nz + GDN_HEADS:nz + 2 * GDN_HEADS]).astype(bf16)
    w_out = ab_w_out.astype(bf16)
    gw_in, gw_out = gm_w_in.astype(bf16), gm_w_out.astype(bf16)
    mixer_consts = (w_qz, w_glu, w_ba, dn_conv_w.astype(f32), _pad_lanes(dn_A_log), _pad_lanes(dn_dt_bias), row(dn_norm_g),
                    cc_conv_w.astype(f32), row(cc_conv_b), row(cc_ln_g), row(cc_ln_b), w_out)
    gm_consts = (gw_in, gm_w_s.astype(f32), gm_b_s.astype(f32)[:, :, None], row(gm_ln_g), row(gm_ln_b), gw_out)
    lng, lnb = ln_g.astype(f32), ln_b.astype(f32)

    mem_k, mem_v, mem_kh, mem_vh = _mem_kv(mem_prompt, wk, wv)

    yp, ys = x_prompt.reshape(B * SEQ, D), x_sample.reshape(N, D)
    state = (state_dn_S, jnp.swapaxes(state_dn_conv, 0, 1), jnp.swapaxes(state_cc_conv, 0, 1))
    for l in range(DEPTH):
        yp, ys = _ffn_ln(yp, wg, wu, wd, lng, lnb, l=l, half=0, tm=FFN_ROWS, xs=ys)
        if l % 2 == 0:
            yp, dn_s_p, dn_c_p, cc_c_p = _mixer0_ln(yp, *mixer_consts, lng, lnb, l=l, batch=B, seq=SEQ)
            ys, dn_s_s, dn_c_s, cc_c_s = _mixer0_dec_ln(ys, *state, *mixer_consts, lng, lnb, l=l)
        else:
            yp, _ = _gmlp_ln(yp, *gm_consts, lng, lnb, l=l, tm=GMLP_ROWS, single=False)
            ys, gm_v_s = _gmlp_ln(ys, *gm_consts, lng, lnb, l=l, tm=N, single=True)
        yp = _xattn_ln(yp, mem_kh, mem_vh, wq, wo, lng, lnb, l=l, seq=SEQ)
        yp, ys = _ffn_ln(yp, wg, wu, wd, lng, lnb, l=l, half=1, tm=FFN_ATTEND_ROWS, xs=ys,
                         attend=(cache_mem_k, cache_mem_v, wq, wo))

    return (yp.reshape(B, SEQ, D), ys.reshape(N, 1, D), mem_k, mem_v,
            dn_s_p, dn_c_p[:, DN_TAIL - (SCONV_W - 1):], cc_c_p[:, CC_TAIL - (CC_W - 1):],
            dn_s_s, jnp.swapaxes(dn_c_s, 0, 1), jnp.swapaxes(cc_c_s, 0, 1), gm_v_s.reshape(N, 1, GM_WIDTH))
```
